```python
import math
import jax, jax.numpy as jnp
from jax import lax
import numpy as np

D_MODEL = 1024
BATCH = 4
SEQ = 4096
DEPTH = 1
DEC_BATCH = 128
DEC_SEQ = 8
PAST_LEN = 8192
PAGE_SIZE = 128

MEM_LEN = 256
RWKV_WIDTH = D_MODEL // 2
RWKV_HEAD_DIM = 64
RWKV_HEADS = RWKV_WIDTH // RWKV_HEAD_DIM
LORA_DECAY = 64
LORA_A = 64
LORA_GATE = 128
RWKV_COLS = 3 * RWKV_WIDTH + LORA_DECAY + LORA_A + LORA_GATE
DIFF_WIDTH = D_MODEL - RWKV_WIDTH
DIFF_HEADS = 4
DIFF_V_DIM = DIFF_WIDTH // DIFF_HEADS
DIFF_QK_DIM = DIFF_V_DIM // 2
DIFF_COLS = 3 * DIFF_WIDTH
IN_COLS = RWKV_COLS + DIFF_COLS
MEM_HEADS = 4
MEM_HEAD_DIM = 128
MEM_WIDTH = MEM_HEADS * MEM_HEAD_DIM
D_FF = -(-8 * D_MODEL // (3 * 256)) * 256
ROPE_THETA = 10000.0
RMS_EPS = 1e-6
LNX_EPS = 1e-5 * RWKV_HEAD_DIM
Q_BLOCK = 128

kernel_name = 'hymba_rwkv7_diffattn_memxattn_step'


def _lambda_init(layer):
    return 0.8 - 0.6 * math.exp(-0.3 * layer)


def _rms(x, g):
    xf = x.astype(jnp.float32)
    y = xf * lax.rsqrt(jnp.mean(xf * xf, axis=-1, keepdims=True) + RMS_EPS) * g.astype(jnp.float32)
    return y.astype(x.dtype)


def _rope(x, pos):
    half = x.shape[-1] // 2
    inv = ROPE_THETA ** (-jnp.arange(half, dtype=jnp.float32) / half)
    ang = pos.astype(jnp.float32)[:, None] * inv[None, :]
    shape = (1, pos.shape[0]) + (1,) * (x.ndim - 3) + (half,)
    cos, sin = jnp.cos(ang).reshape(shape), jnp.sin(ang).reshape(shape)
    xf = x.astype(jnp.float32)
    x1, x2 = xf[..., :half], xf[..., half:]
    return jnp.concatenate([x1 * cos - x2 * sin, x2 * cos + x1 * sin], axis=-1).astype(x.dtype)


def _rwkv_time_mix(p, prev_row, s0, mu, w0, w2, a0, a2, g2, k_k, k_a, r_k, lnx_w, lnx_b):
    f32 = jnp.float32
    B, T, _ = p.shape
    shifted = jnp.concatenate([prev_row[:, None].astype(p.dtype), p[:, :-1]], axis=1)
    ps = p + (shifted - p) * mu
    W = RWKV_WIDTH
    r, k, v, wd, ad, gd = jnp.split(ps, [W, 2 * W, 3 * W, 3 * W + LORA_DECAY, 3 * W + LORA_DECAY + LORA_A], axis=-1)
    w_log = -jax.nn.softplus(-(w0 + jnp.tanh(wd) @ w2).astype(f32)) - 0.5
    decay = jnp.exp(-jnp.exp(w_log))
    a = jax.nn.sigmoid((a0 + ad @ a2).astype(f32))
    g = jax.nn.sigmoid(gd) @ g2
    heads = lambda z: z.reshape(B, T, RWKV_HEADS, RWKV_HEAD_DIM)
    kk = heads((k * k_k).astype(f32))
    kk = kk / jnp.maximum(jnp.sqrt(jnp.sum(kk * kk, axis=-1, keepdims=True)), 1e-12)
    k = heads(k.astype(f32) * (1.0 + (a - 1.0) * k_a.astype(f32)))
    r32, v32 = heads(r.astype(f32)), heads(v.astype(f32))
    xs = (r32, heads(decay), k, v32, -kk, kk * heads(a))
    xs = tuple(jnp.swapaxes(z, 0, 1) for z in xs)

    def step(S, inp):
        r_t, w_t, k_t, v_t, a_t, b_t = inp
        sa = jnp.einsum('bhvk,bhk->bhv', S, a_t)
        S = S * w_t[:, :, None, :] + sa[..., None] * b_t[:, :, None, :] + v_t[..., None] * k_t[:, :, None, :]
        return S, jnp.einsum('bhvk,bhk->bhv', S, r_t)

    sT, y = lax.scan(step, s0.astype(f32), xs)
    y = jnp.swapaxes(y, 0, 1)
    mean = jnp.mean(y, axis=-1, keepdims=True)
    var = jnp.mean((y - mean) ** 2, axis=-1, keepdims=True)
    y = ((y - mean) * lax.rsqrt(var + LNX_EPS)).reshape(B, T, W) * lnx_w.astype(f32) + lnx_b.astype(f32)
    bonus = jnp.sum(r32 * k * r_k.astype(f32), axis=-1, keepdims=True) * v32
    out = (y + bonus.reshape(B, T, W)) * g.astype(f32)
    return out.astype(p.dtype), p[:, -1], sT


def _diff_qkv(pd, pos, q_norm, k_norm):
    B, T, _ = pd.shape
    q, k, v = jnp.split(pd, [DIFF_WIDTH, 2 * DIFF_WIDTH], axis=-1)
    q = _rope(_rms(q.reshape(B, T, DIFF_HEADS, 2, DIFF_QK_DIM), q_norm), pos)
    k = _rope(_rms(k.reshape(B, T, DIFF_HEADS, 2, DIFF_QK_DIM), k_norm), pos)
    return q, k.reshape(B, T, DIFF_HEADS, DIFF_V_DIM), v.reshape(B, T, DIFF_HEADS, DIFF_V_DIM)


def _diff_attend(q, k, v, mask, lam):
    k = k.reshape(k.shape[:-1] + (2, DIFF_QK_DIM))
    s = jnp.einsum('bqhcd,bkhcd->bhcqk', q, k).astype(jnp.float32) * (DIFF_QK_DIM ** -0.5)
    s = jnp.where(mask, s, -jnp.inf)
    prob = jax.nn.softmax(s, axis=-1)
    attn = prob[:, :, 0] - lam * prob[:, :, 1]
    return jnp.einsum('bhqk,bkhd->bqhd', attn.astype(v.dtype), v)


def _diff_prompt(q, k, v, lam):
    B, T = q.shape[:2]
    nb = T // Q_BLOCK
    qb = jnp.moveaxis(q.reshape((B, nb, Q_BLOCK) + q.shape[2:]), 1, 0)
    starts = jnp.arange(nb, dtype=jnp.int32) * Q_BLOCK
    kpos = jnp.arange(T, dtype=jnp.int32)

    def blk(args):
        qi, s0 = args
        mask = (s0 + jnp.arange(Q_BLOCK, dtype=jnp.int32))[:, None] >= kpos[None, :]
        return _diff_attend(qi, k, v, mask, lam)

    o = lax.map(blk, (qb, starts))
    return jnp.moveaxis(o, 0, 1).reshape(B, T, DIFF_HEADS, DIFF_V_DIM)


def _diff_sample(q, k_new, v_new, pool_k, pool_v, page_table, lam):
    S = q.shape[1]
    past = page_table.shape[1] * PAGE_SIZE
    kpos = jnp.arange(past + S, dtype=jnp.int32)
    qpos = past + jnp.arange(S, dtype=jnp.int32)
    mask = kpos[None, :] <= qpos[:, None]

    def one(args):
        qi, kn, vn, pages = args
        kp = pool_k[pages].reshape(past, DIFF_HEADS, DIFF_V_DIM)
        vp = pool_v[pages].reshape(past, DIFF_HEADS, DIFF_V_DIM)
        kc = jnp.concatenate([kp, kn.astype(kp.dtype)], axis=0)[None]
        vc = jnp.concatenate([vp, vn.astype(vp.dtype)], axis=0)[None]
        return _diff_attend(qi[None], kc, vc, mask, lam)[0]

    return lax.map(one, (q, k_new, v_new, page_table))


def _mem_kv(mem, g, w_mkv, k_norm):
    B, M, _ = mem.shape
    k, v = jnp.split(_rms(mem, g) @ w_mkv, 2, axis=-1)
    k = _rms(k.reshape(B, M, MEM_HEADS, MEM_HEAD_DIM), k_norm)
    return k, v.reshape(B, M, MEM_HEADS, MEM_HEAD_DIM)


def _mem_attend(h, mk, mv, w_mq, q_norm, w_mo):
    B, T, _ = h.shape
    q = _rms((h @ w_mq).reshape(B, T, MEM_HEADS, MEM_HEAD_DIM), q_norm)
    s = jnp.einsum('bqhd,bkhd->bhqk', q, mk.astype(q.dtype)).astype(jnp.float32) * (MEM_HEAD_DIM ** -0.5)
    prob = jax.nn.softmax(s, axis=-1)
    o = jnp.einsum('bhqk,bkhd->bqhd', prob.astype(mv.dtype), mv)
    return o.reshape(B, T, MEM_WIDTH).astype(h.dtype) @ w_mo


def _swiglu(h, w_gate_up, w_down):
    gt, up = jnp.split(h @ w_gate_up, 2, axis=-1)
    return (jax.nn.silu(gt) * up) @ w_down


def setup_inputs(seed: int = 0) -> dict:
    f32 = jnp.float32
    key = jax.random.key(seed)
    kit = iter(jax.random.split(key, 64))
    nrm = lambda shape, scale: jax.random.normal(next(kit), shape, f32) * scale
    gain = lambda shape: 1.0 + nrm(shape, 0.02)
    n_pages = PAST_LEN // PAGE_SIZE
    n_used = DEC_BATCH * n_pages
    n_phys = n_used + n_used // 4
    L = DEPTH
    page_table = jax.random.permutation(next(kit), n_phys)[:n_used].reshape(DEC_BATCH, n_pages).astype(jnp.int32)
    return {
        'x_prompt': nrm((BATCH, SEQ, D_MODEL), 1.0),
        'x_sample': nrm((DEC_BATCH, DEC_SEQ, D_MODEL), 1.0),
        'cache_diff_k': nrm((L, n_phys, PAGE_SIZE, DIFF_HEADS, DIFF_V_DIM), 1.0),
        'cache_diff_v': nrm((L, n_phys, PAGE_SIZE, DIFF_HEADS, DIFF_V_DIM), 1.0),
        'cache_mem_k': nrm((L, DEC_BATCH, MEM_LEN, MEM_HEADS, MEM_HEAD_DIM), 1.0),
        'cache_mem_v': nrm((L, DEC_BATCH, MEM_LEN, MEM_HEADS, MEM_HEAD_DIM), 1.0),
        'state_rwkv': nrm((L, DEC_BATCH, RWKV_HEADS, RWKV_HEAD_DIM, RWKV_HEAD_DIM), 1.0),
        'state_shift': nrm((L, DEC_BATCH, RWKV_COLS), 1.0),
        'page_table': page_table,
        'mem_prompt': nrm((BATCH, MEM_LEN, D_MODEL), 1.0),
        'norm_mix': gain((L, D_MODEL)),
        'w_in': nrm((L, D_MODEL, IN_COLS), D_MODEL ** -0.5),
        'rwkv_mu': jax.random.uniform(next(kit), (L, RWKV_COLS), f32),
        'rwkv_w0': jax.random.uniform(next(kit), (L, RWKV_WIDTH), f32, -3.0, 1.0),
        'rwkv_w2': nrm((L, LORA_DECAY, RWKV_WIDTH), 0.1 * LORA_DECAY ** -0.5),
        'rwkv_a0': nrm((L, RWKV_WIDTH), 0.1),
        'rwkv_a2': nrm((L, LORA_A, RWKV_WIDTH), 0.1 * LORA_A ** -0.5),
        'rwkv_g2': nrm((L, LORA_GATE, RWKV_WIDTH), LORA_GATE ** -0.5),
        'rwkv_k_k': 0.85 + nrm((L, RWKV_WIDTH), 0.02),
        'rwkv_k_a': gain((L, RWKV_WIDTH)),
        'rwkv_r_k': nrm((L, RWKV_HEADS, RWKV_HEAD_DIM), 0.1),
        'rwkv_lnx_w': gain((L, RWKV_WIDTH)),
        'rwkv_lnx_b': nrm((L, RWKV_WIDTH), 0.01),
        'diff_q_norm': gain((L, 2, DIFF_QK_DIM)),
        'diff_k_norm': gain((L, 2, DIFF_QK_DIM)),
        'diff_lam_q1': nrm((L, DIFF_QK_DIM), 0.1),
        'diff_lam_k1': nrm((L, DIFF_QK_DIM), 0.1),
        'diff_lam_q2': nrm((L, DIFF_QK_DIM), 0.1),
        'diff_lam_k2': nrm((L, DIFF_QK_DIM), 0.1),
        'diff_sub_norm': gain((L, DIFF_V_DIM)),
        'w_o': nrm((L, D_MODEL, D_MODEL), D_MODEL ** -0.5),
        'norm_mem': gain((L, D_MODEL)),
        'norm_memkv': gain((L, D_MODEL)),
        'w_mq': nrm((L, D_MODEL, MEM_WIDTH), D_MODEL ** -0.5),
        'w_mkv': nrm((L, D_MODEL, 2 * MEM_WIDTH), D_MODEL ** -0.5),
        'mem_q_norm': gain((L, MEM_HEAD_DIM)),
        'mem_k_norm': gain((L, MEM_HEAD_DIM)),
        'w_mo': nrm((L, MEM_WIDTH, D_MODEL), MEM_WIDTH ** -0.5),
        'norm_ffn': gain((L, D_MODEL)),
        'w_gate_up': nrm((L, D_MODEL, 2 * D_FF), D_MODEL ** -0.5),
        'w_down': nrm((L, D_FF, D_MODEL), D_FF ** -0.5),
    }


def reference(x_prompt, x_sample, cache_diff_k, cache_diff_v, cache_mem_k, cache_mem_v, state_rwkv, state_shift,
              page_table, mem_prompt, norm_mix, w_in, rwkv_mu, rwkv_w0, rwkv_w2, rwkv_a0, rwkv_a2, rwkv_g2,
              rwkv_k_k, rwkv_k_a, rwkv_r_k, rwkv_lnx_w, rwkv_lnx_b, diff_q_norm, diff_k_norm, diff_lam_q1,
              diff_lam_k1, diff_lam_q2, diff_lam_k2, diff_sub_norm, w_o, norm_mem, norm_memkv, w_mq, w_mkv,
              mem_q_norm, mem_k_norm, w_mo, norm_ffn, w_gate_up, w_down):
    f32 = jnp.float32
    B, T, _ = x_prompt.shape
    Bs, S, _ = x_sample.shape
    past = page_table.shape[1] * PAGE_SIZE
    pos_p = jnp.arange(T, dtype=jnp.int32)
    pos_s = past + jnp.arange(S, dtype=jnp.int32)
    xp, xs = x_prompt, x_sample
    dkp, dvp, mkp, mvp, rsp, shp = [], [], [], [], [], []
    dks, dvs, rss, shs = [], [], [], []
    for l in range(DEPTH):
        lam_init = _lambda_init(l)
        lam = (jnp.exp(jnp.sum(diff_lam_q1[l].astype(f32) * diff_lam_k1[l].astype(f32)))
               - jnp.exp(jnp.sum(diff_lam_q2[l].astype(f32) * diff_lam_k2[l].astype(f32))) + lam_init)
        rw = (rwkv_mu[l], rwkv_w0[l], rwkv_w2[l], rwkv_a0[l], rwkv_a2[l], rwkv_g2[l], rwkv_k_k[l],
              rwkv_k_a[l], rwkv_r_k[l], rwkv_lnx_w[l], rwkv_lnx_b[l])

        def mix(x, pos, prev, s0, attend):
            p = _rms(x, norm_mix[l]) @ w_in[l]
            r_out, last_row, sT = _rwkv_time_mix(p[..., :RWKV_COLS], prev, s0, *rw)
            q, k, v = _diff_qkv(p[..., RWKV_COLS:], pos, diff_q_norm[l], diff_k_norm[l])
            o = _rms(attend(q, k, v), diff_sub_norm[l]) * (1.0 - lam_init)
            o = o.reshape(o.shape[:2] + (DIFF_WIDTH,)).astype(r_out.dtype)
            return x + jnp.concatenate([r_out, o], axis=-1) @ w_o[l], last_row, sT, k, v

        def tail(x, mk, mv):
            x = x + _mem_attend(_rms(x, norm_mem[l]), mk, mv, w_mq[l], mem_q_norm[l], w_mo[l])
            return x + _swiglu(_rms(x, norm_ffn[l]), w_gate_up[l], w_down[l])

        mk_p, mv_p = _mem_kv(mem_prompt, norm_memkv[l], w_mkv[l], mem_k_norm[l])
        prev0 = jnp.zeros((B, RWKV_COLS), xp.dtype)
        s00 = jnp.zeros((B, RWKV_HEADS, RWKV_HEAD_DIM, RWKV_HEAD_DIM), f32)
        xp, last_p, sT_p, k_p, v_p = mix(xp, pos_p, prev0, s00, lambda q, k, v: _diff_prompt(q, k, v, lam))
        xp = tail(xp, mk_p, mv_p)
        xs, last_s, sT_s, k_s, v_s = mix(
            xs, pos_s, state_shift[l], state_rwkv[l],
            lambda q, k, v: _diff_sample(q, k, v, cache_diff_k[l], cache_diff_v[l], page_table, lam))
        xs = tail(xs, cache_mem_k[l], cache_mem_v[l])
        dkp.append(k_p); dvp.append(v_p); mkp.append(mk_p); mvp.append(mv_p)
        rsp.append(sT_p.astype(xp.dtype)); shp.append(last_p)
        dks.append(k_s); dvs.append(v_s); rss.append(sT_s.astype(state_rwkv.dtype)); shs.append(last_s)
    return (xp, xs, jnp.stack(dkp), jnp.stack(dvp), jnp.stack(mkp), jnp.stack(mvp), jnp.stack(rsp),
            jnp.stack(shp), jnp.stack(dks), jnp.stack(dvs), jnp.stack(rss), jnp.stack(shs))
```

```python
import functools
import math

import jax
import jax.numpy as jnp
from jax import lax
from jax.experimental import pallas as pl
from jax.experimental.pallas import tpu as pltpu

F32 = jnp.float32
BF16 = jnp.bfloat16

LANES = 128
HEAD = 64
RW_WIDTH = 512
RW_COLS = 3 * RW_WIDTH + 64 + 64 + 128
DIFF_WIDTH = 512
N_DIFF_HEADS = 4
MEM_WIDTH = 512
N_MEM_HEADS = 4
PAGE = 128
ROPE_THETA = 10000.0
RMS_EPS = 1e-6
LNX_EPS = 1e-5 * HEAD
NEG_BIG = -1e30
VMEM_LIMIT = 56 * 1024 * 1024


def _cparams(sem):
    return pltpu.CompilerParams(dimension_semantics=sem, vmem_limit_bytes=VMEM_LIMIT)


def _const_spec(shape):
    nd = len(shape)
    return pl.BlockSpec(shape, lambda *_: (0,) * nd)


def _nt_dot(a, b):
    return lax.dot_general(a, b, (((1,), (1,)), ((), ())), preferred_element_type=F32)


def _rms_rows(x, g):
    ms = jnp.mean(x * x, axis=-1, keepdims=True)
    return x * lax.rsqrt(ms + RMS_EPS) * g


def _lane_iota(shape):
    return lax.broadcasted_iota(jnp.int32, shape, len(shape) - 1)


def _rms_half(xh, gain):
    lo = _lane_iota(xh.shape) < HEAD
    x2 = xh * xh
    s0 = jnp.sum(jnp.where(lo, x2, 0.0), axis=-1, keepdims=True)
    s1 = jnp.sum(jnp.where(lo, 0.0, x2), axis=-1, keepdims=True)
    inv = jnp.where(lo, lax.rsqrt(s0 * (1.0 / HEAD) + RMS_EPS), lax.rsqrt(s1 * (1.0 / HEAD) + RMS_EPS))
    return xh * inv * gain


def _rope_block(xh, cos, sin_signed):
    first = (_lane_iota(xh.shape) & 32) == 0
    partner = jnp.where(first, pltpu.roll(xh, LANES - 32, 1), pltpu.roll(xh, 32, 1))
    return xh * cos + partner * sin_signed


def _rope_tables(pos):
    half = HEAD // 2
    inv = ROPE_THETA ** (-jnp.arange(half, dtype=F32) / half)
    ang = pos.astype(F32)[:, None] * inv[None, :]
    cos, sin = jnp.cos(ang), jnp.sin(ang)
    return jnp.tile(cos, (1, 4)), jnp.concatenate([-sin, sin, -sin, sin], axis=1)


def _split_hi_lo(x):
    hi = x.astype(BF16)
    lo = (x - hi.astype(F32)).astype(BF16)
    return jnp.concatenate([hi, lo], axis=1)


def _group_ones2():
    r = lax.broadcasted_iota(jnp.int32, (2 * LANES, LANES), 0)
    c = lax.broadcasted_iota(jnp.int32, (2 * LANES, LANES), 1)
    return jnp.where(((r & (LANES - 1)) >> 6) == (c >> 6), 1.0, 0.0).astype(BF16)


def _group_sum(x, g2):
    return jnp.dot(_split_hi_lo(x), g2, preferred_element_type=F32)


def _group_sum_wide(x, g2):
    return jnp.concatenate(
        [_group_sum(x[:, c * LANES:(c + 1) * LANES], g2) for c in range(x.shape[1] // LANES)], axis=1)


def _sigmoid(x):
    return 1.0 / (1.0 + jnp.exp(-x))


def _in_proj_kernel(x_ref, g_ref, w_ref, qn_ref, kn_ref, cos_ref, sin_ref,
                    prw_ref, qb_ref, k_ref, v_ref, kb_ref, vb_ref):
    h = _rms_rows(x_ref[...], g_ref[...]).astype(BF16)
    prw_ref[...] = jnp.dot(h, w_ref[:, :RW_COLS], preferred_element_type=F32)
    pd = jnp.dot(h, w_ref[:, RW_COLS:], preferred_element_type=F32)
    cos, sin = cos_ref[...], sin_ref[...]
    qn, kn = qn_ref[...], kn_ref[...]
    for hh in range(N_DIFF_HEADS):
        sl = slice(hh * LANES, (hh + 1) * LANES)
        q = _rope_block(_rms_half(pd[:, sl], qn), cos, sin) * (HEAD ** -0.5)
        k = _rope_block(_rms_half(pd[:, DIFF_WIDTH + hh * LANES:DIFF_WIDTH + (hh + 1) * LANES], kn), cos, sin)
        qb_ref[:, sl] = q.astype(BF16)
        k_ref[:, sl] = k
        kb_ref[:, sl] = k.astype(BF16)
    v = pd[:, 2 * DIFF_WIDTH:]
    v_ref[...] = v
    vb_ref[...] = v.astype(BF16)


def _in_proj(x, g, w_bf, qn, kn, cos, sin, tm):
    n, d = x.shape
    period = cos.shape[0] // tm
    row = lambda i: (i, 0)
    outs = (jax.ShapeDtypeStruct((n, RW_COLS), F32), jax.ShapeDtypeStruct((n, DIFF_WIDTH), BF16),
            jax.ShapeDtypeStruct((n, DIFF_WIDTH), F32), jax.ShapeDtypeStruct((n, DIFF_WIDTH), F32),
            jax.ShapeDtypeStruct((n, DIFF_WIDTH), BF16), jax.ShapeDtypeStruct((n, DIFF_WIDTH), BF16))
    return pl.pallas_call(
        _in_proj_kernel,
        grid=(n // tm,),
        in_specs=[pl.BlockSpec((tm, d), row), _const_spec((1, d)), _const_spec(w_bf.shape),
                  _const_spec((1, LANES)), _const_spec((1, LANES)),
                  pl.BlockSpec((tm, LANES), lambda i: (i % period, 0)),
                  pl.BlockSpec((tm, LANES), lambda i: (i % period, 0))],
        out_specs=(pl.BlockSpec((tm, RW_COLS), row),) + (pl.BlockSpec((tm, DIFF_WIDTH), row),) * 5,
        out_shape=outs,
        compiler_params=_cparams(("parallel",)),
        name="in_proj",
    )(x, g, w_bf, qn, kn, cos, sin)


def _rwkv_kernel(p_ref, prev_ref, s0_ref, mu_ref, w0_ref, w2_ref, a0_ref, a2_ref, g2_ref,
                 kk_ref, ka_ref, rk_ref, lw_ref, lb_ref,
                 out_ref, st_ref,
                 state, carry, r_s, w_s, k_s, v_s, a_s, b_s, y_s, bonus_s, gate_s, *, ns, tt):
    ti = pl.program_id(1)
    npair = RW_WIDTH // LANES
    g2 = _group_ones2()

    @pl.when(ti == 0)
    def _():
        state[...] = s0_ref[...]
        carry[...] = prev_ref[...]

    row0 = lax.broadcasted_iota(jnp.int32, (tt, RW_COLS), 0) == 0
    for s in range(ns):
        p = p_ref[s]
        shifted = jnp.where(row0, carry[s], pltpu.roll(p, 1, 0))
        carry[s] = p[tt - 1:tt, :]
        ps = p + (shifted - p) * mu_ref[...]
        r = ps[:, :RW_WIDTH]
        k = ps[:, RW_WIDTH:2 * RW_WIDTH]
        v = ps[:, 2 * RW_WIDTH:3 * RW_WIDTH]
        wa = ps[:, 3 * RW_WIDTH:3 * RW_WIDTH + LANES]
        gd = ps[:, 3 * RW_WIDTH + LANES:]
        z = -(w0_ref[...] + jnp.dot(jnp.tanh(wa).astype(BF16), w2_ref[...], preferred_element_type=F32))
        w_log = -(jnp.maximum(z, 0.0) + jnp.log1p(jnp.exp(-jnp.abs(z)))) - 0.5
        decay = jnp.exp(-jnp.exp(w_log))
        a = _sigmoid(a0_ref[...] + jnp.dot(wa.astype(BF16), a2_ref[...], preferred_element_type=F32))
        gate = jnp.dot(_sigmoid(gd).astype(BF16), g2_ref[...], preferred_element_type=F32)
        kk = k * kk_ref[...]
        kk = kk / jnp.maximum(jnp.sqrt(_group_sum_wide(kk * kk, g2)), 1e-12)
        k2 = k * (1.0 + (a - 1.0) * ka_ref[...])
        r_s[s] = r
        w_s[s] = decay
        k_s[s] = k2
        v_s[s] = v
        a_s[s] = -kk
        b_s[s] = kk * a
        bonus_s[s] = _group_sum_wide(r * k2 * rk_ref[...], g2) * v
        gate_s[s] = gate

    tiles = [(s, j) for s in range(ns) for j in range(npair)]
    nt = len(tiles)
    sub = lax.broadcasted_iota(jnp.int32, (nt * HEAD, LANES), 0)
    e1 = jnp.where((_lane_iota((nt * HEAD, LANES)) & (HEAD - 1)) == (sub & (HEAD - 1)), 1.0, 0.0)
    seqs = dict(r=r_s, w=w_s, k=k_s, v=v_s, a=a_s, b=b_s)

    def block(tb, c):
        base = pl.multiple_of(tb * 8, 8)
        rows8 = {q: [ref[s, pl.ds(base, 8), j * LANES:(j + 1) * LANES] for (s, j) in tiles] for q, ref in seqs.items()}
        ys = []
        for i in range(8):
            bc = lambda q: jnp.concatenate(
                [jnp.broadcast_to(rows8[q][n][i:i + 1], (HEAD, LANES)) for n in range(nt)], axis=0)
            st = state[...]
            red = jnp.dot(jnp.concatenate([_split_hi_lo(st * bc("a")), _split_hi_lo(e1 * bc("v"))], axis=0), g2,
                          preferred_element_type=F32)
            st = st * bc("w") + red[:nt * HEAD] * bc("b") + red[nt * HEAD:] * bc("k")
            state[...] = st
            yred = jnp.dot(_split_hi_lo(st * bc("r")), g2, preferred_element_type=F32)
            ys.append(jnp.sum((e1 * yred).reshape(nt, HEAD, LANES), axis=1))
        for n, (s, j) in enumerate(tiles):
            y_s[s, pl.ds(base, 8), j * LANES:(j + 1) * LANES] = jnp.concatenate([y[n:n + 1] for y in ys], axis=0)
        return c

    lax.fori_loop(0, tt // 8, block, 0)

    for s in range(ns):
        y = y_s[s]
        mean = _group_sum_wide(y, g2) * (1.0 / HEAD)
        d = y - mean
        var = _group_sum_wide(d * d, g2) * (1.0 / HEAD)
        yn = d * lax.rsqrt(var + LNX_EPS) * lw_ref[...] + lb_ref[...]
        out_ref[s] = (yn + bonus_s[s]) * gate_s[s]

    @pl.when(ti == pl.num_programs(1) - 1)
    def _():
        st_ref[...] = state[...]


def _rwkv(prw, prev, s0_pairs, rw, ns, tt):
    S, T, _ = prw.shape
    npair = RW_WIDTH // LANES
    vec = lambda n: _const_spec((1, n))
    blk = lambda w: pl.BlockSpec((ns, tt, w), lambda i, t: (i, t, 0))
    st_spec = pl.BlockSpec((ns * npair * HEAD, LANES), lambda i, t: (i, 0))
    seq = lambda: pltpu.VMEM((ns, tt, RW_WIDTH), F32)
    return pl.pallas_call(
        functools.partial(_rwkv_kernel, ns=ns, tt=tt),
        grid=(S // ns, T // tt),
        in_specs=[blk(RW_COLS), pl.BlockSpec((ns, 1, RW_COLS), lambda i, t: (i, 0, 0)), st_spec,
                  vec(RW_COLS), vec(RW_WIDTH), _const_spec((LANES, RW_WIDTH)), vec(RW_WIDTH),
                  _const_spec((LANES, RW_WIDTH)), _const_spec((LANES, RW_WIDTH)),
                  vec(RW_WIDTH), vec(RW_WIDTH), vec(RW_WIDTH), vec(RW_WIDTH), vec(RW_WIDTH)],
        out_specs=(blk(RW_WIDTH), st_spec),
        out_shape=(jax.ShapeDtypeStruct((S, T, RW_WIDTH), F32),
                   jax.ShapeDtypeStruct((S * npair * HEAD, LANES), F32)),
        scratch_shapes=[pltpu.VMEM((ns * npair * HEAD, LANES), F32), pltpu.VMEM((ns, 1, RW_COLS), F32)]
                       + [seq() for _ in range(9)],
        compiler_params=_cparams(("parallel", "arbitrary")),
        name="rwkv",
    )(prw, prev, s0_pairs, *rw)


def _lambda(lamp_ref, lam_init):
    lp = lamp_ref[...]
    s1 = jnp.sum(lp[0:1] * lp[1:2], axis=-1, keepdims=True)
    s2 = jnp.sum(lp[2:3] * lp[3:4], axis=-1, keepdims=True)
    return jnp.exp(s1) - jnp.exp(s2) + lam_init


def _stack_halves(q):
    lo = _lane_iota(q.shape) < HEAD
    zero = jnp.zeros_like(q)
    return jnp.concatenate([jnp.where(lo, q, zero), jnp.where(lo, zero, q)], axis=0)


def _diff_finish(acc, l, lam, sn, lam_init, rows):
    o = acc[:rows] / l[:rows] - lam * (acc[rows:] / l[rows:])
    return _rms_rows(o, sn) * (1.0 - lam_init)


def _diff_prompt_kernel(q_ref, k_ref, v_ref, lamp_ref, sn_ref, o_ref, m_ref, l_ref, acc_ref, *, tq, lam_init):
    i = pl.program_id(2)
    q2 = _stack_halves(q_ref[0])
    start = pl.multiple_of(i * tq, tq)
    s = _nt_dot(q2, k_ref[0, pl.ds(start, tq), :])
    row = lax.broadcasted_iota(jnp.int32, s.shape, 0)
    col = lax.broadcasted_iota(jnp.int32, s.shape, 1)
    s = jnp.where(col <= jnp.where(row >= tq, row - tq, row), s, NEG_BIG)
    m = jnp.max(s, axis=-1, keepdims=True)
    p = jnp.exp(s - m)
    m_ref[...] = m
    l_ref[...] = jnp.sum(p, axis=-1, keepdims=True)
    acc_ref[...] = jnp.dot(p.astype(BF16), v_ref[0, pl.ds(start, tq), :], preferred_element_type=F32)

    def body(j, c):
        off = pl.multiple_of(j * tq, tq)
        s = _nt_dot(q2, k_ref[0, pl.ds(off, tq), :])
        m_old = m_ref[...]
        m_new = jnp.maximum(m_old, jnp.max(s, axis=-1, keepdims=True))
        alpha = jnp.exp(m_old - m_new)
        p = jnp.exp(s - m_new)
        l_ref[...] = alpha * l_ref[...] + jnp.sum(p, axis=-1, keepdims=True)
        acc_ref[...] = alpha * acc_ref[...] + jnp.dot(p.astype(BF16), v_ref[0, pl.ds(off, tq), :],
                                                      preferred_element_type=F32)
        m_ref[...] = m_new
        return c

    lax.fori_loop(0, i, body, 0)
    o_ref[0] = _diff_finish(acc_ref[...], l_ref[...], _lambda(lamp_ref, lam_init), sn_ref[...], lam_init, tq)


def _diff_prompt(qb, kb, vb, lamp, sn, tq, lam_init):
    B, T, _ = qb.shape
    tile = pl.BlockSpec((1, tq, LANES), lambda b, h, i: (b, i, h))
    full = pl.BlockSpec((1, T, LANES), lambda b, h, i: (b, 0, h))
    return pl.pallas_call(
        functools.partial(_diff_prompt_kernel, tq=tq, lam_init=lam_init),
        grid=(B, N_DIFF_HEADS, T // tq),
        in_specs=[tile, full, full, _const_spec((4, HEAD)), _const_spec((1, LANES))],
        out_specs=tile,
        out_shape=jax.ShapeDtypeStruct((B, T, DIFF_WIDTH), F32),
        scratch_shapes=[pltpu.VMEM((2 * tq, 1), F32), pltpu.VMEM((2 * tq, 1), F32), pltpu.VMEM((2 * tq, LANES), F32)],
        compiler_params=_cparams(("parallel", "parallel", "arbitrary")),
        name="diff_prompt",
    )(qb, kb, vb, lamp, sn)


def _diff_sample_kernel(pt_ref, q_ref, kn_ref, vn_ref, lamp_ref, sn_ref, *rest, pp, s_new, lam_init):
    del pt_ref
    kpages, vpages = rest[:pp], rest[pp:2 * pp]
    o_ref, m_ref, l_ref, acc_ref = rest[2 * pp:]
    j = pl.program_id(1)
    q = q_ref[0]

    def update(h, scores, values):
        s = scores[0] if len(scores) == 1 else jnp.concatenate(scores, axis=1)
        m_old = m_ref[h]
        m_new = jnp.maximum(m_old, jnp.max(s, axis=-1, keepdims=True))
        alpha = jnp.exp(m_old - m_new)
        p = jnp.exp(s - m_new)
        acc = alpha * acc_ref[h]
        for c, vh in enumerate(values):
            acc = acc + jnp.dot(p[:, c * PAGE:(c + 1) * PAGE].astype(BF16), vh, preferred_element_type=F32)
        l_ref[h] = alpha * l_ref[h] + jnp.sum(p, axis=-1, keepdims=True)
        acc_ref[h] = acc
        m_ref[h] = m_new

    @pl.when(j == 0)
    def _():
        m_ref[...] = jnp.full(m_ref.shape, NEG_BIG, F32)
        l_ref[...] = jnp.zeros(l_ref.shape, F32)
        acc_ref[...] = jnp.zeros(acc_ref.shape, F32)
        pad = jnp.zeros((PAGE - s_new, LANES), F32)
        row = lax.broadcasted_iota(jnp.int32, (2 * s_new, PAGE), 0)
        col = lax.broadcasted_iota(jnp.int32, (2 * s_new, PAGE), 1)
        mask = col <= jnp.where(row >= s_new, row - s_new, row)
        for h in range(N_DIFF_HEADS):
            sl = slice(h * LANES, (h + 1) * LANES)
            kh = jnp.concatenate([kn_ref[0][:, sl], pad], axis=0).astype(BF16)
            vh = jnp.concatenate([vn_ref[0][:, sl], pad], axis=0).astype(BF16)
            s = jnp.where(mask, _nt_dot(_stack_halves(q[:, sl]), kh), NEG_BIG)
            update(h, [s], [vh])

    for h in range(N_DIFF_HEADS):
        sl = slice(h * LANES, (h + 1) * LANES)
        q2 = _stack_halves(q[:, sl])
        scores = [_nt_dot(q2, kp[:, sl].astype(BF16)) for kp in kpages]
        update(h, scores, [vp[:, sl].astype(BF16) for vp in vpages])

    @pl.when(j == pl.num_programs(1) - 1)
    def _():
        lam = _lambda(lamp_ref, lam_init)
        for h in range(N_DIFF_HEADS):
            o_ref[0, :, h * LANES:(h + 1) * LANES] = _diff_finish(acc_ref[h], l_ref[h], lam, sn_ref[...], lam_init, s_new)


def _diff_sample(page_table, qb, k_new, v_new, pool_k, pool_v, lamp, sn, pp, lam_init):
    Bs, S, _ = qb.shape
    n_pages = page_table.shape[1]
    new = pl.BlockSpec((1, S, DIFF_WIDTH), lambda b, j, pt: (b, 0, 0))
    page = lambda c: pl.BlockSpec((None, PAGE, DIFF_WIDTH), lambda b, j, pt, c=c: (pt[b, j * pp + c], 0, 0))
    cst = lambda shape: pl.BlockSpec(shape, lambda b, j, pt: (0,) * len(shape))
    return pl.pallas_call(
        functools.partial(_diff_sample_kernel, pp=pp, s_new=S, lam_init=lam_init),
        grid_spec=pltpu.PrefetchScalarGridSpec(
            num_scalar_prefetch=1,
            grid=(Bs, n_pages // pp),
            in_specs=[new, new, new, cst((4, HEAD)), cst((1, LANES))] + [page(c) for c in range(pp)] * 2,
            out_specs=new,
            scratch_shapes=[pltpu.VMEM((N_DIFF_HEADS, 2 * S, 1), F32), pltpu.VMEM((N_DIFF_HEADS, 2 * S, 1), F32),
                            pltpu.VMEM((N_DIFF_HEADS, 2 * S, LANES), F32)]),
        out_shape=jax.ShapeDtypeStruct((Bs, S, DIFF_WIDTH), F32),
        compiler_params=_cparams(("parallel", "arbitrary")),
        name="diff_sample",
    )(page_table, qb, k_new, v_new, lamp, sn, *([pool_k] * pp), *([pool_v] * pp))


def _out_proj_kernel(x_ref, r_ref, o_ref, w_ref, y_ref):
    y = x_ref[...] + jnp.dot(r_ref[...].astype(BF16), w_ref[:RW_WIDTH, :], preferred_element_type=F32)
    y_ref[...] = y + jnp.dot(o_ref[...].astype(BF16), w_ref[RW_WIDTH:, :], preferred_element_type=F32)


def _out_proj(x, r_out, o, w_bf, tm):
    n, d = x.shape
    row = lambda i: (i, 0)
    return pl.pallas_call(
        _out_proj_kernel,
        grid=(n // tm,),
        in_specs=[pl.BlockSpec((tm, d), row), pl.BlockSpec((tm, RW_WIDTH), row), pl.BlockSpec((tm, DIFF_WIDTH), row),
                  _const_spec(w_bf.shape)],
        out_specs=pl.BlockSpec((tm, d), row),
        out_shape=jax.ShapeDtypeStruct((n, d), F32),
        compiler_params=_cparams(("parallel",)),
        name="out_proj",
    )(x, r_out, o, w_bf)


def _mem_kv_kernel(x_ref, g_ref, w_ref, kn_ref, k_ref, v_ref):
    h = _rms_rows(x_ref[...], g_ref[...]).astype(BF16)
    kv = jnp.dot(h, w_ref[...], preferred_element_type=F32)
    for hh in range(N_MEM_HEADS):
        sl = slice(hh * LANES, (hh + 1) * LANES)
        k_ref[:, sl] = _rms_rows(kv[:, sl], kn_ref[...])
    v_ref[...] = kv[:, MEM_WIDTH:]


def _mem_kv(mem, g, w_bf, kn, tm):
    n, d = mem.shape
    row = lambda i: (i, 0)
    return pl.pallas_call(
        _mem_kv_kernel,
        grid=(n // tm,),
        in_specs=[pl.BlockSpec((tm, d), row), _const_spec((1, d)), _const_spec(w_bf.shape), _const_spec((1, LANES))],
        out_specs=(pl.BlockSpec((tm, MEM_WIDTH), row),) * 2,
        out_shape=(jax.ShapeDtypeStruct((n, MEM_WIDTH), F32),) * 2,
        compiler_params=_cparams(("parallel",)),
        name="mem_kv",
    )(mem, g, w_bf, kn)


def _mem_attn_kernel(x_ref, mk_ref, mv_ref, g_ref, wq_ref, qn_ref, wo_ref, y_ref, o_scr, *, nb, tr):
    d = x_ref.shape[-1]
    x = x_ref[...].reshape(nb * tr, d)
    h = _rms_rows(x, g_ref[...]).astype(BF16)
    q = jnp.dot(h, wq_ref[...], preferred_element_type=F32)
    for hh in range(N_MEM_HEADS):
        sl = slice(hh * LANES, (hh + 1) * LANES)
        qh = (_rms_rows(q[:, sl], qn_ref[...]) * (LANES ** -0.5)).astype(BF16)
        for n in range(nb):
            s = _nt_dot(qh[n * tr:(n + 1) * tr], mk_ref[n, :, sl].astype(BF16))
            p = jnp.exp(s - jnp.max(s, axis=-1, keepdims=True))
            p = p / jnp.sum(p, axis=-1, keepdims=True)
            o_scr[n * tr:(n + 1) * tr, sl] = jnp.dot(p.astype(BF16), mv_ref[n, :, sl].astype(BF16),
                                                     preferred_element_type=F32)
    y = x + jnp.dot(o_scr[...].astype(BF16), wo_ref[...], preferred_element_type=F32)
    y_ref[...] = y.reshape(nb, tr, d)


def _mem_attn(x, mk, mv, g, wq_bf, qn, wo_bf, nb, tr):
    G, R, d = x.shape
    M = mk.shape[1]
    return pl.pallas_call(
        functools.partial(_mem_attn_kernel, nb=nb, tr=tr),
        grid=(G // nb, R // tr),
        in_specs=[pl.BlockSpec((nb, tr, d), lambda i, t: (i, t, 0)),
                  pl.BlockSpec((nb, M, MEM_WIDTH), lambda i, t: (i, 0, 0)),
                  pl.BlockSpec((nb, M, MEM_WIDTH), lambda i, t: (i, 0, 0)),
                  _const_spec((1, d)), _const_spec(wq_bf.shape), _const_spec((1, LANES)), _const_spec(wo_bf.shape)],
        out_specs=pl.BlockSpec((nb, tr, d), lambda i, t: (i, t, 0)),
        out_shape=jax.ShapeDtypeStruct((G, R, d), F32),
        scratch_shapes=[pltpu.VMEM((nb * tr, MEM_WIDTH), F32)],
        compiler_params=_cparams(("parallel", "arbitrary")),
        name="mem_attn",
    )(x, mk, mv, g, wq_bf, qn, wo_bf)


def _ffn_kernel(x_ref, g_ref, wgu_ref, wd_ref, y_ref, *, d_ff, fc):
    x = x_ref[...]
    h = _rms_rows(x, g_ref[...]).astype(BF16)
    y = x
    for c in range(d_ff // fc):
        gt = jnp.dot(h, wgu_ref[:, c * fc:(c + 1) * fc], preferred_element_type=F32)
        up = jnp.dot(h, wgu_ref[:, d_ff + c * fc:d_ff + (c + 1) * fc], preferred_element_type=F32)
        act = (gt * _sigmoid(gt) * up).astype(BF16)
        y = y + jnp.dot(act, wd_ref[c * fc:(c + 1) * fc, :], preferred_element_type=F32)
    y_ref[...] = y


def _ffn(x, g, wgu_bf, wd_bf, tm):
    n, d = x.shape
    d_ff = wd_bf.shape[0]
    fc = d_ff // 2 if (d_ff // 2) % LANES == 0 else d_ff
    row = lambda i: (i, 0)
    return pl.pallas_call(
        functools.partial(_ffn_kernel, d_ff=d_ff, fc=fc),
        grid=(n // tm,),
        in_specs=[pl.BlockSpec((tm, d), row), _const_spec((1, d)), _const_spec(wgu_bf.shape), _const_spec(wd_bf.shape)],
        out_specs=pl.BlockSpec((tm, d), row),
        out_shape=jax.ShapeDtypeStruct((n, d), F32),
        compiler_params=_cparams(("parallel",)),
        name="ffn",
    )(x, g, wgu_bf, wd_bf)


def _pairs_from_state(s):
    S = s.shape[0]
    return s.reshape(S, 4, 2, HEAD, HEAD).transpose(0, 1, 3, 2, 4).reshape(S * 4 * HEAD, LANES)


def _state_from_pairs(t, S):
    return t.reshape(S, 4, HEAD, 2, HEAD).transpose(0, 1, 3, 2, 4).reshape(S, 8, HEAD, HEAD)


def _tile(n, pref):
    t = min(n, pref)
    while n % t:
        t //= 2
    return t


def kernel(x_prompt, x_sample, cache_diff_k, cache_diff_v, cache_mem_k, cache_mem_v, state_rwkv, state_shift, page_table, mem_prompt, norm_mix, w_in, rwkv_mu, rwkv_w0, rwkv_w2, rwkv_a0, rwkv_a2, rwkv_g2, rwkv_k_k, rwkv_k_a, rwkv_r_k, rwkv_lnx_w, rwkv_lnx_b, diff_q_norm, diff_k_norm, diff_lam_q1, diff_lam_k1, diff_lam_q2, diff_lam_k2, diff_sub_norm, w_o, norm_mem, norm_memkv, w_mq, w_mkv, mem_q_norm, mem_k_norm, w_mo, norm_ffn, w_gate_up, w_down):
    B, T, D = x_prompt.shape
    Bs, S, _ = x_sample.shape
    depth = w_in.shape[0]
    n_pages = page_table.shape[1]
    past = n_pages * PAGE
    M = mem_prompt.shape[1]
    xp = x_prompt.reshape(B * T, D)
    xs = x_sample.reshape(Bs * S, D)
    mem = mem_prompt.reshape(B * M, D)

    tm_p = _tile(B * T, 256)
    tm_p = _tile(T, tm_p)
    tm_s = _tile(Bs * S, 256)
    cos_p, sin_p = _rope_tables(jnp.arange(T, dtype=jnp.int32))
    cos_s, sin_s = _rope_tables(past + jnp.arange(S, dtype=jnp.int32))
    cos_s, sin_s = jnp.tile(cos_s, (tm_s // S, 1)), jnp.tile(sin_s, (tm_s // S, 1))
    tq = _tile(T, 256)
    pp = _tile(n_pages, 8)
    ns_s = _tile(Bs, 4)
    tt_p = _tile(T, 128)

    outs = [[] for _ in range(10)]
    for l in range(depth):
        lam_init = 0.8 - 0.6 * math.exp(-0.3 * l)
        row = lambda a: a[l].reshape(1, -1)
        bf = lambda a: a[l].astype(BF16)
        zpad = jnp.zeros((HEAD, RW_WIDTH), F32)
        rw = (row(rwkv_mu), row(rwkv_w0), jnp.concatenate([rwkv_w2[l], zpad], axis=0).astype(BF16), row(rwkv_a0),
              jnp.concatenate([zpad, rwkv_a2[l]], axis=0).astype(BF16), bf(rwkv_g2), row(rwkv_k_k), row(rwkv_k_a),
              row(rwkv_r_k), row(rwkv_lnx_w), row(rwkv_lnx_b))
        lamp = jnp.stack([diff_lam_q1[l], diff_lam_k1[l], diff_lam_q2[l], diff_lam_k2[l]])
        w_in_bf, w_o_bf, w_mq_bf, w_mkv_bf, w_mo_bf = bf(w_in), bf(w_o), bf(w_mq), bf(w_mkv), bf(w_mo)
        w_gu_bf, w_d_bf = bf(w_gate_up), bf(w_down)
        qn, kn, sn = row(diff_q_norm), row(diff_k_norm), row(diff_sub_norm)

        prw, qb, k, v, kb, vb = _in_proj(xp, row(norm_mix), w_in_bf, qn, kn, cos_p, sin_p, tm_p)
        prw3 = prw.reshape(B, T, RW_COLS)
        r_out, st_p = _rwkv(prw3, jnp.zeros((B, 1, RW_COLS), F32), jnp.zeros((B * 4 * HEAD, LANES), F32), rw, B, tt_p)
        o = _diff_prompt(qb.reshape(B, T, -1), kb.reshape(B, T, -1), vb.reshape(B, T, -1), lamp, sn, tq, lam_init)
        x1 = _out_proj(xp, r_out.reshape(B * T, -1), o.reshape(B * T, -1), w_o_bf, tm_p)
        mk, mv = _mem_kv(mem, row(norm_memkv), w_mkv_bf, row(mem_k_norm), _tile(B * M, 256))
        x2 = _mem_attn(x1.reshape(B, T, D), mk.reshape(B, M, -1), mv.reshape(B, M, -1), row(norm_mem), w_mq_bf,
                       row(mem_q_norm), w_mo_bf, 1, tm_p)
        xp = _ffn(x2.reshape(B * T, D), row(norm_ffn), w_gu_bf, w_d_bf, tm_p)
        outs[0].append(k.reshape(B, T, N_DIFF_HEADS, LANES))
        outs[1].append(v.reshape(B, T, N_DIFF_HEADS, LANES))
        outs[2].append(mk.reshape(B, M, N_MEM_HEADS, LANES))
        outs[3].append(mv.reshape(B, M, N_MEM_HEADS, LANES))
        outs[4].append(_state_from_pairs(st_p, B))
        outs[5].append(prw3[:, -1])

        prw, qb, k, v, _, _ = _in_proj(xs, row(norm_mix), w_in_bf, qn, kn, cos_s, sin_s, tm_s)
        prw3 = prw.reshape(Bs, S, RW_COLS)
        r_out, st_s = _rwkv(prw3, state_shift[l][:, None, :], _pairs_from_state(state_rwkv[l]), rw, ns_s, S)
        o = _diff_sample(page_table, qb.reshape(Bs, S, -1), k.reshape(Bs, S, -1), v.reshape(Bs, S, -1),
                         cache_diff_k[l].reshape(-1, PAGE, DIFF_WIDTH), cache_diff_v[l].reshape(-1, PAGE, DIFF_WIDTH),
                         lamp, sn, pp, lam_init)
        x1 = _out_proj(xs, r_out.reshape(Bs * S, -1), o.reshape(Bs * S, -1), w_o_bf, tm_s)
        x2 = _mem_attn(x1.reshape(Bs, S, D), cache_mem_k[l].reshape(Bs, M, -1), cache_mem_v[l].reshape(Bs, M, -1),
                       row(norm_mem), w_mq_bf, row(mem_q_norm), w_mo_bf, ns_s, S)
        xs = _ffn(x2.reshape(Bs * S, D), row(norm_ffn), w_gu_bf, w_d_bf, tm_s)
        outs[6].append(k.reshape(Bs, S, N_DIFF_HEADS, LANES))
        outs[7].append(v.reshape(Bs, S, N_DIFF_HEADS, LANES))
        outs[8].append(_state_from_pairs(st_s, Bs))
        outs[9].append(prw3[:, -1])

    st = [jnp.stack(o) for o in outs]
    return (xp.reshape(B, T, D), xs.reshape(Bs, S, D), st[0], st[1], st[2], st[3], st[4], st[5],
            st[6], st[7], st[8], st[9])
```

```python
import functools
import math

import jax
import jax.numpy as jnp
from jax import lax
from jax.experimental import pallas as pl
from jax.experimental.pallas import tpu as pltpu

F32 = jnp.float32
BF16 = jnp.bfloat16

LANES = 128
HEAD = 64
RW_WIDTH = 512
RW_COLS = 3 * RW_WIDTH + 64 + 64 + 128
DIFF_WIDTH = 512
N_DIFF_HEADS = 4
MEM_WIDTH = 512
N_MEM_HEADS = 4
PAGE = 128
ROPE_THETA = 10000.0
RMS_EPS = 1e-6
LNX_EPS = 1e-5 * HEAD
NEG_BIG = -1e30
VMEM_LIMIT = 56 * 1024 * 1024


def _cparams(sem):
    return pltpu.CompilerParams(dimension_semantics=sem, vmem_limit_bytes=VMEM_LIMIT)


def _const_spec(shape):
    nd = len(shape)
    return pl.BlockSpec(shape, lambda *_: (0,) * nd)


def _nt_dot(a, b):
    return lax.dot_general(a, b, (((1,), (1,)), ((), ())), preferred_element_type=F32)


def _rms_rows(x, g):
    ms = jnp.mean(x * x, axis=-1, keepdims=True)
    return x * lax.rsqrt(ms + RMS_EPS) * g


def _lane_iota(shape):
    return lax.broadcasted_iota(jnp.int32, shape, len(shape) - 1)


def _rms_half(xh, gain):
    lo = _lane_iota(xh.shape) < HEAD
    x2 = xh * xh
    s0 = jnp.sum(jnp.where(lo, x2, 0.0), axis=-1, keepdims=True)
    s1 = jnp.sum(jnp.where(lo, 0.0, x2), axis=-1, keepdims=True)
    inv = jnp.where(lo, lax.rsqrt(s0 * (1.0 / HEAD) + RMS_EPS), lax.rsqrt(s1 * (1.0 / HEAD) + RMS_EPS))
    return xh * inv * gain


def _rope_block(xh, cos, sin_signed):
    first = (_lane_iota(xh.shape) & 32) == 0
    partner = jnp.where(first, pltpu.roll(xh, LANES - 32, 1), pltpu.roll(xh, 32, 1))
    return xh * cos + partner * sin_signed


def _rope_tables(pos):
    half = HEAD // 2
    inv = ROPE_THETA ** (-jnp.arange(half, dtype=F32) / half)
    ang = pos.astype(F32)[:, None] * inv[None, :]
    cos, sin = jnp.cos(ang), jnp.sin(ang)
    return jnp.tile(cos, (1, 4)), jnp.concatenate([-sin, sin, -sin, sin], axis=1)


def _split_hi_lo(x):
    hi = x.astype(BF16)
    lo = (x - hi.astype(F32)).astype(BF16)
    return jnp.concatenate([hi, lo], axis=1)


def _group_ones2():
    r = lax.broadcasted_iota(jnp.int32, (2 * LANES, LANES), 0)
    c = lax.broadcasted_iota(jnp.int32, (2 * LANES, LANES), 1)
    return jnp.where(((r & (LANES - 1)) >> 6) == (c >> 6), 1.0, 0.0).astype(BF16)


def _group_sum(x, g2):
    return jnp.dot(_split_hi_lo(x), g2, preferred_element_type=F32)


def _group_sum_wide(x, g2):
    return jnp.concatenate(
        [_group_sum(x[:, c * LANES:(c + 1) * LANES], g2) for c in range(x.shape[1] // LANES)], axis=1)


def _sigmoid(x):
    return 1.0 / (1.0 + jnp.exp(-x))


def _head_rows(n_rows, h, n_heads):
    return pl.ds(h, n_rows, stride=n_heads)


def _in_proj_kernel(x_ref, g_ref, w_ref, qn_ref, kn_ref, cos_ref, sin_ref,
                    prw_ref, qb_ref, k_ref, v_ref, *flash_refs):
    tm = x_ref.shape[0]
    h = _rms_rows(x_ref[...], g_ref[...]).astype(BF16)
    prw_ref[...] = jnp.dot(h, w_ref[:, :RW_COLS], preferred_element_type=F32)
    pd = jnp.dot(h, w_ref[:, RW_COLS:], preferred_element_type=F32)
    cos, sin = cos_ref[...], sin_ref[...]
    qn, kn = qn_ref[...], kn_ref[...]
    for hh in range(N_DIFF_HEADS):
        sl = slice(hh * LANES, (hh + 1) * LANES)
        q = _rope_block(_rms_half(pd[:, sl], qn), cos, sin) * (HEAD ** -0.5)
        k = _rope_block(_rms_half(pd[:, DIFF_WIDTH + hh * LANES:DIFF_WIDTH + (hh + 1) * LANES], kn), cos, sin)
        v = pd[:, 2 * DIFF_WIDTH + hh * LANES:2 * DIFF_WIDTH + (hh + 1) * LANES]
        qb_ref[:, sl] = q.astype(BF16)
        k_ref[_head_rows(tm, hh, N_DIFF_HEADS), :] = k
        v_ref[_head_rows(tm, hh, N_DIFF_HEADS), :] = v
        if flash_refs:
            kb_ref, vt_ref = flash_refs
            kb_ref[:, sl] = k.astype(BF16)
            vt_ref[0, sl, :] = jnp.transpose(v).astype(BF16)


def _in_proj(x, g, w_bf, qn, kn, cos, sin, tm, seq_len=None):
    n, d = x.shape
    period = cos.shape[0] // tm
    row = lambda i: (i, 0)
    outs = [jax.ShapeDtypeStruct((n, RW_COLS), F32), jax.ShapeDtypeStruct((n, DIFF_WIDTH), BF16),
            jax.ShapeDtypeStruct((n * N_DIFF_HEADS, LANES), F32), jax.ShapeDtypeStruct((n * N_DIFF_HEADS, LANES), F32)]
    out_specs = [pl.BlockSpec((tm, RW_COLS), row), pl.BlockSpec((tm, DIFF_WIDTH), row),
                 pl.BlockSpec((tm * N_DIFF_HEADS, LANES), row), pl.BlockSpec((tm * N_DIFF_HEADS, LANES), row)]
    if seq_len is not None:
        per_seq = seq_len // tm
        outs += [jax.ShapeDtypeStruct((n, DIFF_WIDTH), BF16), jax.ShapeDtypeStruct((n // seq_len, DIFF_WIDTH, seq_len), BF16)]
        out_specs += [pl.BlockSpec((tm, DIFF_WIDTH), row),
                      pl.BlockSpec((1, DIFF_WIDTH, tm), lambda i: (i // per_seq, 0, i % per_seq))]
    return pl.pallas_call(
        _in_proj_kernel,
        grid=(n // tm,),
        in_specs=[pl.BlockSpec((tm, d), row), _const_spec((1, d)), _const_spec(w_bf.shape),
                  _const_spec((1, LANES)), _const_spec((1, LANES)),
                  pl.BlockSpec((tm, LANES), lambda i: (i % period, 0)),
                  pl.BlockSpec((tm, LANES), lambda i: (i % period, 0))],
        out_specs=tuple(out_specs),
        out_shape=tuple(outs),
        compiler_params=_cparams(("parallel",)),
        name="in_proj",
    )(x, g, w_bf, qn, kn, cos, sin)


def _rwkv_kernel(p_ref, prev_ref, s0_ref, mu_ref, w0_ref, w2_ref, a0_ref, a2_ref, g2_ref,
                 kk_ref, ka_ref, rk_ref, lw_ref, lb_ref,
                 out_ref, st_ref,
                 state, carry, r_s, w_s, k_s, v_s, a_s, b_s, y_s, bonus_s, gate_s, *, ns, tt):
    ti = pl.program_id(1)
    npair = RW_WIDTH // LANES
    g2 = _group_ones2()

    @pl.when(ti == 0)
    def _():
        state[...] = s0_ref[...]
        carry[...] = prev_ref[...]

    row0 = lax.broadcasted_iota(jnp.int32, (tt, RW_COLS), 0) == 0
    for s in range(ns):
        p = p_ref[s]
        shifted = jnp.where(row0, carry[s], pltpu.roll(p, 1, 0))
        carry[s] = p[tt - 1:tt, :]
        ps = p + (shifted - p) * mu_ref[...]
        r = ps[:, :RW_WIDTH]
        k = ps[:, RW_WIDTH:2 * RW_WIDTH]
        v = ps[:, 2 * RW_WIDTH:3 * RW_WIDTH]
        wa = ps[:, 3 * RW_WIDTH:3 * RW_WIDTH + LANES]
        gd = ps[:, 3 * RW_WIDTH + LANES:]
        z = -(w0_ref[...] + jnp.dot(jnp.tanh(wa).astype(BF16), w2_ref[...], preferred_element_type=F32))
        w_log = -(jnp.maximum(z, 0.0) + jnp.log1p(jnp.exp(-jnp.abs(z)))) - 0.5
        decay = jnp.exp(-jnp.exp(w_log))
        a = _sigmoid(a0_ref[...] + jnp.dot(wa.astype(BF16), a2_ref[...], preferred_element_type=F32))
        gate = jnp.dot(_sigmoid(gd).astype(BF16), g2_ref[...], preferred_element_type=F32)
        kk = k * kk_ref[...]
        kk = kk / jnp.maximum(jnp.sqrt(_group_sum_wide(kk * kk, g2)), 1e-12)
        k2 = k * (1.0 + (a - 1.0) * ka_ref[...])
        r_s[s] = r
        w_s[s] = decay
        k_s[s] = k2
        v_s[s] = v
        a_s[s] = -kk
        b_s[s] = kk * a
        bonus_s[s] = _group_sum_wide(r * k2 * rk_ref[...], g2) * v
        gate_s[s] = gate

    tiles = [(s, j) for s in range(ns) for j in range(npair)]
    nt = len(tiles)
    sub = lax.broadcasted_iota(jnp.int32, (nt * HEAD, LANES), 0)
    e1 = jnp.where((_lane_iota((nt * HEAD, LANES)) & (HEAD - 1)) == (sub & (HEAD - 1)), 1.0, 0.0)
    seqs = dict(r=r_s, w=w_s, k=k_s, v=v_s, a=a_s, b=b_s)

    def block(tb, c):
        base = pl.multiple_of(tb * 8, 8)
        rows8 = {q: [ref[s, pl.ds(base, 8), j * LANES:(j + 1) * LANES] for (s, j) in tiles] for q, ref in seqs.items()}
        ys = []
        for i in range(8):
            bc = lambda q: jnp.concatenate(
                [jnp.broadcast_to(rows8[q][n][i:i + 1], (HEAD, LANES)) for n in range(nt)], axis=0)
            st = state[...]
            red = jnp.dot(jnp.concatenate([_split_hi_lo(st * bc("a")), _split_hi_lo(e1 * bc("v"))], axis=0), g2,
                          preferred_element_type=F32)
            st = st * bc("w") + red[:nt * HEAD] * bc("b") + red[nt * HEAD:] * bc("k")
            state[...] = st
            yred = jnp.dot(_split_hi_lo(st * bc("r")), g2, preferred_element_type=F32)
            ys.append(jnp.sum((e1 * yred).reshape(nt, HEAD, LANES), axis=1))
        for n, (s, j) in enumerate(tiles):
            y_s[s, pl.ds(base, 8), j * LANES:(j + 1) * LANES] = jnp.concatenate([y[n:n + 1] for y in ys], axis=0)
        return c

    lax.fori_loop(0, tt // 8, block, 0)

    for s in range(ns):
        y = y_s[s]
        mean = _group_sum_wide(y, g2) * (1.0 / HEAD)
        d = y - mean
        var = _group_sum_wide(d * d, g2) * (1.0 / HEAD)
        yn = d * lax.rsqrt(var + LNX_EPS) * lw_ref[...] + lb_ref[...]
        out_ref[s] = (yn + bonus_s[s]) * gate_s[s]

    @pl.when(ti == pl.num_programs(1) - 1)
    def _():
        st_ref[...] = state[...]


def _rwkv(prw, prev, s0_pairs, rw, ns, tt):
    S, T, _ = prw.shape
    npair = RW_WIDTH // LANES
    vec = lambda n: _const_spec((1, n))
    blk = lambda w: pl.BlockSpec((ns, tt, w), lambda i, t: (i, t, 0))
    st_spec = pl.BlockSpec((ns * npair * HEAD, LANES), lambda i, t: (i, 0))
    seq = lambda: pltpu.VMEM((ns, tt, RW_WIDTH), F32)
    return pl.pallas_call(
        functools.partial(_rwkv_kernel, ns=ns, tt=tt),
        grid=(S // ns, T // tt),
        in_specs=[blk(RW_COLS), pl.BlockSpec((ns, 1, RW_COLS), lambda i, t: (i, 0, 0)), st_spec,
                  vec(RW_COLS), vec(RW_WIDTH), _const_spec((LANES, RW_WIDTH)), vec(RW_WIDTH),
                  _const_spec((LANES, RW_WIDTH)), _const_spec((LANES, RW_WIDTH)),
                  vec(RW_WIDTH), vec(RW_WIDTH), vec(RW_WIDTH), vec(RW_WIDTH), vec(RW_WIDTH)],
        out_specs=(blk(RW_WIDTH), st_spec),
        out_shape=(jax.ShapeDtypeStruct((S, T, RW_WIDTH), F32),
                   jax.ShapeDtypeStruct((S * npair * HEAD, LANES), F32)),
        scratch_shapes=[pltpu.VMEM((ns * npair * HEAD, LANES), F32), pltpu.VMEM((ns, 1, RW_COLS), F32)]
                       + [seq() for _ in range(9)],
        compiler_params=_cparams(("parallel", "arbitrary")),
        name="rwkv",
    )(prw, prev, s0_pairs, *rw)


def _lambda(lamp_ref, lam_init):
    lp = lamp_ref[...]
    s1 = jnp.sum(lp[0:1] * lp[1:2], axis=-1, keepdims=True)
    s2 = jnp.sum(lp[2:3] * lp[3:4], axis=-1, keepdims=True)
    return jnp.exp(s1) - jnp.exp(s2) + lam_init


def _stack_halves(q):
    lo = _lane_iota(q.shape) < HEAD
    zero = jnp.zeros_like(q)
    return jnp.concatenate([jnp.where(lo, q, zero), jnp.where(lo, zero, q)], axis=0)


def _diff_finish(acc, l, lam, sn, lam_init, rows):
    o = acc[:rows] / l[:rows] - lam * (acc[rows:] / l[rows:])
    return _rms_rows(o, sn) * (1.0 - lam_init)


def _diff_prompt_kernel(q_ref, k_ref, vt_ref, lamp_ref, sn_ref, o_ref, m_ref, l_ref, acc_ref, *, tq, lam_init):
    i = pl.program_id(2)
    q2 = _stack_halves(q_ref[0])
    start = pl.multiple_of(i * tq, tq)
    s = _nt_dot(k_ref[0, pl.ds(start, tq), :], q2)
    key = lax.broadcasted_iota(jnp.int32, s.shape, 0)
    qry = lax.broadcasted_iota(jnp.int32, s.shape, 1)
    s = jnp.where(key <= jnp.where(qry >= tq, qry - tq, qry), s, NEG_BIG)
    m = jnp.max(s, axis=0, keepdims=True)
    p = jnp.exp(s - m)
    m_ref[...] = m
    l_ref[...] = jnp.sum(p, axis=0, keepdims=True)
    acc_ref[...] = jnp.dot(vt_ref[0, :, pl.ds(start, tq)], p.astype(BF16), preferred_element_type=F32)

    def body(j, c):
        off = pl.multiple_of(j * tq, tq)
        s = _nt_dot(k_ref[0, pl.ds(off, tq), :], q2)
        m_old = m_ref[...]
        m_new = jnp.maximum(m_old, jnp.max(s, axis=0, keepdims=True))
        alpha = jnp.exp(m_old - m_new)
        p = jnp.exp(s - m_new)
        l_ref[...] = alpha * l_ref[...] + jnp.sum(p, axis=0, keepdims=True)
        acc_ref[...] = alpha * acc_ref[...] + jnp.dot(vt_ref[0, :, pl.ds(off, tq)], p.astype(BF16),
                                                      preferred_element_type=F32)
        m_ref[...] = m_new
        return c

    lax.fori_loop(0, i, body, 0)
    acc = acc_ref[...] / l_ref[...]
    o_t = acc[:, :tq] - _lambda(lamp_ref, lam_init) * acc[:, tq:]
    ms = jnp.mean(o_t * o_t, axis=0, keepdims=True)
    o_t = o_t * lax.rsqrt(ms + RMS_EPS)
    o_ref[0] = jnp.transpose(o_t) * sn_ref[...] * (1.0 - lam_init)


def _diff_prompt(qb, kb, vt, lamp, sn, tq, lam_init):
    B, T, _ = qb.shape
    tile = pl.BlockSpec((1, tq, LANES), lambda b, h, i: (b, i, h))
    return pl.pallas_call(
        functools.partial(_diff_prompt_kernel, tq=tq, lam_init=lam_init),
        grid=(B, N_DIFF_HEADS, T // tq),
        in_specs=[tile, pl.BlockSpec((1, T, LANES), lambda b, h, i: (b, 0, h)),
                  pl.BlockSpec((1, LANES, T), lambda b, h, i: (b, h, 0)),
                  _const_spec((4, HEAD)), _const_spec((1, LANES))],
        out_specs=tile,
        out_shape=jax.ShapeDtypeStruct((B, T, DIFF_WIDTH), F32),
        scratch_shapes=[pltpu.VMEM((1, 2 * tq), F32), pltpu.VMEM((1, 2 * tq), F32), pltpu.VMEM((LANES, 2 * tq), F32)],
        compiler_params=_cparams(("parallel", "parallel", "arbitrary")),
        name="diff_prompt",
    )(qb, kb, vt, lamp, sn)


def _diff_sample_kernel(pt_ref, q_ref, kn_ref, vn_ref, lamp_ref, sn_ref, *rest, pp, s_new, lam_init):
    del pt_ref
    kpages, vpages = rest[:pp], rest[pp:2 * pp]
    o_ref, m_ref, l_ref, acc_ref = rest[2 * pp:]
    j = pl.program_id(1)
    nh = N_DIFF_HEADS
    rows = 2 * s_new
    q = q_ref[0]
    q2 = [_stack_halves(q[:, h * LANES:(h + 1) * LANES]) for h in range(nh)]

    def update(scores, values):
        s = jnp.concatenate([jnp.concatenate(sc, axis=1) if len(sc) > 1 else sc[0] for sc in scores], axis=0)
        m_old = m_ref[...]
        m_new = jnp.maximum(m_old, jnp.max(s, axis=-1, keepdims=True))
        alpha = jnp.exp(m_old - m_new)
        p = jnp.exp(s - m_new)
        pv = []
        for h in range(nh):
            ph = p[h * rows:(h + 1) * rows].astype(BF16)
            acc = jnp.dot(ph[:, :PAGE], values[h][0], preferred_element_type=F32)
            for c in range(1, len(values[h])):
                acc = acc + jnp.dot(ph[:, c * PAGE:(c + 1) * PAGE], values[h][c], preferred_element_type=F32)
            pv.append(acc)
        l_ref[...] = alpha * l_ref[...] + jnp.sum(p, axis=-1, keepdims=True)
        acc_ref[...] = alpha * acc_ref[...] + jnp.concatenate(pv, axis=0)
        m_ref[...] = m_new

    @pl.when(j == 0)
    def _():
        m_ref[...] = jnp.full(m_ref.shape, NEG_BIG, F32)
        l_ref[...] = jnp.zeros(l_ref.shape, F32)
        acc_ref[...] = jnp.zeros(acc_ref.shape, F32)
        pad = jnp.zeros((PAGE - s_new, LANES), F32)
        row = lax.broadcasted_iota(jnp.int32, (rows, PAGE), 0)
        col = lax.broadcasted_iota(jnp.int32, (rows, PAGE), 1)
        mask = col <= jnp.where(row >= s_new, row - s_new, row)
        new_rows = lambda ref, h: jnp.concatenate([ref[0, _head_rows(s_new, h, nh), :], pad], axis=0).astype(BF16)
        update([[jnp.where(mask, _nt_dot(q2[h], new_rows(kn_ref, h)), NEG_BIG)] for h in range(nh)],
               [[new_rows(vn_ref, h)] for h in range(nh)])

    page_rows = lambda ref, h: ref[_head_rows(PAGE, h, nh), :].astype(BF16)
    update([[_nt_dot(q2[h], page_rows(kp, h)) for kp in kpages] for h in range(nh)],
           [[page_rows(vp, h) for vp in vpages] for h in range(nh)])

    @pl.when(j == pl.num_programs(1) - 1)
    def _():
        lam = _lambda(lamp_ref, lam_init)
        acc, l = acc_ref[...], l_ref[...]
        for h in range(nh):
            o_ref[0, :, h * LANES:(h + 1) * LANES] = _diff_finish(
                acc[h * rows:(h + 1) * rows], l[h * rows:(h + 1) * rows], lam, sn_ref[...], lam_init, s_new)


def _diff_sample(page_table, qb, k_new, v_new, pool_k, pool_v, lamp, sn, pp, lam_init):
    Bs, S, _ = qb.shape
    n_pages = page_table.shape[1]
    nh = N_DIFF_HEADS
    qspec = pl.BlockSpec((1, S, DIFF_WIDTH), lambda b, j, pt: (b, 0, 0))
    new = pl.BlockSpec((1, S * nh, LANES), lambda b, j, pt: (b, 0, 0))
    page = lambda c: pl.BlockSpec((None, PAGE * nh, LANES), lambda b, j, pt, c=c: (pt[b, j * pp + c], 0, 0))
    cst = lambda shape: pl.BlockSpec(shape, lambda b, j, pt: (0,) * len(shape))
    return pl.pallas_call(
        functools.partial(_diff_sample_kernel, pp=pp, s_new=S, lam_init=lam_init),
        grid_spec=pltpu.PrefetchScalarGridSpec(
            num_scalar_prefetch=1,
            grid=(Bs, n_pages // pp),
            in_specs=[qspec, new, new, cst((4, HEAD)), cst((1, LANES))] + [page(c) for c in range(pp)] * 2,
            out_specs=qspec,
            scratch_shapes=[pltpu.VMEM((nh * 2 * S, 1), F32), pltpu.VMEM((nh * 2 * S, 1), F32),
                            pltpu.VMEM((nh * 2 * S, LANES), F32)]),
        out_shape=jax.ShapeDtypeStruct((Bs, S, DIFF_WIDTH), F32),
        compiler_params=_cparams(("parallel", "arbitrary")),
        name="diff_sample",
    )(page_table, qb, k_new, v_new, lamp, sn, *([pool_k] * pp), *([pool_v] * pp))


def _out_proj_kernel(x_ref, r_ref, o_ref, w_ref, y_ref):
    y = x_ref[...] + jnp.dot(r_ref[...].astype(BF16), w_ref[:RW_WIDTH, :], preferred_element_type=F32)
    y_ref[...] = y + jnp.dot(o_ref[...].astype(BF16), w_ref[RW_WIDTH:, :], preferred_element_type=F32)


def _out_proj(x, r_out, o, w_bf, tm):
    n, d = x.shape
    row = lambda i: (i, 0)
    return pl.pallas_call(
        _out_proj_kernel,
        grid=(n // tm,),
        in_specs=[pl.BlockSpec((tm, d), row), pl.BlockSpec((tm, RW_WIDTH), row), pl.BlockSpec((tm, DIFF_WIDTH), row),
                  _const_spec(w_bf.shape)],
        out_specs=pl.BlockSpec((tm, d), row),
        out_shape=jax.ShapeDtypeStruct((n, d), F32),
        compiler_params=_cparams(("parallel",)),
        name="out_proj",
    )(x, r_out, o, w_bf)


def _mem_kv_kernel(x_ref, g_ref, w_ref, kn_ref, k_ref, v_ref):
    h = _rms_rows(x_ref[...], g_ref[...]).astype(BF16)
    kv = jnp.dot(h, w_ref[...], preferred_element_type=F32)
    tm = x_ref.shape[0]
    for hh in range(N_MEM_HEADS):
        sl = slice(hh * LANES, (hh + 1) * LANES)
        k_ref[_head_rows(tm, hh, N_MEM_HEADS), :] = _rms_rows(kv[:, sl], kn_ref[...])
        v_ref[_head_rows(tm, hh, N_MEM_HEADS), :] = kv[:, MEM_WIDTH + hh * LANES:MEM_WIDTH + (hh + 1) * LANES]


def _mem_kv(mem, g, w_bf, kn, tm):
    n, d = mem.shape
    row = lambda i: (i, 0)
    return pl.pallas_call(
        _mem_kv_kernel,
        grid=(n // tm,),
        in_specs=[pl.BlockSpec((tm, d), row), _const_spec((1, d)), _const_spec(w_bf.shape), _const_spec((1, LANES))],
        out_specs=(pl.BlockSpec((tm * N_MEM_HEADS, LANES), row),) * 2,
        out_shape=(jax.ShapeDtypeStruct((n * N_MEM_HEADS, LANES), F32),) * 2,
        compiler_params=_cparams(("parallel",)),
        name="mem_kv",
    )(mem, g, w_bf, kn)


def _mem_attn_kernel(x_ref, mk_ref, mv_ref, g_ref, wq_ref, qn_ref, wo_ref, y_ref, o_scr, *, nb, tr, n_mem):
    d = x_ref.shape[-1]
    x = x_ref[...].reshape(nb * tr, d)
    h = _rms_rows(x, g_ref[...]).astype(BF16)
    q = jnp.dot(h, wq_ref[...], preferred_element_type=F32)
    for hh in range(N_MEM_HEADS):
        sl = slice(hh * LANES, (hh + 1) * LANES)
        qh = (_rms_rows(q[:, sl], qn_ref[...]) * (LANES ** -0.5)).astype(BF16)
        for n in range(nb):
            mem_rows = _head_rows(n_mem, hh, N_MEM_HEADS)
            s = _nt_dot(qh[n * tr:(n + 1) * tr], mk_ref[n, mem_rows, :].astype(BF16))
            p = jnp.exp(s - jnp.max(s, axis=-1, keepdims=True))
            p = p / jnp.sum(p, axis=-1, keepdims=True)
            o_scr[n * tr:(n + 1) * tr, sl] = jnp.dot(p.astype(BF16), mv_ref[n, mem_rows, :].astype(BF16),
                                                     preferred_element_type=F32)
    y = x + jnp.dot(o_scr[...].astype(BF16), wo_ref[...], preferred_element_type=F32)
    y_ref[...] = y.reshape(nb, tr, d)


def _mem_attn(x, mk, mv, g, wq_bf, qn, wo_bf, nb, tr):
    G, R, d = x.shape
    M4 = mk.shape[1]
    return pl.pallas_call(
        functools.partial(_mem_attn_kernel, nb=nb, tr=tr, n_mem=M4 // N_MEM_HEADS),
        grid=(G // nb, R // tr),
        in_specs=[pl.BlockSpec((nb, tr, d), lambda i, t: (i, t, 0)),
                  pl.BlockSpec((nb, M4, LANES), lambda i, t: (i, 0, 0)),
                  pl.BlockSpec((nb, M4, LANES), lambda i, t: (i, 0, 0)),
                  _const_spec((1, d)), _const_spec(wq_bf.shape), _const_spec((1, LANES)), _const_spec(wo_bf.shape)],
        out_specs=pl.BlockSpec((nb, tr, d), lambda i, t: (i, t, 0)),
        out_shape=jax.ShapeDtypeStruct((G, R, d), F32),
        scratch_shapes=[pltpu.VMEM((nb * tr, MEM_WIDTH), F32)],
        compiler_params=_cparams(("parallel", "arbitrary")),
        name="mem_attn",
    )(x, mk, mv, g, wq_bf, qn, wo_bf)


def _ffn_kernel(x_ref, g_ref, wgu_ref, wd_ref, y_ref, *, d_ff, fc):
    x = x_ref[...]
    h = _rms_rows(x, g_ref[...]).astype(BF16)
    y = x
    for c in range(d_ff // fc):
        gt = jnp.dot(h, wgu_ref[:, c * fc:(c + 1) * fc], preferred_element_type=F32)
        up = jnp.dot(h, wgu_ref[:, d_ff + c * fc:d_ff + (c + 1) * fc], preferred_element_type=F32)
        act = (gt * _sigmoid(gt) * up).astype(BF16)
        y = y + jnp.dot(act, wd_ref[c * fc:(c + 1) * fc, :], preferred_element_type=F32)
    y_ref[...] = y


def _ffn(x, g, wgu_bf, wd_bf, tm):
    n, d = x.shape
    d_ff = wd_bf.shape[0]
    fc = d_ff // 2 if (d_ff // 2) % LANES == 0 else d_ff
    row = lambda i: (i, 0)
    return pl.pallas_call(
        functools.partial(_ffn_kernel, d_ff=d_ff, fc=fc),
        grid=(n // tm,),
        in_specs=[pl.BlockSpec((tm, d), row), _const_spec((1, d)), _const_spec(wgu_bf.shape), _const_spec(wd_bf.shape)],
        out_specs=pl.BlockSpec((tm, d), row),
        out_shape=jax.ShapeDtypeStruct((n, d), F32),
        compiler_params=_cparams(("parallel",)),
        name="ffn",
    )(x, g, wgu_bf, wd_bf)


def _pairs_from_state(s):
    S = s.shape[0]
    return s.reshape(S, 4, 2, HEAD, HEAD).transpose(0, 1, 3, 2, 4).reshape(S * 4 * HEAD, LANES)


def _state_from_pairs(t, S):
    return t.reshape(S, 4, HEAD, 2, HEAD).transpose(0, 1, 3, 2, 4).reshape(S, 8, HEAD, HEAD)


def _tile(n, pref):
    t = min(n, pref)
    while n % t:
        t //= 2
    return t


def kernel(x_prompt, x_sample, cache_diff_k, cache_diff_v, cache_mem_k, cache_mem_v, state_rwkv, state_shift, page_table, mem_prompt, norm_mix, w_in, rwkv_mu, rwkv_w0, rwkv_w2, rwkv_a0, rwkv_a2, rwkv_g2, rwkv_k_k, rwkv_k_a, rwkv_r_k, rwkv_lnx_w, rwkv_lnx_b, diff_q_norm, diff_k_norm, diff_lam_q1, diff_lam_k1, diff_lam_q2, diff_lam_k2, diff_sub_norm, w_o, norm_mem, norm_memkv, w_mq, w_mkv, mem_q_norm, mem_k_norm, w_mo, norm_ffn, w_gate_up, w_down):
    B, T, D = x_prompt.shape
    Bs, S, _ = x_sample.shape
    depth = w_in.shape[0]
    n_pages = page_table.shape[1]
    past = n_pages * PAGE
    M = mem_prompt.shape[1]
    xp = x_prompt.reshape(B * T, D)
    xs = x_sample.reshape(Bs * S, D)
    mem = mem_prompt.reshape(B * M, D)

    tm_p = _tile(B * T, 256)
    tm_p = _tile(T, tm_p)
    tm_s = _tile(Bs * S, 256)
    cos_p, sin_p = _rope_tables(jnp.arange(T, dtype=jnp.int32))
    cos_s, sin_s = _rope_tables(past + jnp.arange(S, dtype=jnp.int32))
    cos_s, sin_s = jnp.tile(cos_s, (tm_s // S, 1)), jnp.tile(sin_s, (tm_s // S, 1))
    tq = _tile(T, 512)
    pp = _tile(n_pages, 16)
    ns_s = _tile(Bs, 4)
    nb_mem = _tile(Bs, 16)
    tt_p = _tile(T, 128)

    outs = [[] for _ in range(10)]
    for l in range(depth):
        lam_init = 0.8 - 0.6 * math.exp(-0.3 * l)
        row = lambda a: a[l].reshape(1, -1)
        bf = lambda a: a[l].astype(BF16)
        zpad = jnp.zeros((HEAD, RW_WIDTH), F32)
        rw = (row(rwkv_mu), row(rwkv_w0), jnp.concatenate([rwkv_w2[l], zpad], axis=0).astype(BF16), row(rwkv_a0),
              jnp.concatenate([zpad, rwkv_a2[l]], axis=0).astype(BF16), bf(rwkv_g2), row(rwkv_k_k), row(rwkv_k_a),
              row(rwkv_r_k), row(rwkv_lnx_w), row(rwkv_lnx_b))
        lamp = jnp.stack([diff_lam_q1[l], diff_lam_k1[l], diff_lam_q2[l], diff_lam_k2[l]])
        w_in_bf, w_o_bf, w_mq_bf, w_mkv_bf, w_mo_bf = bf(w_in), bf(w_o), bf(w_mq), bf(w_mkv), bf(w_mo)
        w_gu_bf, w_d_bf = bf(w_gate_up), bf(w_down)
        qn, kn, sn = row(diff_q_norm), row(diff_k_norm), row(diff_sub_norm)

        prw, qb, k, v, kb, vt = _in_proj(xp, row(norm_mix), w_in_bf, qn, kn, cos_p, sin_p, tm_p, seq_len=T)
        prw3 = prw.reshape(B, T, RW_COLS)
        r_out, st_p = _rwkv(prw3, jnp.zeros((B, 1, RW_COLS), F32), jnp.zeros((B * 4 * HEAD, LANES), F32), rw, B, tt_p)
        o = _diff_prompt(qb.reshape(B, T, -1), kb.reshape(B, T, -1), vt, lamp, sn, tq, lam_init)
        x1 = _out_proj(xp, r_out.reshape(B * T, -1), o.reshape(B * T, -1), w_o_bf, tm_p)
        mk, mv = _mem_kv(mem, row(norm_memkv), w_mkv_bf, row(mem_k_norm), _tile(M, 256))
        x2 = _mem_attn(x1.reshape(B, T, D), mk.reshape(B, M * N_MEM_HEADS, LANES), mv.reshape(B, M * N_MEM_HEADS, LANES),
                       row(norm_mem), w_mq_bf, row(mem_q_norm), w_mo_bf, 1, tm_p)
        xp = _ffn(x2.reshape(B * T, D), row(norm_ffn), w_gu_bf, w_d_bf, tm_p)
        outs[0].append(k.reshape(B, T, N_DIFF_HEADS, LANES))
        outs[1].append(v.reshape(B, T, N_DIFF_HEADS, LANES))
        outs[2].append(mk.reshape(B, M, N_MEM_HEADS, LANES))
        outs[3].append(mv.reshape(B, M, N_MEM_HEADS, LANES))
        outs[4].append(_state_from_pairs(st_p, B))
        outs[5].append(prw3[:, -1])

        prw, qb, k, v = _in_proj(xs, row(norm_mix), w_in_bf, qn, kn, cos_s, sin_s, tm_s)
        prw3 = prw.reshape(Bs, S, RW_COLS)
        r_out, st_s = _rwkv(prw3, state_shift[l][:, None, :], _pairs_from_state(state_rwkv[l]), rw, ns_s, S)
        o = _diff_sample(page_table, qb.reshape(Bs, S, -1), k.reshape(Bs, S * N_DIFF_HEADS, LANES),
                         v.reshape(Bs, S * N_DIFF_HEADS, LANES),
                         cache_diff_k[l].reshape(-1, PAGE * N_DIFF_HEADS, LANES),
                         cache_diff_v[l].reshape(-1, PAGE * N_DIFF_HEADS, LANES), lamp, sn, pp, lam_init)
        x1 = _out_proj(xs, r_out.reshape(Bs * S, -1), o.reshape(Bs * S, -1), w_o_bf, tm_s)
        x2 = _mem_attn(x1.reshape(Bs, S, D), cache_mem_k[l].reshape(Bs, M * N_MEM_HEADS, LANES),
                       cache_mem_v[l].reshape(Bs, M * N_MEM_HEADS, LANES),
                       row(norm_mem), w_mq_bf, row(mem_q_norm), w_mo_bf, nb_mem, S)
        xs = _ffn(x2.reshape(Bs * S, D), row(norm_ffn), w_gu_bf, w_d_bf, tm_s)
        outs[6].append(k.reshape(Bs, S, N_DIFF_HEADS, LANES))
        outs[7].append(v.reshape(Bs, S, N_DIFF_HEADS, LANES))
        outs[8].append(_state_from_pairs(st_s, Bs))
        outs[9].append(prw3[:, -1])

    st = [jnp.stack(o) for o in outs]
    return (xp.reshape(B, T, D), xs.reshape(Bs, S, D), st[0], st[1], st[2], st[3], st[4], st[5],
            st[6], st[7], st[8], st[9])
```

```python
import functools
import math

import jax
import jax.numpy as jnp
from jax import lax
from jax.experimental import pallas as pl
from jax.experimental.pallas import tpu as pltpu

F32 = jnp.float32
BF16 = jnp.bfloat16

LANES = 128
HEAD = 64
RW_WIDTH = 512
RW_COLS = 3 * RW_WIDTH + 64 + 64 + 128
DIFF_WIDTH = 512
N_DIFF_HEADS = 4
MEM_WIDTH = 512
N_MEM_HEADS = 4
PAGE = 128
ROPE_THETA = 10000.0
RMS_EPS = 1e-6
LNX_EPS = 1e-5 * HEAD
NEG_BIG = -1e30
VMEM_LIMIT = 56 * 1024 * 1024
RWKV_CHUNK = 64
RWKV_TERMS = 1
ONES_ROWS = 16
LOG2E = 1.4426950408889634


def _cparams(sem):
    return pltpu.CompilerParams(dimension_semantics=sem, vmem_limit_bytes=VMEM_LIMIT)


def _const_spec(shape):
    nd = len(shape)
    return pl.BlockSpec(shape, lambda *_: (0,) * nd)


def _nt_dot(a, b):
    return lax.dot_general(a, b, (((1,), (1,)), ((), ())), preferred_element_type=F32)


def _rms_rows(x, g):
    ms = jnp.mean(x * x, axis=-1, keepdims=True)
    return x * lax.rsqrt(ms + RMS_EPS) * g


def _lane_iota(shape):
    return lax.broadcasted_iota(jnp.int32, shape, len(shape) - 1)


def _rms_half(xh, gain):
    lo = _lane_iota(xh.shape) < HEAD
    x2 = xh * xh
    s0 = jnp.sum(jnp.where(lo, x2, 0.0), axis=-1, keepdims=True)
    s1 = jnp.sum(jnp.where(lo, 0.0, x2), axis=-1, keepdims=True)
    inv = jnp.where(lo, lax.rsqrt(s0 * (1.0 / HEAD) + RMS_EPS), lax.rsqrt(s1 * (1.0 / HEAD) + RMS_EPS))
    return xh * inv * gain


def _rope_block(xh, cos, sin_signed):
    first = (_lane_iota(xh.shape) & 32) == 0
    partner = jnp.where(first, pltpu.roll(xh, LANES - 32, 1), pltpu.roll(xh, 32, 1))
    return xh * cos + partner * sin_signed


def _rope_tables(pos):
    half = HEAD // 2
    inv = ROPE_THETA ** (-jnp.arange(half, dtype=F32) / half)
    ang = pos.astype(F32)[:, None] * inv[None, :]
    cos, sin = jnp.cos(ang), jnp.sin(ang)
    return jnp.tile(cos, (1, 4)), jnp.concatenate([-sin, sin, -sin, sin], axis=1)


def _split_hi_lo(x):
    hi = x.astype(BF16)
    lo = (x - hi.astype(F32)).astype(BF16)
    return jnp.concatenate([hi, lo], axis=1)


def _group_ones2():
    r = lax.broadcasted_iota(jnp.int32, (2 * LANES, LANES), 0)
    c = lax.broadcasted_iota(jnp.int32, (2 * LANES, LANES), 1)
    return jnp.where(((r & (LANES - 1)) >> 6) == (c >> 6), 1.0, 0.0).astype(BF16)


def _group_sum(x, g2):
    return jnp.dot(_split_hi_lo(x), g2, preferred_element_type=F32)


def _group_sum_wide(x, g2):
    return jnp.concatenate(
        [_group_sum(x[:, c * LANES:(c + 1) * LANES], g2) for c in range(x.shape[1] // LANES)], axis=1)


def _sigmoid(x):
    return 1.0 / (1.0 + jnp.exp(-x))


def _head_rows(n_rows, h, n_heads):
    return pl.ds(h, n_rows, stride=n_heads)


def _in_proj_kernel(x_ref, g_ref, w_ref, qn_ref, kn_ref, cos_ref, sin_ref,
                    prw_ref, qb_ref, k_ref, v_ref, *flash_refs):
    tm = x_ref.shape[0]
    h = _rms_rows(x_ref[...], g_ref[...]).astype(BF16)
    prw_ref[...] = jnp.dot(h, w_ref[:, :RW_COLS], preferred_element_type=F32)
    pd = jnp.dot(h, w_ref[:, RW_COLS:], preferred_element_type=F32)
    cos, sin = cos_ref[...], sin_ref[...]
    qn, kn = qn_ref[...], kn_ref[...]
    for hh in range(N_DIFF_HEADS):
        sl = slice(hh * LANES, (hh + 1) * LANES)
        q = _rope_block(_rms_half(pd[:, sl], qn), cos, sin) * (HEAD ** -0.5 * LOG2E)
        k = _rope_block(_rms_half(pd[:, DIFF_WIDTH + hh * LANES:DIFF_WIDTH + (hh + 1) * LANES], kn), cos, sin)
        v = pd[:, 2 * DIFF_WIDTH + hh * LANES:2 * DIFF_WIDTH + (hh + 1) * LANES]
        qb_ref[:, sl] = q.astype(BF16)
        k_ref[_head_rows(tm, hh, N_DIFF_HEADS), :] = k
        v_ref[_head_rows(tm, hh, N_DIFF_HEADS), :] = v
        if flash_refs:
            kb_ref, vt_ref = flash_refs
            kb_ref[:, sl] = k.astype(BF16)
            vt_ref[0, sl, :] = jnp.transpose(v).astype(BF16)


def _in_proj(x, g, w_bf, qn, kn, cos, sin, tm, seq_len=None):
    n, d = x.shape
    period = cos.shape[0] // tm
    row = lambda i: (i, 0)
    outs = [jax.ShapeDtypeStruct((n, RW_COLS), F32), jax.ShapeDtypeStruct((n, DIFF_WIDTH), BF16),
            jax.ShapeDtypeStruct((n * N_DIFF_HEADS, LANES), F32), jax.ShapeDtypeStruct((n * N_DIFF_HEADS, LANES), F32)]
    out_specs = [pl.BlockSpec((tm, RW_COLS), row), pl.BlockSpec((tm, DIFF_WIDTH), row),
                 pl.BlockSpec((tm * N_DIFF_HEADS, LANES), row), pl.BlockSpec((tm * N_DIFF_HEADS, LANES), row)]
    if seq_len is not None:
        per_seq = seq_len // tm
        outs += [jax.ShapeDtypeStruct((n, DIFF_WIDTH), BF16), jax.ShapeDtypeStruct((n // seq_len, DIFF_WIDTH, seq_len), BF16)]
        out_specs += [pl.BlockSpec((tm, DIFF_WIDTH), row),
                      pl.BlockSpec((1, DIFF_WIDTH, tm), lambda i: (i // per_seq, 0, i % per_seq))]
    return pl.pallas_call(
        _in_proj_kernel,
        grid=(n // tm,),
        in_specs=[pl.BlockSpec((tm, d), row), _const_spec((1, d)), _const_spec(w_bf.shape),
                  _const_spec((1, LANES)), _const_spec((1, LANES)),
                  pl.BlockSpec((tm, LANES), lambda i: (i % period, 0)),
                  pl.BlockSpec((tm, LANES), lambda i: (i % period, 0))],
        out_specs=tuple(out_specs),
        out_shape=tuple(outs),
        compiler_params=_cparams(("parallel",)),
        name="in_proj",
    )(x, g, w_bf, qn, kn, cos, sin)


def _rwkv_token_vectors(p, shifted, prm, g2):
    mu_ref, w0_ref, w2_ref, a0_ref, a2_ref, g2_ref, kk_ref, ka_ref, rk_ref = prm
    ps = p + (shifted - p) * mu_ref[...]
    r = ps[:, :RW_WIDTH]
    k = ps[:, RW_WIDTH:2 * RW_WIDTH]
    v = ps[:, 2 * RW_WIDTH:3 * RW_WIDTH]
    wa = ps[:, 3 * RW_WIDTH:3 * RW_WIDTH + LANES]
    gd = ps[:, 3 * RW_WIDTH + LANES:]
    z = -(w0_ref[...] + jnp.dot(jnp.tanh(wa).astype(BF16), w2_ref[...], preferred_element_type=F32))
    w_log = -(jnp.maximum(z, 0.0) + jnp.log1p(jnp.exp(-jnp.abs(z)))) - 0.5
    a = _sigmoid(a0_ref[...] + jnp.dot(wa.astype(BF16), a2_ref[...], preferred_element_type=F32))
    gate = jnp.dot(_sigmoid(gd).astype(BF16), g2_ref[...], preferred_element_type=F32)
    kk = k * kk_ref[...]
    kk = kk / jnp.maximum(jnp.sqrt(_group_sum_wide(kk * kk, g2)), 1e-12)
    k2 = k * (1.0 + (a - 1.0) * ka_ref[...])
    bonus = _group_sum_wide(r * k2 * rk_ref[...], g2) * v
    return r, -jnp.exp(w_log), k2, v, -kk, kk * a, bonus, gate


def _rwkv_finish(y, bonus, gate, lw, lb, g2):
    mean = _group_sum_wide(y, g2) * (1.0 / HEAD)
    d = y - mean
    var = _group_sum_wide(d * d, g2) * (1.0 / HEAD)
    return (d * lax.rsqrt(var + LNX_EPS) * lw + lb + bonus) * gate


def _mm(x, w, terms):
    if terms == 1:
        return jnp.dot(x.astype(BF16), w.astype(BF16), preferred_element_type=F32)
    w_hi = w.astype(BF16)
    out = jnp.dot(_split_hi_lo(x), jnp.concatenate([w_hi, w_hi], axis=0), preferred_element_type=F32)
    if terms == 3:
        w_lo = (w - w_hi.astype(F32)).astype(BF16)
        out = out + jnp.dot(x.astype(BF16), w_lo, preferred_element_type=F32)
    return out


def _stack_heads(x):
    lo = _lane_iota(x.shape) < HEAD
    return jnp.concatenate([jnp.where(lo, x, 0.0), jnp.where(lo, 0.0, x)], axis=0)


def _nt_mm(x, w, terms):
    if terms == 1:
        return _nt_dot(x.astype(BF16), w.astype(BF16))
    w_hi = w.astype(BF16)
    out = _nt_dot(_split_hi_lo(x), jnp.concatenate([w_hi, w_hi], axis=1))
    if terms == 3:
        out = out + _nt_dot(x.astype(BF16), (w - w_hi.astype(F32)).astype(BF16))
    return out


def _chunk_units(units, slots, zs, terms):
    n = len(units)
    c2 = 2 * units[0][0].shape[0]
    ti = lax.broadcasted_iota(jnp.int32, (c2, c2), 0)
    tj = lax.broadcasted_iota(jnp.int32, (c2, c2), 1)
    eye = ti == tj
    strict = (ti & (c2 // 2 - 1)) > (tj & (c2 // 2 - 1))
    incl = (ti & (c2 // 2 - 1)) >= (tj & (c2 // 2 - 1))
    s_a = [_stack_heads(u[0]) for u in units]
    s_r = [_stack_heads(u[1]) for u in units]
    s_v = [_stack_heads(u[6]) for u in units]
    sc = [_nt_mm(jnp.concatenate([s_a[i], s_r[i]], axis=0),
                 jnp.concatenate([_stack_heads(units[i][2]), _stack_heads(units[i][3])], axis=0), terms)
          for i in range(n)]
    x = [jnp.where(strict, s[:c2, :c2], 0.0) for s in sc]
    l_ak = [jnp.where(strict, s[:c2, c2:], 0.0) for s in sc]
    m_rbk = [jnp.concatenate([jnp.where(incl, s[c2:, :c2], 0.0), jnp.where(incl, s[c2:, c2:], 0.0)], axis=1) for s in sc]
    t_inv = [jnp.where(eye, 1.0, 0.0) + xi for xi in x]
    lv = [_mm(l_ak[i], s_v[i], terms) for i in range(n)]
    for _ in range(max(1, (c2 // 2 - 1).bit_length()) - 1):
        x = [_mm(xi, xi, terms) for xi in x]
        t_inv = [t_inv[i] + _mm(t_inv[i], x[i], terms) for i in range(n)]
    pq = [_mm(t_inv[i], jnp.concatenate([s_a[i], lv[i]], axis=1), terms) for i in range(n)]
    pqv = [jnp.concatenate([pq[i], jnp.concatenate([jnp.zeros_like(s_v[i]), s_v[i]], axis=1)], axis=0) for i in range(n)]
    ah = [_mm(jnp.transpose(jnp.concatenate([_stack_heads(units[i][4]), _stack_heads(units[i][5])], axis=0)), pqv[i], terms)
          for i in range(n)]
    ry = [_mm(m_rbk[i], pqv[i], terms) for i in range(n)]
    zs, ys = list(zs), []
    for i in range(n):
        z = zs[slots[i]]
        y = _mm(s_r[i] + ry[i][:, :c2], z, 3) + ry[i][:, c2:]
        ys.append(y[:c2 // 2] + y[c2 // 2:])
        zs[slots[i]] = _mm(jnp.where(eye, units[i][7], 0.0) + ah[i][:, :c2], z, 3) + ah[i][:, c2:]
    return ys, zs


def _rwkv_kernel(p_ref, prev_ref, s0_ref, mu_ref, w0_ref, w2_ref, a0_ref, a2_ref, g2_ref,
                 kk_ref, ka_ref, rk_ref, lw_ref, lb_ref,
                 out_ref, st_ref,
                 state, carry, r_s, w_s, k_s, v_s, a_s, b_s, y_s, bonus_s, gate_s, *, ns, tt):
    ti = pl.program_id(1)
    npair = RW_WIDTH // LANES
    g2 = _group_ones2()

    @pl.when(ti == 0)
    def _():
        state[...] = s0_ref[...]
        carry[...] = prev_ref[...]

    row0 = lax.broadcasted_iota(jnp.int32, (tt, RW_COLS), 0) == 0
    prm = (mu_ref, w0_ref, w2_ref, a0_ref, a2_ref, g2_ref, kk_ref, ka_ref, rk_ref)
    for s in range(ns):
        p = p_ref[s]
        shifted = jnp.where(row0, carry[s], pltpu.roll(p, 1, 0))
        carry[s] = p[tt - 1:tt, :]
        r_s[s], log_w, k_s[s], v_s[s], a_s[s], b_s[s], bonus_s[s], gate_s[s] = _rwkv_token_vectors(p, shifted, prm, g2)
        w_s[s] = jnp.exp(log_w)

    tiles = [(s, j) for s in range(ns) for j in range(npair)]
    nt = len(tiles)
    sub = lax.broadcasted_iota(jnp.int32, (nt * HEAD, LANES), 0)
    e1 = jnp.where((_lane_iota((nt * HEAD, LANES)) & (HEAD - 1)) == (sub & (HEAD - 1)), 1.0, 0.0)
    seqs = dict(r=r_s, w=w_s, k=k_s, v=v_s, a=a_s, b=b_s)

    def block(tb, c):
        base = pl.multiple_of(tb * 8, 8)
        rows8 = {q: [ref[s, pl.ds(base, 8), j * LANES:(j + 1) * LANES] for (s, j) in tiles] for q, ref in seqs.items()}
        ys = []
        for i in range(8):
            bc = lambda q: jnp.concatenate(
                [jnp.broadcast_to(rows8[q][n][i:i + 1], (HEAD, LANES)) for n in range(nt)], axis=0)
            st = state[...]
            red = jnp.dot(jnp.concatenate([_split_hi_lo(st * bc("a")), _split_hi_lo(e1 * bc("v"))], axis=0), g2,
                          preferred_element_type=F32)
            st = st * bc("w") + red[:nt * HEAD] * bc("b") + red[nt * HEAD:] * bc("k")
            state[...] = st
            yred = jnp.dot(_split_hi_lo(st * bc("r")), g2, preferred_element_type=F32)
            ys.append(jnp.sum((e1 * yred).reshape(nt, HEAD, LANES), axis=1))
        for n, (s, j) in enumerate(tiles):
            y_s[s, pl.ds(base, 8), j * LANES:(j + 1) * LANES] = jnp.concatenate([y[n:n + 1] for y in ys], axis=0)
        return c

    lax.fori_loop(0, tt // 8, block, 0)

    for s in range(ns):
        out_ref[s] = _rwkv_finish(y_s[s], bonus_s[s], gate_s[s], lw_ref[...], lb_ref[...], g2)

    @pl.when(ti == pl.num_programs(1) - 1)
    def _():
        st_ref[...] = state[...]


def _rwkv_chunk_kernel(p_ref, prev_ref, z0_ref, mu_ref, w0_ref, w2_ref, a0_ref, a2_ref, g2_ref,
                       kk_ref, ka_ref, rk_ref, lw_ref, lb_ref,
                       out_ref, zt_ref,
                       zstate, carry, r_s, g_s, k_s, v_s, a_s, b_s, y_s, bonus_s, gate_s, *, tt, chunk, terms):
    ti = pl.program_id(1)
    npair = RW_WIDTH // LANES
    g2 = _group_ones2()

    @pl.when(ti == 0)
    def _():
        zstate[...] = z0_ref[0]
        carry[...] = prev_ref[0]

    p = p_ref[0]
    row0 = lax.broadcasted_iota(jnp.int32, (tt, RW_COLS), 0) == 0
    shifted = jnp.where(row0, carry[...], pltpu.roll(p, 1, 0))
    carry[...] = p[tt - 1:tt, :]
    prm = (mu_ref, w0_ref, w2_ref, a0_ref, a2_ref, g2_ref, kk_ref, ka_ref, rk_ref)
    (r_s[...], g_s[...], k_s[...], v_s[...], a_s[...], b_s[...], bonus_s[...], gate_s[...]) = _rwkv_token_vectors(
        p, shifted, prm, g2)

    ci = lax.broadcasted_iota(jnp.int32, (chunk, 3 * chunk), 0)
    cj = lax.broadcasted_iota(jnp.int32, (chunk, 3 * chunk), 1)
    cj = jnp.where(cj >= 2 * chunk, cj - 2 * chunk, jnp.where(cj >= chunk, cj - chunk, cj))
    ltri3 = jnp.where(cj <= ci, 1.0, 0.0).astype(BF16)

    lanes = [slice(j * LANES, (j + 1) * LANES) for j in range(npair)]

    def scaled(c):
        rows = pl.ds(pl.multiple_of(c * chunk, chunk), chunk)
        g = g_s[rows, :]
        g_hi = g.astype(BF16)
        g_r = g - g_hi.astype(F32)
        g_mid = g_r.astype(BF16)
        g_lo = (g_r - g_mid.astype(F32)).astype(BF16)
        cum = jnp.dot(ltri3, jnp.concatenate([g_hi, g_mid, g_lo], axis=0), preferred_element_type=F32)
        total = cum[chunk - 1:chunk, :]
        e_in, e_ex, e_inv, e_rest = jnp.exp(cum), jnp.exp(cum - g), jnp.exp(-cum), jnp.exp(total - cum)
        a_t, r_t = a_s[rows, :] * e_ex, r_s[rows, :] * e_in
        b, k, v = b_s[rows, :], k_s[rows, :], v_s[rows, :]
        ops = (a_t, r_t, b * e_inv, k * e_inv, b * e_rest, k * e_rest, v, jnp.exp(total))
        return rows, [tuple(t[:, sl] for t in ops) for sl in lanes]

    def body(c2, carry_):
        rows_a, units_a = scaled(2 * c2)
        rows_b, units_b = scaled(2 * c2 + 1)
        ys, z_new = _chunk_units(units_a + units_b, list(range(npair)) * 2, [zstate[sl, :] for sl in lanes], terms)
        for j, sl in enumerate(lanes):
            y_s[rows_a, sl] = ys[j]
            y_s[rows_b, sl] = ys[npair + j]
            zstate[sl, :] = z_new[j]
        return carry_

    lax.fori_loop(0, tt // (2 * chunk), body, 0)
    out_ref[0] = _rwkv_finish(y_s[...], bonus_s[...], gate_s[...], lw_ref[...], lb_ref[...], g2)

    @pl.when(ti == pl.num_programs(1) - 1)
    def _():
        zt_ref[0] = zstate[...]


def _rwkv(prw, prev, s0_pairs, rw, ns, tt):
    S, T, _ = prw.shape
    npair = RW_WIDTH // LANES
    vec = lambda n: _const_spec((1, n))
    blk = lambda w: pl.BlockSpec((ns, tt, w), lambda i, t: (i, t, 0))
    st_spec = pl.BlockSpec((ns * npair * HEAD, LANES), lambda i, t: (i, 0))
    seq = lambda: pltpu.VMEM((ns, tt, RW_WIDTH), F32)
    return pl.pallas_call(
        functools.partial(_rwkv_kernel, ns=ns, tt=tt),
        grid=(S // ns, T // tt),
        in_specs=[blk(RW_COLS), pl.BlockSpec((ns, 1, RW_COLS), lambda i, t: (i, 0, 0)), st_spec,
                  vec(RW_COLS), vec(RW_WIDTH), _const_spec((LANES, RW_WIDTH)), vec(RW_WIDTH),
                  _const_spec((LANES, RW_WIDTH)), _const_spec((LANES, RW_WIDTH)),
                  vec(RW_WIDTH), vec(RW_WIDTH), vec(RW_WIDTH), vec(RW_WIDTH), vec(RW_WIDTH)],
        out_specs=(blk(RW_WIDTH), st_spec),
        out_shape=(jax.ShapeDtypeStruct((S, T, RW_WIDTH), F32),
                   jax.ShapeDtypeStruct((S * npair * HEAD, LANES), F32)),
        scratch_shapes=[pltpu.VMEM((ns * npair * HEAD, LANES), F32), pltpu.VMEM((ns, 1, RW_COLS), F32)]
                       + [seq() for _ in range(9)],
        compiler_params=_cparams(("parallel", "arbitrary")),
        name="rwkv",
    )(prw, prev, s0_pairs, *rw)


def _rwkv_chunked(prw, prev, z0, rw, tt, chunk, terms):
    S, T, _ = prw.shape
    vec = lambda n: _const_spec((1, n))
    blk = lambda w: pl.BlockSpec((1, tt, w), lambda i, t: (i, t, 0))
    z_spec = pl.BlockSpec((1, RW_WIDTH, LANES), lambda i, t: (i, 0, 0))
    seq = lambda: pltpu.VMEM((tt, RW_WIDTH), F32)
    return pl.pallas_call(
        functools.partial(_rwkv_chunk_kernel, tt=tt, chunk=chunk, terms=terms),
        grid=(S, T // tt),
        in_specs=[blk(RW_COLS), pl.BlockSpec((1, 1, RW_COLS), lambda i, t: (i, 0, 0)), z_spec,
                  vec(RW_COLS), vec(RW_WIDTH), _const_spec((LANES, RW_WIDTH)), vec(RW_WIDTH),
                  _const_spec((LANES, RW_WIDTH)), _const_spec((LANES, RW_WIDTH)),
                  vec(RW_WIDTH), vec(RW_WIDTH), vec(RW_WIDTH), vec(RW_WIDTH), vec(RW_WIDTH)],
        out_specs=(blk(RW_WIDTH), z_spec),
        out_shape=(jax.ShapeDtypeStruct((S, T, RW_WIDTH), F32), jax.ShapeDtypeStruct((S, RW_WIDTH, LANES), F32)),
        scratch_shapes=[pltpu.VMEM((RW_WIDTH, LANES), F32), pltpu.VMEM((1, RW_COLS), F32)] + [seq() for _ in range(9)],
        compiler_params=_cparams(("parallel", "arbitrary")),
        name="rwkv_chunked",
    )(prw, prev, z0, *rw)


def _state_from_blockdiag(z, S):
    z = z.reshape(S, 4, 2, HEAD, 2, HEAD)
    z = jnp.stack([z[:, :, 0, :, 0, :], z[:, :, 1, :, 1, :]], axis=2)
    return z.reshape(S, 8, HEAD, HEAD).transpose(0, 1, 3, 2)


def _lambda(lamp_ref, lam_init):
    lp = lamp_ref[...]
    s1 = jnp.sum(lp[0:1] * lp[1:2], axis=-1, keepdims=True)
    s2 = jnp.sum(lp[2:3] * lp[3:4], axis=-1, keepdims=True)
    return jnp.exp(s1) - jnp.exp(s2) + lam_init


def _stack_halves(q):
    lo = _lane_iota(q.shape) < HEAD
    zero = jnp.zeros_like(q)
    return jnp.concatenate([jnp.where(lo, q, zero), jnp.where(lo, zero, q)], axis=0)


def _diff_finish(acc, l, lam, sn, lam_init, rows):
    o = acc[:rows] / l[:rows] - lam * (acc[rows:] / l[rows:])
    return _rms_rows(o, sn) * (1.0 - lam_init)


def _diff_prompt_kernel(q_ref, k_ref, vt_ref, lamp_ref, sn_ref, o_ref, m_ref, acc_ref, *, tq, hp, lam_init):
    i = pl.program_id(2)
    heads = [slice(h * LANES, (h + 1) * LANES) for h in range(hp)]
    q2 = [_stack_halves(q_ref[0, :, sl]) for sl in heads]
    start = pl.multiple_of(i * tq, tq)
    s = [_nt_dot(k_ref[0, pl.ds(start, tq), sl], q2[h]) for h, sl in enumerate(heads)]
    key = lax.broadcasted_iota(jnp.int32, s[0].shape, 0)
    qry = lax.broadcasted_iota(jnp.int32, s[0].shape, 1)
    causal = key <= jnp.where(qry >= tq, qry - tq, qry)
    s = [jnp.where(causal, sh, NEG_BIG) for sh in s]
    m = [jnp.max(sh, axis=0, keepdims=True) for sh in s]
    p = [jnp.exp2(s[h] - m[h]) for h in range(hp)]
    ones = jnp.ones((ONES_ROWS, tq), BF16)

    def values(off, sl):
        return jnp.concatenate([vt_ref[0, sl, pl.ds(off, tq)], ones], axis=0)

    for h, sl in enumerate(heads):
        m_ref[h] = m[h]
        acc_ref[h] = jnp.dot(values(start, sl), p[h].astype(BF16), preferred_element_type=F32)

    def body(j, c):
        off = pl.multiple_of(j * tq, tq)
        s = [_nt_dot(k_ref[0, pl.ds(off, tq), sl], q2[h]) for h, sl in enumerate(heads)]
        m_old = [m_ref[h] for h in range(hp)]
        m_new = [jnp.maximum(m_old[h], jnp.max(s[h], axis=0, keepdims=True)) for h in range(hp)]
        alpha = [jnp.exp2(m_old[h] - m_new[h]) for h in range(hp)]
        p = [jnp.exp2(s[h] - m_new[h]) for h in range(hp)]
        pv = [jnp.dot(values(off, sl), p[h].astype(BF16), preferred_element_type=F32) for h, sl in enumerate(heads)]
        for h in range(hp):
            acc_ref[h] = alpha[h] * acc_ref[h] + pv[h]
            m_ref[h] = m_new[h]
        return c

    lax.fori_loop(0, i, body, 0)
    lam = _lambda(lamp_ref, lam_init)
    for h, sl in enumerate(heads):
        acc = acc_ref[h]
        acc = acc[:LANES] / acc[LANES:LANES + 1]
        o_t = acc[:, :tq] - lam * acc[:, tq:]
        ms = jnp.mean(o_t * o_t, axis=0, keepdims=True)
        o_t = o_t * lax.rsqrt(ms + RMS_EPS)
        o_ref[0, :, sl] = jnp.transpose(o_t) * sn_ref[...] * (1.0 - lam_init)


def _diff_prompt(qb, kb, vt, lamp, sn, tq, hp, lam_init):
    B, T, _ = qb.shape
    tile = pl.BlockSpec((1, tq, hp * LANES), lambda b, h, i: (b, i, h))
    return pl.pallas_call(
        functools.partial(_diff_prompt_kernel, tq=tq, hp=hp, lam_init=lam_init),
        grid=(B, N_DIFF_HEADS // hp, T // tq),
        in_specs=[tile, pl.BlockSpec((1, T, hp * LANES), lambda b, h, i: (b, 0, h)),
                  pl.BlockSpec((1, hp * LANES, T), lambda b, h, i: (b, h, 0)),
                  _const_spec((4, HEAD)), _const_spec((1, LANES))],
        out_specs=tile,
        out_shape=jax.ShapeDtypeStruct((B, T, DIFF_WIDTH), F32),
        scratch_shapes=[pltpu.VMEM((hp, 1, 2 * tq), F32), pltpu.VMEM((hp, LANES + ONES_ROWS, 2 * tq), F32)],
        compiler_params=_cparams(("parallel", "parallel", "arbitrary")),
        name="diff_prompt",
    )(qb, kb, vt, lamp, sn)


def _diff_sample_kernel(pt_ref, q_ref, kn_ref, vn_ref, lamp_ref, sn_ref, *rest, pp, s_new, lam_init):
    del pt_ref
    kpages, vpages = rest[:pp], rest[pp:2 * pp]
    o_ref, m_ref, l_ref, acc_ref = rest[2 * pp:]
    j = pl.program_id(1)
    nh = N_DIFF_HEADS
    rows = 2 * s_new
    q = q_ref[0]
    q2 = [_stack_halves(q[:, h * LANES:(h + 1) * LANES]) for h in range(nh)]

    def update(scores, values):
        s = jnp.concatenate([jnp.concatenate(sc, axis=1) if len(sc) > 1 else sc[0] for sc in scores], axis=0)
        m_old = m_ref[...]
        m_new = jnp.maximum(m_old, jnp.max(s, axis=-1, keepdims=True))
        alpha = jnp.exp2(m_old - m_new)
        p = jnp.exp2(s - m_new)
        pv = []
        for h in range(nh):
            ph = p[h * rows:(h + 1) * rows].astype(BF16)
            acc = jnp.dot(ph[:, :PAGE], values[h][0], preferred_element_type=F32)
            for c in range(1, len(values[h])):
                acc = acc + jnp.dot(ph[:, c * PAGE:(c + 1) * PAGE], values[h][c], preferred_element_type=F32)
            pv.append(acc)
        l_ref[...] = alpha * l_ref[...] + jnp.sum(p, axis=-1, keepdims=True)
        acc_ref[...] = alpha * acc_ref[...] + jnp.concatenate(pv, axis=0)
        m_ref[...] = m_new

    @pl.when(j == 0)
    def _():
        m_ref[...] = jnp.full(m_ref.shape, NEG_BIG, F32)
        l_ref[...] = jnp.zeros(l_ref.shape, F32)
        acc_ref[...] = jnp.zeros(acc_ref.shape, F32)
        pad = jnp.zeros((PAGE - s_new, LANES), F32)
        row = lax.broadcasted_iota(jnp.int32, (rows, PAGE), 0)
        col = lax.broadcasted_iota(jnp.int32, (rows, PAGE), 1)
        mask = col <= jnp.where(row >= s_new, row - s_new, row)
        new_rows = lambda ref, h: jnp.concatenate([ref[0, _head_rows(s_new, h, nh), :], pad], axis=0).astype(BF16)
        update([[jnp.where(mask, _nt_dot(q2[h], new_rows(kn_ref, h)), NEG_BIG)] for h in range(nh)],
               [[new_rows(vn_ref, h)] for h in range(nh)])

    page_rows = lambda ref, h: ref[_head_rows(PAGE, h, nh), :].astype(BF16)
    update([[_nt_dot(q2[h], page_rows(kp, h)) for kp in kpages] for h in range(nh)],
           [[page_rows(vp, h) for vp in vpages] for h in range(nh)])

    @pl.when(j == pl.num_programs(1) - 1)
    def _():
        lam = _lambda(lamp_ref, lam_init)
        acc, l = acc_ref[...], l_ref[...]
        for h in range(nh):
            o_ref[0, :, h * LANES:(h + 1) * LANES] = _diff_finish(
                acc[h * rows:(h + 1) * rows], l[h * rows:(h + 1) * rows], lam, sn_ref[...], lam_init, s_new)


def _diff_sample(page_table, qb, k_new, v_new, pool_k, pool_v, lamp, sn, pp, lam_init):
    Bs, S, _ = qb.shape
    n_pages = page_table.shape[1]
    nh = N_DIFF_HEADS
    qspec = pl.BlockSpec((1, S, DIFF_WIDTH), lambda b, j, pt: (b, 0, 0))
    new = pl.BlockSpec((1, S * nh, LANES), lambda b, j, pt: (b, 0, 0))
    page = lambda c: pl.BlockSpec((None, PAGE * nh, LANES), lambda b, j, pt, c=c: (pt[b, j * pp + c], 0, 0))
    cst = lambda shape: pl.BlockSpec(shape, lambda b, j, pt: (0,) * len(shape))
    return pl.pallas_call(
        functools.partial(_diff_sample_kernel, pp=pp, s_new=S, lam_init=lam_init),
        grid_spec=pltpu.PrefetchScalarGridSpec(
            num_scalar_prefetch=1,
            grid=(Bs, n_pages // pp),
            in_specs=[qspec, new, new, cst((4, HEAD)), cst((1, LANES))] + [page(c) for c in range(pp)] * 2,
            out_specs=qspec,
            scratch_shapes=[pltpu.VMEM((nh * 2 * S, 1), F32), pltpu.VMEM((nh * 2 * S, 1), F32),
                            pltpu.VMEM((nh * 2 * S, LANES), F32)]),
        out_shape=jax.ShapeDtypeStruct((Bs, S, DIFF_WIDTH), F32),
        compiler_params=_cparams(("parallel", "arbitrary")),
        name="diff_sample",
    )(page_table, qb, k_new, v_new, lamp, sn, *([pool_k] * pp), *([pool_v] * pp))


def _out_proj_kernel(x_ref, r_ref, o_ref, w_ref, y_ref):
    y = x_ref[...] + jnp.dot(r_ref[...].astype(BF16), w_ref[:RW_WIDTH, :], preferred_element_type=F32)
    y_ref[...] = y + jnp.dot(o_ref[...].astype(BF16), w_ref[RW_WIDTH:, :], preferred_element_type=F32)


def _out_proj(x, r_out, o, w_bf, tm):
    n, d = x.shape
    row = lambda i: (i, 0)
    return pl.pallas_call(
        _out_proj_kernel,
        grid=(n // tm,),
        in_specs=[pl.BlockSpec((tm, d), row), pl.BlockSpec((tm, RW_WIDTH), row), pl.BlockSpec((tm, DIFF_WIDTH), row),
                  _const_spec(w_bf.shape)],
        out_specs=pl.BlockSpec((tm, d), row),
        out_shape=jax.ShapeDtypeStruct((n, d), F32),
        compiler_params=_cparams(("parallel",)),
        name="out_proj",
    )(x, r_out, o, w_bf)


def _mem_kv_kernel(x_ref, g_ref, w_ref, kn_ref, k_ref, v_ref):
    h = _rms_rows(x_ref[...], g_ref[...]).astype(BF16)
    kv = jnp.dot(h, w_ref[...], preferred_element_type=F32)
    tm = x_ref.shape[0]
    for hh in range(N_MEM_HEADS):
        sl = slice(hh * LANES, (hh + 1) * LANES)
        k_ref[_head_rows(tm, hh, N_MEM_HEADS), :] = _rms_rows(kv[:, sl], kn_ref[...])
        v_ref[_head_rows(tm, hh, N_MEM_HEADS), :] = kv[:, MEM_WIDTH + hh * LANES:MEM_WIDTH + (hh + 1) * LANES]


def _mem_kv(mem, g, w_bf, kn, tm):
    n, d = mem.shape
    row = lambda i: (i, 0)
    return pl.pallas_call(
        _mem_kv_kernel,
        grid=(n // tm,),
        in_specs=[pl.BlockSpec((tm, d), row), _const_spec((1, d)), _const_spec(w_bf.shape), _const_spec((1, LANES))],
        out_specs=(pl.BlockSpec((tm * N_MEM_HEADS, LANES), row),) * 2,
        out_shape=(jax.ShapeDtypeStruct((n * N_MEM_HEADS, LANES), F32),) * 2,
        compiler_params=_cparams(("parallel",)),
        name="mem_kv",
    )(mem, g, w_bf, kn)


def _mem_attn_kernel(x_ref, mk_ref, mv_ref, g_ref, wq_ref, qn_ref, wo_ref, y_ref, o_scr, *, nb, tr, n_mem):
    d = x_ref.shape[-1]
    x = x_ref[...].reshape(nb * tr, d)
    h = _rms_rows(x, g_ref[...]).astype(BF16)
    q = jnp.dot(h, wq_ref[...], preferred_element_type=F32)
    for hh in range(N_MEM_HEADS):
        sl = slice(hh * LANES, (hh + 1) * LANES)
        qh = (_rms_rows(q[:, sl], qn_ref[...]) * (LANES ** -0.5)).astype(BF16)
        for n in range(nb):
            mem_rows = _head_rows(n_mem, hh, N_MEM_HEADS)
            s = _nt_dot(qh[n * tr:(n + 1) * tr], mk_ref[n, mem_rows, :].astype(BF16))
            p = jnp.exp(s - jnp.max(s, axis=-1, keepdims=True))
            p = p / jnp.sum(p, axis=-1, keepdims=True)
            o_scr[n * tr:(n + 1) * tr, sl] = jnp.dot(p.astype(BF16), mv_ref[n, mem_rows, :].astype(BF16),
                                                     preferred_element_type=F32)
    y = x + jnp.dot(o_scr[...].astype(BF16), wo_ref[...], preferred_element_type=F32)
    y_ref[...] = y.reshape(nb, tr, d)


def _mem_attn(x, mk, mv, g, wq_bf, qn, wo_bf, nb, tr):
    G, R, d = x.shape
    M4 = mk.shape[1]
    return pl.pallas_call(
        functools.partial(_mem_attn_kernel, nb=nb, tr=tr, n_mem=M4 // N_MEM_HEADS),
        grid=(G // nb, R // tr),
        in_specs=[pl.BlockSpec((nb, tr, d), lambda i, t: (i, t, 0)),
                  pl.BlockSpec((nb, M4, LANES), lambda i, t: (i, 0, 0)),
                  pl.BlockSpec((nb, M4, LANES), lambda i, t: (i, 0, 0)),
                  _const_spec((1, d)), _const_spec(wq_bf.shape), _const_spec((1, LANES)), _const_spec(wo_bf.shape)],
        out_specs=pl.BlockSpec((nb, tr, d), lambda i, t: (i, t, 0)),
        out_shape=jax.ShapeDtypeStruct((G, R, d), F32),
        scratch_shapes=[pltpu.VMEM((nb * tr, MEM_WIDTH), F32)],
        compiler_params=_cparams(("parallel", "arbitrary")),
        name="mem_attn",
    )(x, mk, mv, g, wq_bf, qn, wo_bf)


def _ffn_kernel(x_ref, g_ref, wgu_ref, wd_ref, y_ref, *, d_ff, fc):
    x = x_ref[...]
    h = _rms_rows(x, g_ref[...]).astype(BF16)
    y = x
    for c in range(d_ff // fc):
        gt = jnp.dot(h, wgu_ref[:, c * fc:(c + 1) * fc], preferred_element_type=F32)
        up = jnp.dot(h, wgu_ref[:, d_ff + c * fc:d_ff + (c + 1) * fc], preferred_element_type=F32)
        act = (gt * _sigmoid(gt) * up).astype(BF16)
        y = y + jnp.dot(act, wd_ref[c * fc:(c + 1) * fc, :], preferred_element_type=F32)
    y_ref[...] = y


def _ffn(x, g, wgu_bf, wd_bf, tm):
    n, d = x.shape
    d_ff = wd_bf.shape[0]
    fc = d_ff // 2 if (d_ff // 2) % LANES == 0 else d_ff
    row = lambda i: (i, 0)
    return pl.pallas_call(
        functools.partial(_ffn_kernel, d_ff=d_ff, fc=fc),
        grid=(n // tm,),
        in_specs=[pl.BlockSpec((tm, d), row), _const_spec((1, d)), _const_spec(wgu_bf.shape), _const_spec(wd_bf.shape)],
        out_specs=pl.BlockSpec((tm, d), row),
        out_shape=jax.ShapeDtypeStruct((n, d), F32),
        compiler_params=_cparams(("parallel",)),
        name="ffn",
    )(x, g, wgu_bf, wd_bf)


def _pairs_from_state(s):
    S = s.shape[0]
    return s.reshape(S, 4, 2, HEAD, HEAD).transpose(0, 1, 3, 2, 4).reshape(S * 4 * HEAD, LANES)


def _state_from_pairs(t, S):
    return t.reshape(S, 4, HEAD, 2, HEAD).transpose(0, 1, 3, 2, 4).reshape(S, 8, HEAD, HEAD)


def _tile(n, pref):
    t = min(n, pref)
    while n % t:
        t //= 2
    return t


def kernel(x_prompt, x_sample, cache_diff_k, cache_diff_v, cache_mem_k, cache_mem_v, state_rwkv, state_shift, page_table, mem_prompt, norm_mix, w_in, rwkv_mu, rwkv_w0, rwkv_w2, rwkv_a0, rwkv_a2, rwkv_g2, rwkv_k_k, rwkv_k_a, rwkv_r_k, rwkv_lnx_w, rwkv_lnx_b, diff_q_norm, diff_k_norm, diff_lam_q1, diff_lam_k1, diff_lam_q2, diff_lam_k2, diff_sub_norm, w_o, norm_mem, norm_memkv, w_mq, w_mkv, mem_q_norm, mem_k_norm, w_mo, norm_ffn, w_gate_up, w_down):
    B, T, D = x_prompt.shape
    Bs, S, _ = x_sample.shape
    depth = w_in.shape[0]
    n_pages = page_table.shape[1]
    past = n_pages * PAGE
    M = mem_prompt.shape[1]
    xp = x_prompt.reshape(B * T, D)
    xs = x_sample.reshape(Bs * S, D)
    mem = mem_prompt.reshape(B * M, D)

    tm_p = _tile(B * T, 256)
    tm_p = _tile(T, tm_p)
    tm_s = _tile(Bs * S, 256)
    cos_p, sin_p = _rope_tables(jnp.arange(T, dtype=jnp.int32))
    cos_s, sin_s = _rope_tables(past + jnp.arange(S, dtype=jnp.int32))
    cos_s, sin_s = jnp.tile(cos_s, (tm_s // S, 1)), jnp.tile(sin_s, (tm_s // S, 1))
    tq = _tile(T, 512)
    pp = _tile(n_pages, 16)
    ns_s = _tile(Bs, 4)
    nb_mem = _tile(Bs, 16)
    tt_p = _tile(T, 256)

    outs = [[] for _ in range(10)]
    for l in range(depth):
        lam_init = 0.8 - 0.6 * math.exp(-0.3 * l)
        row = lambda a: a[l].reshape(1, -1)
        bf = lambda a: a[l].astype(BF16)
        zpad = jnp.zeros((HEAD, RW_WIDTH), F32)
        rw = (row(rwkv_mu), row(rwkv_w0), jnp.concatenate([rwkv_w2[l], zpad], axis=0).astype(BF16), row(rwkv_a0),
              jnp.concatenate([zpad, rwkv_a2[l]], axis=0).astype(BF16), bf(rwkv_g2), row(rwkv_k_k), row(rwkv_k_a),
              row(rwkv_r_k), row(rwkv_lnx_w), row(rwkv_lnx_b))
        lamp = jnp.stack([diff_lam_q1[l], diff_lam_k1[l], diff_lam_q2[l], diff_lam_k2[l]])
        w_in_bf, w_o_bf, w_mq_bf, w_mkv_bf, w_mo_bf = bf(w_in), bf(w_o), bf(w_mq), bf(w_mkv), bf(w_mo)
        w_gu_bf, w_d_bf = bf(w_gate_up), bf(w_down)
        qn, kn, sn = row(diff_q_norm), row(diff_k_norm), row(diff_sub_norm)

        prw, qb, k, v, kb, vt = _in_proj(xp, row(norm_mix), w_in_bf, qn, kn, cos_p, sin_p, tm_p, seq_len=T)
        prw3 = prw.reshape(B, T, RW_COLS)
        r_out, z_p = _rwkv_chunked(prw3, jnp.zeros((B, 1, RW_COLS), F32), jnp.zeros((B, RW_WIDTH, LANES), F32), rw,
                                   tt_p, _tile(tt_p, RWKV_CHUNK), RWKV_TERMS)
        o = _diff_prompt(qb.reshape(B, T, -1), kb.reshape(B, T, -1), vt, lamp, sn, tq, 2, lam_init)
        x1 = _out_proj(xp, r_out.reshape(B * T, -1), o.reshape(B * T, -1), w_o_bf, tm_p)
        mk, mv = _mem_kv(mem, row(norm_memkv), w_mkv_bf, row(mem_k_norm), _tile(M, 256))
        x2 = _mem_attn(x1.reshape(B, T, D), mk.reshape(B, M * N_MEM_HEADS, LANES), mv.reshape(B, M * N_MEM_HEADS, LANES),
                       row(norm_mem), w_mq_bf, row(mem_q_norm), w_mo_bf, 1, tm_p)
        xp = _ffn(x2.reshape(B * T, D), row(norm_ffn), w_gu_bf, w_d_bf, tm_p)
        outs[0].append(k.reshape(B, T, N_DIFF_HEADS, LANES))
        outs[1].append(v.reshape(B, T, N_DIFF_HEADS, LANES))
        outs[2].append(mk.reshape(B, M, N_MEM_HEADS, LANES))
        outs[3].append(mv.reshape(B, M, N_MEM_HEADS, LANES))
        outs[4].append(_state_from_blockdiag(z_p, B))
        outs[5].append(prw3[:, -1])

        prw, qb, k, v = _in_proj(xs, row(norm_mix), w_in_bf, qn, kn, cos_s, sin_s, tm_s)
        prw3 = prw.reshape(Bs, S, RW_COLS)
        r_out, st_s = _rwkv(prw3, state_shift[l][:, None, :], _pairs_from_state(state_rwkv[l]), rw, ns_s, S)
        o = _diff_sample(page_table, qb.reshape(Bs, S, -1), k.reshape(Bs, S * N_DIFF_HEADS, LANES),
                         v.reshape(Bs, S * N_DIFF_HEADS, LANES),
                         cache_diff_k[l].reshape(-1, PAGE * N_DIFF_HEADS, LANES),
                         cache_diff_v[l].reshape(-1, PAGE * N_DIFF_HEADS, LANES), lamp, sn, pp, lam_init)
        x1 = _out_proj(xs, r_out.reshape(Bs * S, -1), o.reshape(Bs * S, -1), w_o_bf, tm_s)
        x2 = _mem_attn(x1.reshape(Bs, S, D), cache_mem_k[l].reshape(Bs, M * N_MEM_HEADS, LANES),
                       cache_mem_v[l].reshape(Bs, M * N_MEM_HEADS, LANES),
                       row(norm_mem), w_mq_bf, row(mem_q_norm), w_mo_bf, nb_mem, S)
        xs = _ffn(x2.reshape(Bs * S, D), row(norm_ffn), w_gu_bf, w_d_bf, tm_s)
        outs[6].append(k.reshape(Bs, S, N_DIFF_HEADS, LANES))
        outs[7].append(v.reshape(Bs, S, N_DIFF_HEADS, LANES))
        outs[8].append(_state_from_pairs(st_s, Bs))
        outs[9].append(prw3[:, -1])

    st = [jnp.stack(o) for o in outs]
    return (xp.reshape(B, T, D), xs.reshape(Bs, S, D), st[0], st[1], st[2], st[3], st[4], st[5],
            st[6], st[7], st[8], st[9])
```

```python
import functools
import math

import jax
import jax.numpy as jnp
from jax import lax
from jax.experimental import pallas as pl
from jax.experimental.pallas import tpu as pltpu

F32 = jnp.float32
BF16 = jnp.bfloat16

LANES = 128
HEAD = 64
RW_WIDTH = 512
RW_COLS = 3 * RW_WIDTH + 64 + 64 + 128
DIFF_WIDTH = 512
N_DIFF_HEADS = 4
MEM_WIDTH = 512
N_MEM_HEADS = 4
PAGE = 128
ROPE_THETA = 10000.0
RMS_EPS = 1e-6
LNX_EPS = 1e-5 * HEAD
NEG_BIG = -1e30
VMEM_LIMIT = 56 * 1024 * 1024
RWKV_CHUNK = 64
RWKV_TERMS = 1
RWKV_CHUNKS_PER_TRIP = 4
ONES_ROWS = 16
LOG2E = 1.4426950408889634


def _cparams(sem):
    return pltpu.CompilerParams(dimension_semantics=sem, vmem_limit_bytes=VMEM_LIMIT)


def _const_spec(shape):
    nd = len(shape)
    return pl.BlockSpec(shape, lambda *_: (0,) * nd)


def _nt_dot(a, b):
    return lax.dot_general(a, b, (((1,), (1,)), ((), ())), preferred_element_type=F32)


def _rms_rows(x, g):
    ms = jnp.mean(x * x, axis=-1, keepdims=True)
    return x * lax.rsqrt(ms + RMS_EPS) * g


def _lane_iota(shape):
    return lax.broadcasted_iota(jnp.int32, shape, len(shape) - 1)


def _rms_half(xh, gain):
    lo = _lane_iota(xh.shape) < HEAD
    x2 = xh * xh
    s0 = jnp.sum(jnp.where(lo, x2, 0.0), axis=-1, keepdims=True)
    s1 = jnp.sum(jnp.where(lo, 0.0, x2), axis=-1, keepdims=True)
    inv = jnp.where(lo, lax.rsqrt(s0 * (1.0 / HEAD) + RMS_EPS), lax.rsqrt(s1 * (1.0 / HEAD) + RMS_EPS))
    return xh * inv * gain


def _rope_block(xh, cos, sin_signed):
    first = (_lane_iota(xh.shape) & 32) == 0
    partner = jnp.where(first, pltpu.roll(xh, LANES - 32, 1), pltpu.roll(xh, 32, 1))
    return xh * cos + partner * sin_signed


def _rope_tables(pos):
    half = HEAD // 2
    inv = ROPE_THETA ** (-jnp.arange(half, dtype=F32) / half)
    ang = pos.astype(F32)[:, None] * inv[None, :]
    cos, sin = jnp.cos(ang), jnp.sin(ang)
    return jnp.tile(cos, (1, 4)), jnp.concatenate([-sin, sin, -sin, sin], axis=1)


def _split_hi_lo(x):
    hi = x.astype(BF16)
    lo = (x - hi.astype(F32)).astype(BF16)
    return jnp.concatenate([hi, lo], axis=1)


def _group_ones2():
    r = lax.broadcasted_iota(jnp.int32, (2 * LANES, LANES), 0)
    c = lax.broadcasted_iota(jnp.int32, (2 * LANES, LANES), 1)
    return jnp.where(((r & (LANES - 1)) >> 6) == (c >> 6), 1.0, 0.0).astype(BF16)


def _group_sum(x, g2):
    return jnp.dot(_split_hi_lo(x), g2, preferred_element_type=F32)


def _group_sum_wide(x, g2):
    return jnp.concatenate(
        [_group_sum(x[:, c * LANES:(c + 1) * LANES], g2) for c in range(x.shape[1] // LANES)], axis=1)


def _sigmoid(x):
    return 1.0 / (1.0 + jnp.exp(-x))


def _head_rows(n_rows, h, n_heads):
    return pl.ds(h, n_rows, stride=n_heads)


def _in_proj_kernel(x_ref, g_ref, w_ref, qn_ref, kn_ref, cos_ref, sin_ref,
                    prw_ref, qb_ref, k_ref, v_ref, *flash_refs):
    tm = x_ref.shape[0]
    h = _rms_rows(x_ref[...], g_ref[...]).astype(BF16)
    prw_ref[...] = jnp.dot(h, w_ref[:, :RW_COLS], preferred_element_type=F32)
    pd = jnp.dot(h, w_ref[:, RW_COLS:], preferred_element_type=F32)
    cos, sin = cos_ref[...], sin_ref[...]
    qn, kn = qn_ref[...], kn_ref[...]
    for hh in range(N_DIFF_HEADS):
        sl = slice(hh * LANES, (hh + 1) * LANES)
        q = _rope_block(_rms_half(pd[:, sl], qn), cos, sin) * (HEAD ** -0.5 * LOG2E)
        k = _rope_block(_rms_half(pd[:, DIFF_WIDTH + hh * LANES:DIFF_WIDTH + (hh + 1) * LANES], kn), cos, sin)
        v = pd[:, 2 * DIFF_WIDTH + hh * LANES:2 * DIFF_WIDTH + (hh + 1) * LANES]
        qb_ref[:, sl] = q.astype(BF16)
        k_ref[_head_rows(tm, hh, N_DIFF_HEADS), :] = k
        v_ref[_head_rows(tm, hh, N_DIFF_HEADS), :] = v
        if flash_refs:
            kb_ref, vt_ref = flash_refs
            kb_ref[:, sl] = k.astype(BF16)
            vt_ref[0, sl, :] = jnp.transpose(v).astype(BF16)


def _in_proj(x, g, w_bf, qn, kn, cos, sin, tm, seq_len=None):
    n, d = x.shape
    period = cos.shape[0] // tm
    row = lambda i: (i, 0)
    outs = [jax.ShapeDtypeStruct((n, RW_COLS), F32), jax.ShapeDtypeStruct((n, DIFF_WIDTH), BF16),
            jax.ShapeDtypeStruct((n * N_DIFF_HEADS, LANES), F32), jax.ShapeDtypeStruct((n * N_DIFF_HEADS, LANES), F32)]
    out_specs = [pl.BlockSpec((tm, RW_COLS), row), pl.BlockSpec((tm, DIFF_WIDTH), row),
                 pl.BlockSpec((tm * N_DIFF_HEADS, LANES), row), pl.BlockSpec((tm * N_DIFF_HEADS, LANES), row)]
    if seq_len is not None:
        per_seq = seq_len // tm
        outs += [jax.ShapeDtypeStruct((n, DIFF_WIDTH), BF16), jax.ShapeDtypeStruct((n // seq_len, DIFF_WIDTH, seq_len), BF16)]
        out_specs += [pl.BlockSpec((tm, DIFF_WIDTH), row),
                      pl.BlockSpec((1, DIFF_WIDTH, tm), lambda i: (i // per_seq, 0, i % per_seq))]
    return pl.pallas_call(
        _in_proj_kernel,
        grid=(n // tm,),
        in_specs=[pl.BlockSpec((tm, d), row), _const_spec((1, d)), _const_spec(w_bf.shape),
                  _const_spec((1, LANES)), _const_spec((1, LANES)),
                  pl.BlockSpec((tm, LANES), lambda i: (i % period, 0)),
                  pl.BlockSpec((tm, LANES), lambda i: (i % period, 0))],
        out_specs=tuple(out_specs),
        out_shape=tuple(outs),
        compiler_params=_cparams(("parallel",)),
        name="in_proj",
    )(x, g, w_bf, qn, kn, cos, sin)


def _rwkv_token_vectors(p, shifted, prm, g2):
    mu_ref, w0_ref, w2_ref, a0_ref, a2_ref, g2_ref, kk_ref, ka_ref, rk_ref = prm
    ps = p + (shifted - p) * mu_ref[...]
    r = ps[:, :RW_WIDTH]
    k = ps[:, RW_WIDTH:2 * RW_WIDTH]
    v = ps[:, 2 * RW_WIDTH:3 * RW_WIDTH]
    wa = ps[:, 3 * RW_WIDTH:3 * RW_WIDTH + LANES]
    gd = ps[:, 3 * RW_WIDTH + LANES:]
    z = -(w0_ref[...] + jnp.dot(jnp.tanh(wa).astype(BF16), w2_ref[...], preferred_element_type=F32))
    w_log = -(jnp.maximum(z, 0.0) + jnp.log1p(jnp.exp(-jnp.abs(z)))) - 0.5
    a = _sigmoid(a0_ref[...] + jnp.dot(wa.astype(BF16), a2_ref[...], preferred_element_type=F32))
    gate = jnp.dot(_sigmoid(gd).astype(BF16), g2_ref[...], preferred_element_type=F32)
    kk = k * kk_ref[...]
    kk = kk / jnp.maximum(jnp.sqrt(_group_sum_wide(kk * kk, g2)), 1e-12)
    k2 = k * (1.0 + (a - 1.0) * ka_ref[...])
    bonus = _group_sum_wide(r * k2 * rk_ref[...], g2) * v
    return r, -jnp.exp(w_log), k2, v, -kk, kk * a, bonus, gate


def _rwkv_finish(y, bonus, gate, lw, lb, g2):
    mean = _group_sum_wide(y, g2) * (1.0 / HEAD)
    d = y - mean
    var = _group_sum_wide(d * d, g2) * (1.0 / HEAD)
    return (d * lax.rsqrt(var + LNX_EPS) * lw + lb + bonus) * gate


def _mm(x, w, terms):
    if terms == 1:
        return jnp.dot(x.astype(BF16), w.astype(BF16), preferred_element_type=F32)
    w_hi = w.astype(BF16)
    out = jnp.dot(_split_hi_lo(x), jnp.concatenate([w_hi, w_hi], axis=0), preferred_element_type=F32)
    if terms == 3:
        w_lo = (w - w_hi.astype(F32)).astype(BF16)
        out = out + jnp.dot(x.astype(BF16), w_lo, preferred_element_type=F32)
    return out


def _stack_heads(x):
    lo = _lane_iota(x.shape) < HEAD
    return jnp.concatenate([jnp.where(lo, x, 0.0), jnp.where(lo, 0.0, x)], axis=0)


def _nt_mm(x, w, terms):
    if terms == 1:
        return _nt_dot(x.astype(BF16), w.astype(BF16))
    w_hi = w.astype(BF16)
    out = _nt_dot(_split_hi_lo(x), jnp.concatenate([w_hi, w_hi], axis=1))
    if terms == 3:
        out = out + _nt_dot(x.astype(BF16), (w - w_hi.astype(F32)).astype(BF16))
    return out


def _chunk_units(units, slots, zs, terms):
    n = len(units)
    c2 = 2 * units[0][0].shape[0]
    ti = lax.broadcasted_iota(jnp.int32, (c2, c2), 0)
    tj = lax.broadcasted_iota(jnp.int32, (c2, c2), 1)
    eye = ti == tj
    strict = (ti & (c2 // 2 - 1)) > (tj & (c2 // 2 - 1))
    incl = (ti & (c2 // 2 - 1)) >= (tj & (c2 // 2 - 1))
    s_a = [_stack_heads(u[0]) for u in units]
    s_r = [_stack_heads(u[1]) for u in units]
    s_v = [_stack_heads(u[6]) for u in units]
    sc = [_nt_mm(jnp.concatenate([s_a[i], s_r[i]], axis=0),
                 jnp.concatenate([_stack_heads(units[i][2]), _stack_heads(units[i][3])], axis=0), terms)
          for i in range(n)]
    x = [jnp.where(strict, s[:c2, :c2], 0.0) for s in sc]
    l_ak = [jnp.where(strict, s[:c2, c2:], 0.0) for s in sc]
    m_rbk = [jnp.concatenate([jnp.where(incl, s[c2:, :c2], 0.0), jnp.where(incl, s[c2:, c2:], 0.0)], axis=1) for s in sc]
    t_inv = [jnp.where(eye, 1.0, 0.0) + xi for xi in x]
    lv = [_mm(l_ak[i], s_v[i], terms) for i in range(n)]
    for _ in range(max(1, (c2 // 2 - 1).bit_length()) - 1):
        x = [_mm(xi, xi, terms) for xi in x]
        t_inv = [t_inv[i] + _mm(t_inv[i], x[i], terms) for i in range(n)]
    pq = [_mm(t_inv[i], jnp.concatenate([s_a[i], lv[i]], axis=1), terms) for i in range(n)]
    pqv = [jnp.concatenate([pq[i], jnp.concatenate([jnp.zeros_like(s_v[i]), s_v[i]], axis=1)], axis=0) for i in range(n)]
    ah = [_mm(jnp.transpose(jnp.concatenate([_stack_heads(units[i][4]), _stack_heads(units[i][5])], axis=0)), pqv[i], terms)
          for i in range(n)]
    ry = [_mm(m_rbk[i], pqv[i], terms) for i in range(n)]
    zs, ys = list(zs), []
    for i in range(n):
        z = zs[slots[i]]
        y = _mm(s_r[i] + ry[i][:, :c2], z, 3) + ry[i][:, c2:]
        ys.append(y[:c2 // 2] + y[c2 // 2:])
        zs[slots[i]] = _mm(jnp.where(eye, units[i][7], 0.0) + ah[i][:, :c2], z, 3) + ah[i][:, c2:]
    return ys, zs


def _rwkv_kernel(p_ref, prev_ref, s0_ref, mu_ref, w0_ref, w2_ref, a0_ref, a2_ref, g2_ref,
                 kk_ref, ka_ref, rk_ref, lw_ref, lb_ref,
                 out_ref, st_ref,
                 state, carry, r_s, w_s, k_s, v_s, a_s, b_s, y_s, bonus_s, gate_s, *, ns, tt):
    ti = pl.program_id(1)
    npair = RW_WIDTH // LANES
    g2 = _group_ones2()

    @pl.when(ti == 0)
    def _():
        for s in range(ns):
            for j in range(npair):
                state[(s * npair + j) * HEAD:(s * npair + j + 1) * HEAD, :] = jnp.concatenate(
                    [s0_ref[s, 2 * j], s0_ref[s, 2 * j + 1]], axis=1)
        carry[...] = prev_ref[...]

    row0 = lax.broadcasted_iota(jnp.int32, (tt, RW_COLS), 0) == 0
    prm = (mu_ref, w0_ref, w2_ref, a0_ref, a2_ref, g2_ref, kk_ref, ka_ref, rk_ref)
    for s in range(ns):
        p = p_ref[s]
        shifted = jnp.where(row0, carry[s], pltpu.roll(p, 1, 0))
        carry[s] = p[tt - 1:tt, :]
        r_s[s], log_w, k_s[s], v_s[s], a_s[s], b_s[s], bonus_s[s], gate_s[s] = _rwkv_token_vectors(p, shifted, prm, g2)
        w_s[s] = jnp.exp(log_w)

    tiles = [(s, j) for s in range(ns) for j in range(npair)]
    nt = len(tiles)
    sub = lax.broadcasted_iota(jnp.int32, (nt * HEAD, LANES), 0)
    e1 = jnp.where((_lane_iota((nt * HEAD, LANES)) & (HEAD - 1)) == (sub & (HEAD - 1)), 1.0, 0.0)
    seqs = dict(r=r_s, w=w_s, k=k_s, v=v_s, a=a_s, b=b_s)

    def block(tb, c):
        base = pl.multiple_of(tb * 8, 8)
        rows8 = {q: [ref[s, pl.ds(base, 8), j * LANES:(j + 1) * LANES] for (s, j) in tiles] for q, ref in seqs.items()}
        ys = []
        for i in range(8):
            bc = lambda q: jnp.concatenate(
                [jnp.broadcast_to(rows8[q][n][i:i + 1], (HEAD, LANES)) for n in range(nt)], axis=0)
            st = state[...]
            red = jnp.dot(jnp.concatenate([_split_hi_lo(st * bc("a")), _split_hi_lo(e1 * bc("v"))], axis=0), g2,
                          preferred_element_type=F32)
            st = st * bc("w") + red[:nt * HEAD] * bc("b") + red[nt * HEAD:] * bc("k")
            state[...] = st
            yred = jnp.dot(_split_hi_lo(st * bc("r")), g2, preferred_element_type=F32)
            ys.append(jnp.sum((e1 * yred).reshape(nt, HEAD, LANES), axis=1))
        for n, (s, j) in enumerate(tiles):
            y_s[s, pl.ds(base, 8), j * LANES:(j + 1) * LANES] = jnp.concatenate([y[n:n + 1] for y in ys], axis=0)
        return c

    lax.fori_loop(0, tt // 8, block, 0)

    for s in range(ns):
        out_ref[s] = _rwkv_finish(y_s[s], bonus_s[s], gate_s[s], lw_ref[...], lb_ref[...], g2)

    @pl.when(ti == pl.num_programs(1) - 1)
    def _():
        for s in range(ns):
            for j in range(npair):
                tile = state[(s * npair + j) * HEAD:(s * npair + j + 1) * HEAD, :]
                st_ref[s, 2 * j] = tile[:, :HEAD]
                st_ref[s, 2 * j + 1] = tile[:, HEAD:]


def _rwkv_chunk_kernel(p_ref, prev_ref, z0_ref, mu_ref, w0_ref, w2_ref, a0_ref, a2_ref, g2_ref,
                       kk_ref, ka_ref, rk_ref, lw_ref, lb_ref,
                       out_ref, zt_ref,
                       zstate, carry, r_s, g_s, k_s, v_s, a_s, b_s, y_s, bonus_s, gate_s, *, tt, chunk, cpt, terms):
    ti = pl.program_id(1)
    npair = RW_WIDTH // LANES
    g2 = _group_ones2()

    @pl.when(ti == 0)
    def _():
        zstate[...] = z0_ref[0]
        carry[...] = prev_ref[0]

    p = p_ref[0]
    row0 = lax.broadcasted_iota(jnp.int32, (tt, RW_COLS), 0) == 0
    shifted = jnp.where(row0, carry[...], pltpu.roll(p, 1, 0))
    carry[...] = p[tt - 1:tt, :]
    prm = (mu_ref, w0_ref, w2_ref, a0_ref, a2_ref, g2_ref, kk_ref, ka_ref, rk_ref)
    (r_s[...], g_s[...], k_s[...], v_s[...], a_s[...], b_s[...], bonus_s[...], gate_s[...]) = _rwkv_token_vectors(
        p, shifted, prm, g2)

    ci = lax.broadcasted_iota(jnp.int32, (chunk, 3 * chunk), 0)
    cj = lax.broadcasted_iota(jnp.int32, (chunk, 3 * chunk), 1)
    cj = jnp.where(cj >= 2 * chunk, cj - 2 * chunk, jnp.where(cj >= chunk, cj - chunk, cj))
    ltri3 = jnp.where(cj <= ci, 1.0, 0.0).astype(BF16)

    lanes = [slice(j * LANES, (j + 1) * LANES) for j in range(npair)]

    def scaled(c):
        rows = pl.ds(pl.multiple_of(c * chunk, chunk), chunk)
        g = g_s[rows, :]
        g_hi = g.astype(BF16)
        g_r = g - g_hi.astype(F32)
        g_mid = g_r.astype(BF16)
        g_lo = (g_r - g_mid.astype(F32)).astype(BF16)
        cum = jnp.dot(ltri3, jnp.concatenate([g_hi, g_mid, g_lo], axis=0), preferred_element_type=F32)
        total = cum[chunk - 1:chunk, :]
        e_in, e_ex, e_inv, e_rest = jnp.exp(cum), jnp.exp(cum - g), jnp.exp(-cum), jnp.exp(total - cum)
        a_t, r_t = a_s[rows, :] * e_ex, r_s[rows, :] * e_in
        b, k, v = b_s[rows, :], k_s[rows, :], v_s[rows, :]
        ops = (a_t, r_t, b * e_inv, k * e_inv, b * e_rest, k * e_rest, v, jnp.exp(total))
        return rows, [tuple(t[:, sl] for t in ops) for sl in lanes]

    def body(trip, carry_):
        parts = [scaled(trip * cpt + c) for c in range(cpt)]
        ys, z_new = _chunk_units([u for _, units in parts for u in units], list(range(npair)) * cpt,
                                 [zstate[sl, :] for sl in lanes], terms)
        for c, (rows, _) in enumerate(parts):
            for j, sl in enumerate(lanes):
                y_s[rows, sl] = ys[c * npair + j]
        for j, sl in enumerate(lanes):
            zstate[sl, :] = z_new[j]
        return carry_

    lax.fori_loop(0, tt // (cpt * chunk), body, 0)
    out_ref[0] = _rwkv_finish(y_s[...], bonus_s[...], gate_s[...], lw_ref[...], lb_ref[...], g2)

    @pl.when(ti == pl.num_programs(1) - 1)
    def _():
        zt_ref[0] = zstate[...]


def _rwkv(prw, prev, s0, rw, ns, tt):
    S, T, _ = prw.shape
    npair = RW_WIDTH // LANES
    vec = lambda n: _const_spec((1, n))
    blk = lambda w: pl.BlockSpec((ns, tt, w), lambda i, t: (i, t, 0))
    st_spec = pl.BlockSpec((ns, 2 * npair, HEAD, HEAD), lambda i, t: (i, 0, 0, 0))
    seq = lambda: pltpu.VMEM((ns, tt, RW_WIDTH), F32)
    return pl.pallas_call(
        functools.partial(_rwkv_kernel, ns=ns, tt=tt),
        grid=(S // ns, T // tt),
        in_specs=[blk(RW_COLS), pl.BlockSpec((ns, 1, RW_COLS), lambda i, t: (i, 0, 0)), st_spec,
                  vec(RW_COLS), vec(RW_WIDTH), _const_spec((LANES, RW_WIDTH)), vec(RW_WIDTH),
                  _const_spec((LANES, RW_WIDTH)), _const_spec((LANES, RW_WIDTH)),
                  vec(RW_WIDTH), vec(RW_WIDTH), vec(RW_WIDTH), vec(RW_WIDTH), vec(RW_WIDTH)],
        out_specs=(blk(RW_WIDTH), st_spec),
        out_shape=(jax.ShapeDtypeStruct((S, T, RW_WIDTH), F32),
                   jax.ShapeDtypeStruct((S, 2 * npair, HEAD, HEAD), F32)),
        scratch_shapes=[pltpu.VMEM((ns * npair * HEAD, LANES), F32), pltpu.VMEM((ns, 1, RW_COLS), F32)]
                       + [seq() for _ in range(9)],
        compiler_params=_cparams(("parallel", "arbitrary")),
        name="rwkv",
    )(prw, prev, s0, *rw)


def _rwkv_chunked(prw, prev, z0, rw, tt, chunk, terms):
    S, T, _ = prw.shape
    vec = lambda n: _const_spec((1, n))
    blk = lambda w: pl.BlockSpec((1, tt, w), lambda i, t: (i, t, 0))
    z_spec = pl.BlockSpec((1, RW_WIDTH, LANES), lambda i, t: (i, 0, 0))
    seq = lambda: pltpu.VMEM((tt, RW_WIDTH), F32)
    return pl.pallas_call(
        functools.partial(_rwkv_chunk_kernel, tt=tt, chunk=chunk, cpt=_tile(tt // chunk, RWKV_CHUNKS_PER_TRIP),
                          terms=terms),
        grid=(S, T // tt),
        in_specs=[blk(RW_COLS), pl.BlockSpec((1, 1, RW_COLS), lambda i, t: (i, 0, 0)), z_spec,
                  vec(RW_COLS), vec(RW_WIDTH), _const_spec((LANES, RW_WIDTH)), vec(RW_WIDTH),
                  _const_spec((LANES, RW_WIDTH)), _const_spec((LANES, RW_WIDTH)),
                  vec(RW_WIDTH), vec(RW_WIDTH), vec(RW_WIDTH), vec(RW_WIDTH), vec(RW_WIDTH)],
        out_specs=(blk(RW_WIDTH), z_spec),
        out_shape=(jax.ShapeDtypeStruct((S, T, RW_WIDTH), F32), jax.ShapeDtypeStruct((S, RW_WIDTH, LANES), F32)),
        scratch_shapes=[pltpu.VMEM((RW_WIDTH, LANES), F32), pltpu.VMEM((1, RW_COLS), F32)] + [seq() for _ in range(9)],
        compiler_params=_cparams(("parallel", "arbitrary")),
        name="rwkv_chunked",
    )(prw, prev, z0, *rw)


def _state_from_blockdiag(z, S):
    z = z.reshape(S, 4, 2, HEAD, 2, HEAD)
    z = jnp.stack([z[:, :, 0, :, 0, :], z[:, :, 1, :, 1, :]], axis=2)
    return z.reshape(S, 8, HEAD, HEAD).transpose(0, 1, 3, 2)


def _lambda(lamp_ref, lam_init):
    lp = lamp_ref[...]
    s1 = jnp.sum(lp[0:1] * lp[1:2], axis=-1, keepdims=True)
    s2 = jnp.sum(lp[2:3] * lp[3:4], axis=-1, keepdims=True)
    return jnp.exp(s1) - jnp.exp(s2) + lam_init


def _stack_halves(q):
    lo = _lane_iota(q.shape) < HEAD
    zero = jnp.zeros_like(q)
    return jnp.concatenate([jnp.where(lo, q, zero), jnp.where(lo, zero, q)], axis=0)


def _diff_finish(acc, l, lam, sn, lam_init, rows):
    o = acc[:rows] / l[:rows] - lam * (acc[rows:] / l[rows:])
    return _rms_rows(o, sn) * (1.0 - lam_init)


def _diff_prompt_kernel(q_ref, k_ref, vt_ref, lamp_ref, sn_ref, o_ref, m_ref, acc_ref, *, tq, hp, lam_init):
    i = pl.program_id(2)
    heads = [slice(h * LANES, (h + 1) * LANES) for h in range(hp)]
    q2 = [_stack_halves(q_ref[0, :, sl]) for sl in heads]
    start = pl.multiple_of(i * tq, tq)
    s = [_nt_dot(k_ref[0, pl.ds(start, tq), sl], q2[h]) for h, sl in enumerate(heads)]
    key = lax.broadcasted_iota(jnp.int32, s[0].shape, 0)
    qry = lax.broadcasted_iota(jnp.int32, s[0].shape, 1)
    causal = key <= jnp.where(qry >= tq, qry - tq, qry)
    s = [jnp.where(causal, sh, NEG_BIG) for sh in s]
    m = [jnp.max(sh, axis=0, keepdims=True) for sh in s]
    p = [jnp.exp2(s[h] - m[h]) for h in range(hp)]
    ones = jnp.ones((ONES_ROWS, tq), BF16)

    def values(off, sl):
        return jnp.concatenate([vt_ref[0, sl, pl.ds(off, tq)], ones], axis=0)

    for h, sl in enumerate(heads):
        m_ref[h] = m[h]
        acc_ref[h] = jnp.dot(values(start, sl), p[h].astype(BF16), preferred_element_type=F32)

    def body(j, c):
        off = pl.multiple_of(j * tq, tq)
        s = [_nt_dot(k_ref[0, pl.ds(off, tq), sl], q2[h]) for h, sl in enumerate(heads)]
        m_old = [m_ref[h] for h in range(hp)]
        m_new = [jnp.maximum(m_old[h], jnp.max(s[h], axis=0, keepdims=True)) for h in range(hp)]
        alpha = [jnp.exp2(m_old[h] - m_new[h]) for h in range(hp)]
        p = [jnp.exp2(s[h] - m_new[h]) for h in range(hp)]
        pv = [jnp.dot(values(off, sl), p[h].astype(BF16), preferred_element_type=F32) for h, sl in enumerate(heads)]
        for h in range(hp):
            acc_ref[h] = alpha[h] * acc_ref[h] + pv[h]
            m_ref[h] = m_new[h]
        return c

    lax.fori_loop(0, i, body, 0)
    lam = _lambda(lamp_ref, lam_init)
    for h, sl in enumerate(heads):
        acc = acc_ref[h]
        acc = acc[:LANES] / acc[LANES:LANES + 1]
        o_t = acc[:, :tq] - lam * acc[:, tq:]
        ms = jnp.mean(o_t * o_t, axis=0, keepdims=True)
        o_t = o_t * lax.rsqrt(ms + RMS_EPS)
        o_ref[0, :, sl] = jnp.transpose(o_t) * sn_ref[...] * (1.0 - lam_init)


def _diff_prompt(qb, kb, vt, lamp, sn, tq, hp, lam_init):
    B, T, _ = qb.shape
    tile = pl.BlockSpec((1, tq, hp * LANES), lambda b, h, i: (b, i, h))
    return pl.pallas_call(
        functools.partial(_diff_prompt_kernel, tq=tq, hp=hp, lam_init=lam_init),
        grid=(B, N_DIFF_HEADS // hp, T // tq),
        in_specs=[tile, pl.BlockSpec((1, T, hp * LANES), lambda b, h, i: (b, 0, h)),
                  pl.BlockSpec((1, hp * LANES, T), lambda b, h, i: (b, h, 0)),
                  _const_spec((4, HEAD)), _const_spec((1, LANES))],
        out_specs=tile,
        out_shape=jax.ShapeDtypeStruct((B, T, DIFF_WIDTH), F32),
        scratch_shapes=[pltpu.VMEM((hp, 1, 2 * tq), F32), pltpu.VMEM((hp, LANES + ONES_ROWS, 2 * tq), F32)],
        compiler_params=_cparams(("parallel", "parallel", "arbitrary")),
        name="diff_prompt",
    )(qb, kb, vt, lamp, sn)


def _diff_sample_kernel(pt_ref, q_ref, kn_ref, vn_ref, lamp_ref, sn_ref, *rest, pp, s_new, lam_init):
    del pt_ref
    kpages, vpages = rest[:pp], rest[pp:2 * pp]
    o_ref, m_ref, l_ref, acc_ref = rest[2 * pp:]
    j = pl.program_id(1)
    nh = N_DIFF_HEADS
    rows = 2 * s_new
    q = q_ref[0]
    q2 = [_stack_halves(q[:, h * LANES:(h + 1) * LANES]) for h in range(nh)]

    def update(scores, values):
        s = jnp.concatenate([jnp.concatenate(sc, axis=1) if len(sc) > 1 else sc[0] for sc in scores], axis=0)
        m_old = m_ref[...]
        m_new = jnp.maximum(m_old, jnp.max(s, axis=-1, keepdims=True))
        alpha = jnp.exp2(m_old - m_new)
        p = jnp.exp2(s - m_new)
        pv = []
        for h in range(nh):
            ph = p[h * rows:(h + 1) * rows].astype(BF16)
            acc = jnp.dot(ph[:, :PAGE], values[h][0], preferred_element_type=F32)
            for c in range(1, len(values[h])):
                acc = acc + jnp.dot(ph[:, c * PAGE:(c + 1) * PAGE], values[h][c], preferred_element_type=F32)
            pv.append(acc)
        l_ref[...] = alpha * l_ref[...] + jnp.sum(p, axis=-1, keepdims=True)
        acc_ref[...] = alpha * acc_ref[...] + jnp.concatenate(pv, axis=0)
        m_ref[...] = m_new

    @pl.when(j == 0)
    def _():
        m_ref[...] = jnp.full(m_ref.shape, NEG_BIG, F32)
        l_ref[...] = jnp.zeros(l_ref.shape, F32)
        acc_ref[...] = jnp.zeros(acc_ref.shape, F32)
        pad = jnp.zeros((PAGE - s_new, LANES), F32)
        row = lax.broadcasted_iota(jnp.int32, (rows, PAGE), 0)
        col = lax.broadcasted_iota(jnp.int32, (rows, PAGE), 1)
        mask = col <= jnp.where(row >= s_new, row - s_new, row)
        new_rows = lambda ref, h: jnp.concatenate([ref[0, _head_rows(s_new, h, nh), :], pad], axis=0).astype(BF16)
        update([[jnp.where(mask, _nt_dot(q2[h], new_rows(kn_ref, h)), NEG_BIG)] for h in range(nh)],
               [[new_rows(vn_ref, h)] for h in range(nh)])

    page_rows = lambda ref, h: ref[_head_rows(PAGE, h, nh), :].astype(BF16)
    update([[_nt_dot(q2[h], page_rows(kp, h)) for kp in kpages] for h in range(nh)],
           [[page_rows(vp, h) for vp in vpages] for h in range(nh)])

    @pl.when(j == pl.num_programs(1) - 1)
    def _():
        lam = _lambda(lamp_ref, lam_init)
        acc, l = acc_ref[...], l_ref[...]
        for h in range(nh):
            o_ref[0, :, h * LANES:(h + 1) * LANES] = _diff_finish(
                acc[h * rows:(h + 1) * rows], l[h * rows:(h + 1) * rows], lam, sn_ref[...], lam_init, s_new)


def _diff_sample(page_table, qb, k_new, v_new, pool_k, pool_v, lamp, sn, pp, lam_init):
    Bs, S, _ = qb.shape
    n_pages = page_table.shape[1]
    nh = N_DIFF_HEADS
    qspec = pl.BlockSpec((1, S, DIFF_WIDTH), lambda b, j, pt: (b, 0, 0))
    new = pl.BlockSpec((1, S * nh, LANES), lambda b, j, pt: (b, 0, 0))
    page = lambda c: pl.BlockSpec((None, PAGE * nh, LANES), lambda b, j, pt, c=c: (pt[b, j * pp + c], 0, 0))
    cst = lambda shape: pl.BlockSpec(shape, lambda b, j, pt: (0,) * len(shape))
    return pl.pallas_call(
        functools.partial(_diff_sample_kernel, pp=pp, s_new=S, lam_init=lam_init),
        grid_spec=pltpu.PrefetchScalarGridSpec(
            num_scalar_prefetch=1,
            grid=(Bs, n_pages // pp),
            in_specs=[qspec, new, new, cst((4, HEAD)), cst((1, LANES))] + [page(c) for c in range(pp)] * 2,
            out_specs=qspec,
            scratch_shapes=[pltpu.VMEM((nh * 2 * S, 1), F32), pltpu.VMEM((nh * 2 * S, 1), F32),
                            pltpu.VMEM((nh * 2 * S, LANES), F32)]),
        out_shape=jax.ShapeDtypeStruct((Bs, S, DIFF_WIDTH), F32),
        compiler_params=_cparams(("parallel", "arbitrary")),
        name="diff_sample",
    )(page_table, qb, k_new, v_new, lamp, sn, *([pool_k] * pp), *([pool_v] * pp))


def _mem_kv_kernel(x_ref, g_ref, w_ref, kn_ref, k_ref, v_ref):
    h = _rms_rows(x_ref[...], g_ref[...]).astype(BF16)
    kv = jnp.dot(h, w_ref[...], preferred_element_type=F32)
    tm = x_ref.shape[0]
    for hh in range(N_MEM_HEADS):
        sl = slice(hh * LANES, (hh + 1) * LANES)
        k_ref[_head_rows(tm, hh, N_MEM_HEADS), :] = _rms_rows(kv[:, sl], kn_ref[...])
        v_ref[_head_rows(tm, hh, N_MEM_HEADS), :] = kv[:, MEM_WIDTH + hh * LANES:MEM_WIDTH + (hh + 1) * LANES]


def _mem_kv(mem, g, w_bf, kn, tm):
    n, d = mem.shape
    row = lambda i: (i, 0)
    return pl.pallas_call(
        _mem_kv_kernel,
        grid=(n // tm,),
        in_specs=[pl.BlockSpec((tm, d), row), _const_spec((1, d)), _const_spec(w_bf.shape), _const_spec((1, LANES))],
        out_specs=(pl.BlockSpec((tm * N_MEM_HEADS, LANES), row),) * 2,
        out_shape=(jax.ShapeDtypeStruct((n * N_MEM_HEADS, LANES), F32),) * 2,
        compiler_params=_cparams(("parallel",)),
        name="mem_kv",
    )(mem, g, w_bf, kn)


def _mem_attn_kernel(x_ref, r_ref, a_ref, wmix_ref, mk_ref, mv_ref, g_ref, wq_ref, qn_ref, wo_ref, y_ref, o_scr,
                     *, nb, tr, n_mem):
    d = x_ref.shape[-1]
    rows = nb * tr
    x = x_ref[...].reshape(rows, d)
    x = x + jnp.dot(r_ref[...].reshape(rows, RW_WIDTH).astype(BF16), wmix_ref[:RW_WIDTH, :], preferred_element_type=F32)
    x = x + jnp.dot(a_ref[...].reshape(rows, DIFF_WIDTH).astype(BF16), wmix_ref[RW_WIDTH:, :], preferred_element_type=F32)
    h = _rms_rows(x, g_ref[...]).astype(BF16)
    q = jnp.dot(h, wq_ref[...], preferred_element_type=F32)
    heads = [slice(hh * LANES, (hh + 1) * LANES) for hh in range(N_MEM_HEADS)]
    qh = [(_rms_rows(q[:, sl], qn_ref[...]) * (LANES ** -0.5 * LOG2E)).astype(BF16) for sl in heads]
    units = [(hh, n) for hh in range(N_MEM_HEADS) for n in range(nb)]
    mem_rows = lambda ref, hh, n: ref[n, _head_rows(n_mem, hh, N_MEM_HEADS), :].astype(BF16)
    s = [_nt_dot(qh[hh][n * tr:(n + 1) * tr], mem_rows(mk_ref, hh, n)) for hh, n in units]
    p = [jnp.exp2(su - jnp.max(su, axis=-1, keepdims=True)) for su in s]
    o = [jnp.dot(p[u].astype(BF16), mem_rows(mv_ref, hh, n), preferred_element_type=F32) for u, (hh, n) in enumerate(units)]
    for u, (hh, n) in enumerate(units):
        o_scr[n * tr:(n + 1) * tr, heads[hh]] = o[u] / jnp.sum(p[u], axis=-1, keepdims=True)
    y = x + jnp.dot(o_scr[...].astype(BF16), wo_ref[...], preferred_element_type=F32)
    y_ref[...] = y.reshape(nb, tr, d)


def _mem_attn(x, r_out, attn, wmix_bf, mk, mv, g, wq_bf, qn, wo_bf, nb, tr):
    G, R, d = x.shape
    M4 = mk.shape[1]
    tile = lambda w: pl.BlockSpec((nb, tr, w), lambda i, t: (i, t, 0))
    mem = pl.BlockSpec((nb, M4, LANES), lambda i, t: (i, 0, 0))
    return pl.pallas_call(
        functools.partial(_mem_attn_kernel, nb=nb, tr=tr, n_mem=M4 // N_MEM_HEADS),
        grid=(G // nb, R // tr),
        in_specs=[tile(d), tile(RW_WIDTH), tile(DIFF_WIDTH), _const_spec(wmix_bf.shape), mem, mem,
                  _const_spec((1, d)), _const_spec(wq_bf.shape), _const_spec((1, LANES)), _const_spec(wo_bf.shape)],
        out_specs=tile(d),
        out_shape=jax.ShapeDtypeStruct((G, R, d), F32),
        scratch_shapes=[pltpu.VMEM((nb * tr, MEM_WIDTH), F32)],
        compiler_params=_cparams(("parallel", "arbitrary")),
        name="mem_attn",
    )(x, r_out, attn, wmix_bf, mk, mv, g, wq_bf, qn, wo_bf)


def _ffn_kernel(x_ref, g_ref, wgu_ref, wd_ref, y_ref, *, d_ff, fc):
    x = x_ref[...]
    h = _rms_rows(x, g_ref[...]).astype(BF16)
    y = x
    for c in range(d_ff // fc):
        gt = jnp.dot(h, wgu_ref[:, c * fc:(c + 1) * fc], preferred_element_type=F32)
        up = jnp.dot(h, wgu_ref[:, d_ff + c * fc:d_ff + (c + 1) * fc], preferred_element_type=F32)
        act = (gt * _sigmoid(gt) * up).astype(BF16)
        y = y + jnp.dot(act, wd_ref[c * fc:(c + 1) * fc, :], preferred_element_type=F32)
    y_ref[...] = y


def _ffn(x, g, wgu_bf, wd_bf, tm):
    n, d = x.shape
    d_ff = wd_bf.shape[0]
    fc = d_ff // 2 if (d_ff // 2) % LANES == 0 else d_ff
    row = lambda i: (i, 0)
    return pl.pallas_call(
        functools.partial(_ffn_kernel, d_ff=d_ff, fc=fc),
        grid=(n // tm,),
        in_specs=[pl.BlockSpec((tm, d), row), _const_spec((1, d)), _const_spec(wgu_bf.shape), _const_spec(wd_bf.shape)],
        out_specs=pl.BlockSpec((tm, d), row),
        out_shape=jax.ShapeDtypeStruct((n, d), F32),
        compiler_params=_cparams(("parallel",)),
        name="ffn",
    )(x, g, wgu_bf, wd_bf)


def _tile(n, pref):
    t = min(n, pref)
    while n % t:
        t //= 2
    return t


def kernel(x_prompt, x_sample, cache_diff_k, cache_diff_v, cache_mem_k, cache_mem_v, state_rwkv, state_shift, page_table, mem_prompt, norm_mix, w_in, rwkv_mu, rwkv_w0, rwkv_w2, rwkv_a0, rwkv_a2, rwkv_g2, rwkv_k_k, rwkv_k_a, rwkv_r_k, rwkv_lnx_w, rwkv_lnx_b, diff_q_norm, diff_k_norm, diff_lam_q1, diff_lam_k1, diff_lam_q2, diff_lam_k2, diff_sub_norm, w_o, norm_mem, norm_memkv, w_mq, w_mkv, mem_q_norm, mem_k_norm, w_mo, norm_ffn, w_gate_up, w_down):
    B, T, D = x_prompt.shape
    Bs, S, _ = x_sample.shape
    depth = w_in.shape[0]
    n_pages = page_table.shape[1]
    past = n_pages * PAGE
    M = mem_prompt.shape[1]
    xp = x_prompt.reshape(B * T, D)
    xs = x_sample.reshape(Bs * S, D)
    mem = mem_prompt.reshape(B * M, D)

    tm_p = _tile(B * T, 256)
    tm_p = _tile(T, tm_p)
    tm_s = _tile(Bs * S, 256)
    cos_p, sin_p = _rope_tables(jnp.arange(T, dtype=jnp.int32))
    cos_s, sin_s = _rope_tables(past + jnp.arange(S, dtype=jnp.int32))
    cos_s, sin_s = jnp.tile(cos_s, (tm_s // S, 1)), jnp.tile(sin_s, (tm_s // S, 1))
    tq = _tile(T, 512)
    pp = _tile(n_pages, 32)
    ns_s = _tile(Bs, 4)
    nb_mem = _tile(Bs, 16)
    tt_p = _tile(T, 256)

    outs = [[] for _ in range(10)]
    for l in range(depth):
        lam_init = 0.8 - 0.6 * math.exp(-0.3 * l)
        row = lambda a: a[l].reshape(1, -1)
        bf = lambda a: a[l].astype(BF16)
        zpad = jnp.zeros((HEAD, RW_WIDTH), F32)
        rw = (row(rwkv_mu), row(rwkv_w0), jnp.concatenate([rwkv_w2[l], zpad], axis=0).astype(BF16), row(rwkv_a0),
              jnp.concatenate([zpad, rwkv_a2[l]], axis=0).astype(BF16), bf(rwkv_g2), row(rwkv_k_k), row(rwkv_k_a),
              row(rwkv_r_k), row(rwkv_lnx_w), row(rwkv_lnx_b))
        lamp = jnp.stack([diff_lam_q1[l], diff_lam_k1[l], diff_lam_q2[l], diff_lam_k2[l]])
        w_in_bf, w_o_bf, w_mq_bf, w_mkv_bf, w_mo_bf = bf(w_in), bf(w_o), bf(w_mq), bf(w_mkv), bf(w_mo)
        w_gu_bf, w_d_bf = bf(w_gate_up), bf(w_down)
        qn, kn, sn = row(diff_q_norm), row(diff_k_norm), row(diff_sub_norm)

        prw, qb, k, v, kb, vt = _in_proj(xp, row(norm_mix), w_in_bf, qn, kn, cos_p, sin_p, tm_p, seq_len=T)
        prw3 = prw.reshape(B, T, RW_COLS)
        r_out, z_p = _rwkv_chunked(prw3, jnp.zeros((B, 1, RW_COLS), F32), jnp.zeros((B, RW_WIDTH, LANES), F32), rw,
                                   tt_p, _tile(tt_p, RWKV_CHUNK), RWKV_TERMS)
        o = _diff_prompt(qb.reshape(B, T, -1), kb.reshape(B, T, -1), vt, lamp, sn, tq, 2, lam_init)
        mk, mv = _mem_kv(mem, row(norm_memkv), w_mkv_bf, row(mem_k_norm), _tile(M, 256))
        x2 = _mem_attn(xp.reshape(B, T, D), r_out, o, w_o_bf,
                       mk.reshape(B, M * N_MEM_HEADS, LANES), mv.reshape(B, M * N_MEM_HEADS, LANES),
                       row(norm_mem), w_mq_bf, row(mem_q_norm), w_mo_bf, 1, tm_p)
        xp = _ffn(x2.reshape(B * T, D), row(norm_ffn), w_gu_bf, w_d_bf, tm_p)
        outs[0].append(k.reshape(B, T, N_DIFF_HEADS, LANES))
        outs[1].append(v.reshape(B, T, N_DIFF_HEADS, LANES))
        outs[2].append(mk.reshape(B, M, N_MEM_HEADS, LANES))
        outs[3].append(mv.reshape(B, M, N_MEM_HEADS, LANES))
        outs[4].append(_state_from_blockdiag(z_p, B))
        outs[5].append(prw3[:, -1])

        prw, qb, k, v = _in_proj(xs, row(norm_mix), w_in_bf, qn, kn, cos_s, sin_s, tm_s)
        prw3 = prw.reshape(Bs, S, RW_COLS)
        r_out, st_s = _rwkv(prw3, state_shift[l][:, None, :], state_rwkv[l], rw, ns_s, S)
        o = _diff_sample(page_table, qb.reshape(Bs, S, -1), k.reshape(Bs, S * N_DIFF_HEADS, LANES),
                         v.reshape(Bs, S * N_DIFF_HEADS, LANES),
                         cache_diff_k[l].reshape(-1, PAGE * N_DIFF_HEADS, LANES),
                         cache_diff_v[l].reshape(-1, PAGE * N_DIFF_HEADS, LANES), lamp, sn, pp, lam_init)
        x2 = _mem_attn(xs.reshape(Bs, S, D), r_out, o, w_o_bf, cache_mem_k[l].reshape(Bs, M * N_MEM_HEADS, LANES),
                       cache_mem_v[l].reshape(Bs, M * N_MEM_HEADS, LANES),
                       row(norm_mem), w_mq_bf, row(mem_q_norm), w_mo_bf, nb_mem, S)
        xs = _ffn(x2.reshape(Bs * S, D), row(norm_ffn), w_gu_bf, w_d_bf, tm_s)
        outs[6].append(k.reshape(Bs, S, N_DIFF_HEADS, LANES))
        outs[7].append(v.reshape(Bs, S, N_DIFF_HEADS, LANES))
        outs[8].append(st_s)
        outs[9].append(prw3[:, -1])

    st = [jnp.stack(o) for o in outs]
    return (xp.reshape(B, T, D), xs.reshape(Bs, S, D), st[0], st[1], st[2], st[3], st[4], st[5],
            st[6], st[7], st[8], st[9])
```

```python
import functools
import math

import jax
import jax.numpy as jnp
from jax import lax
from jax.experimental import pallas as pl
from jax.experimental.pallas import tpu as pltpu

F32 = jnp.float32
BF16 = jnp.bfloat16

LANES = 128
HEAD = 64
RW_WIDTH = 512
RW_COLS = 3 * RW_WIDTH + 64 + 64 + 128
DIFF_WIDTH = 512
N_DIFF_HEADS = 4
MEM_WIDTH = 512
N_MEM_HEADS = 4
PAGE = 128
ROPE_THETA = 10000.0
RMS_EPS = 1e-6
LNX_EPS = 1e-5 * HEAD
NEG_BIG = -1e30
VMEM_LIMIT = 56 * 1024 * 1024
RWKV_CHUNK = 64
RWKV_TERMS = 1
RWKV_CHUNKS_PER_TRIP = 4
ONES_ROWS = 16
LOG2E = 1.4426950408889634


def _cparams(sem):
    return pltpu.CompilerParams(dimension_semantics=sem, vmem_limit_bytes=VMEM_LIMIT)


def _const_spec(shape):
    nd = len(shape)
    return pl.BlockSpec(shape, lambda *_: (0,) * nd)


def _nt_dot(a, b):
    return lax.dot_general(a, b, (((1,), (1,)), ((), ())), preferred_element_type=F32)


def _rms_rows(x, g):
    ms = jnp.mean(x * x, axis=-1, keepdims=True)
    return x * lax.rsqrt(ms + RMS_EPS) * g


def _lane_iota(shape):
    return lax.broadcasted_iota(jnp.int32, shape, len(shape) - 1)


def _rms_half(xh, gain):
    lo = _lane_iota(xh.shape) < HEAD
    x2 = xh * xh
    s0 = jnp.sum(jnp.where(lo, x2, 0.0), axis=-1, keepdims=True)
    s1 = jnp.sum(jnp.where(lo, 0.0, x2), axis=-1, keepdims=True)
    inv = jnp.where(lo, lax.rsqrt(s0 * (1.0 / HEAD) + RMS_EPS), lax.rsqrt(s1 * (1.0 / HEAD) + RMS_EPS))
    return xh * inv * gain


def _rope_block(xh, cos, sin_signed):
    first = (_lane_iota(xh.shape) & 32) == 0
    partner = jnp.where(first, pltpu.roll(xh, LANES - 32, 1), pltpu.roll(xh, 32, 1))
    return xh * cos + partner * sin_signed


def _rope_tables(pos):
    half = HEAD // 2
    inv = ROPE_THETA ** (-jnp.arange(half, dtype=F32) / half)
    ang = pos.astype(F32)[:, None] * inv[None, :]
    cos, sin = jnp.cos(ang), jnp.sin(ang)
    return jnp.tile(cos, (1, 4)), jnp.concatenate([-sin, sin, -sin, sin], axis=1)


def _split_hi_lo(x):
    hi = x.astype(BF16)
    lo = (x - hi.astype(F32)).astype(BF16)
    return jnp.concatenate([hi, lo], axis=1)


def _group_ones2():
    r = lax.broadcasted_iota(jnp.int32, (2 * LANES, LANES), 0)
    c = lax.broadcasted_iota(jnp.int32, (2 * LANES, LANES), 1)
    return jnp.where(((r & (LANES - 1)) >> 6) == (c >> 6), 1.0, 0.0).astype(BF16)


def _group_sum(x, g2):
    return jnp.dot(_split_hi_lo(x), g2, preferred_element_type=F32)


def _group_sum_wide(x, g2):
    return jnp.concatenate(
        [_group_sum(x[:, c * LANES:(c + 1) * LANES], g2) for c in range(x.shape[1] // LANES)], axis=1)


def _sigmoid(x):
    return 1.0 / (1.0 + jnp.exp(-x))


def _head_rows(n_rows, h, n_heads):
    return pl.ds(h, n_rows, stride=n_heads)


def _in_proj_kernel(x_ref, g_ref, w_ref, qn_ref, kn_ref, cos_ref, sin_ref,
                    prw_ref, qb_ref, k_ref, v_ref, *flash_refs):
    tm = x_ref.shape[0]
    h = _rms_rows(x_ref[...], g_ref[...]).astype(BF16)
    prw_ref[...] = jnp.dot(h, w_ref[:, :RW_COLS], preferred_element_type=F32)
    pd = jnp.dot(h, w_ref[:, RW_COLS:], preferred_element_type=F32)
    cos, sin = cos_ref[...], sin_ref[...]
    qn, kn = qn_ref[...], kn_ref[...]
    for hh in range(N_DIFF_HEADS):
        sl = slice(hh * LANES, (hh + 1) * LANES)
        q = _rope_block(_rms_half(pd[:, sl], qn), cos, sin) * (HEAD ** -0.5 * LOG2E)
        k = _rope_block(_rms_half(pd[:, DIFF_WIDTH + hh * LANES:DIFF_WIDTH + (hh + 1) * LANES], kn), cos, sin)
        v = pd[:, 2 * DIFF_WIDTH + hh * LANES:2 * DIFF_WIDTH + (hh + 1) * LANES]
        qb_ref[:, sl] = q.astype(BF16)
        k_ref[_head_rows(tm, hh, N_DIFF_HEADS), :] = k
        v_ref[_head_rows(tm, hh, N_DIFF_HEADS), :] = v
        if flash_refs:
            kb_ref, vt_ref = flash_refs
            kb_ref[:, sl] = k.astype(BF16)
            vt_ref[0, sl, :] = jnp.transpose(v).astype(BF16)


def _in_proj(x, g, w_bf, qn, kn, cos, sin, tm, seq_len=None):
    n, d = x.shape
    period = cos.shape[0] // tm
    row = lambda i: (i, 0)
    outs = [jax.ShapeDtypeStruct((n, RW_COLS), F32), jax.ShapeDtypeStruct((n, DIFF_WIDTH), BF16),
            jax.ShapeDtypeStruct((n * N_DIFF_HEADS, LANES), F32), jax.ShapeDtypeStruct((n * N_DIFF_HEADS, LANES), F32)]
    out_specs = [pl.BlockSpec((tm, RW_COLS), row), pl.BlockSpec((tm, DIFF_WIDTH), row),
                 pl.BlockSpec((tm * N_DIFF_HEADS, LANES), row), pl.BlockSpec((tm * N_DIFF_HEADS, LANES), row)]
    if seq_len is not None:
        per_seq = seq_len // tm
        outs += [jax.ShapeDtypeStruct((n, DIFF_WIDTH), BF16), jax.ShapeDtypeStruct((n // seq_len, DIFF_WIDTH, seq_len), BF16)]
        out_specs += [pl.BlockSpec((tm, DIFF_WIDTH), row),
                      pl.BlockSpec((1, DIFF_WIDTH, tm), lambda i: (i // per_seq, 0, i % per_seq))]
    return pl.pallas_call(
        _in_proj_kernel,
        grid=(n // tm,),
        in_specs=[pl.BlockSpec((tm, d), row), _const_spec((1, d)), _const_spec(w_bf.shape),
                  _const_spec((1, LANES)), _const_spec((1, LANES)),
                  pl.BlockSpec((tm, LANES), lambda i: (i % period, 0)),
                  pl.BlockSpec((tm, LANES), lambda i: (i % period, 0))],
        out_specs=tuple(out_specs),
        out_shape=tuple(outs),
        compiler_params=_cparams(("parallel",)),
        name="in_proj",
    )(x, g, w_bf, qn, kn, cos, sin)


def _rwkv_token_vectors(p, shifted, prm, g2):
    mu_ref, w0_ref, w2_ref, a0_ref, a2_ref, g2_ref, kk_ref, ka_ref, rk_ref = prm
    ps = p + (shifted - p) * mu_ref[...]
    r = ps[:, :RW_WIDTH]
    k = ps[:, RW_WIDTH:2 * RW_WIDTH]
    v = ps[:, 2 * RW_WIDTH:3 * RW_WIDTH]
    wa = ps[:, 3 * RW_WIDTH:3 * RW_WIDTH + LANES]
    gd = ps[:, 3 * RW_WIDTH + LANES:]
    z = -(w0_ref[...] + jnp.dot(jnp.tanh(wa).astype(BF16), w2_ref[...], preferred_element_type=F32))
    w_log = -(jnp.maximum(z, 0.0) + jnp.log1p(jnp.exp(-jnp.abs(z)))) - 0.5
    a = _sigmoid(a0_ref[...] + jnp.dot(wa.astype(BF16), a2_ref[...], preferred_element_type=F32))
    gate = jnp.dot(_sigmoid(gd).astype(BF16), g2_ref[...], preferred_element_type=F32)
    kk = k * kk_ref[...]
    kk = kk / jnp.maximum(jnp.sqrt(_group_sum_wide(kk * kk, g2)), 1e-12)
    k2 = k * (1.0 + (a - 1.0) * ka_ref[...])
    bonus = _group_sum_wide(r * k2 * rk_ref[...], g2) * v
    return r, -jnp.exp(w_log), k2, v, -kk, kk * a, bonus, gate


def _rwkv_finish(y, bonus, gate, lw, lb, g2):
    mean = _group_sum_wide(y, g2) * (1.0 / HEAD)
    d = y - mean
    var = _group_sum_wide(d * d, g2) * (1.0 / HEAD)
    return (d * lax.rsqrt(var + LNX_EPS) * lw + lb + bonus) * gate


def _mm(x, w, terms):
    if terms == 1:
        return jnp.dot(x.astype(BF16), w.astype(BF16), preferred_element_type=F32)
    w_hi = w.astype(BF16)
    out = jnp.dot(_split_hi_lo(x), jnp.concatenate([w_hi, w_hi], axis=0), preferred_element_type=F32)
    if terms == 3:
        w_lo = (w - w_hi.astype(F32)).astype(BF16)
        out = out + jnp.dot(x.astype(BF16), w_lo, preferred_element_type=F32)
    return out


def _stack_heads(x):
    lo = _lane_iota(x.shape) < HEAD
    return jnp.concatenate([jnp.where(lo, x, 0.0), jnp.where(lo, 0.0, x)], axis=0)


def _nt_mm(x, w, terms):
    if terms == 1:
        return _nt_dot(x.astype(BF16), w.astype(BF16))
    w_hi = w.astype(BF16)
    out = _nt_dot(_split_hi_lo(x), jnp.concatenate([w_hi, w_hi], axis=1))
    if terms == 3:
        out = out + _nt_dot(x.astype(BF16), (w - w_hi.astype(F32)).astype(BF16))
    return out


def _chunk_units(units, slots, zs, terms):
    n = len(units)
    c2 = 2 * units[0][0].shape[0]
    ti = lax.broadcasted_iota(jnp.int32, (c2, c2), 0)
    tj = lax.broadcasted_iota(jnp.int32, (c2, c2), 1)
    eye = ti == tj
    strict = (ti & (c2 // 2 - 1)) > (tj & (c2 // 2 - 1))
    incl = (ti & (c2 // 2 - 1)) >= (tj & (c2 // 2 - 1))
    s_a = [_stack_heads(u[0]) for u in units]
    s_r = [_stack_heads(u[1]) for u in units]
    s_v = [_stack_heads(u[6]) for u in units]
    sc = [_nt_mm(jnp.concatenate([s_a[i], s_r[i]], axis=0),
                 jnp.concatenate([_stack_heads(units[i][2]), _stack_heads(units[i][3])], axis=0), terms)
          for i in range(n)]
    x = [jnp.where(strict, s[:c2, :c2], 0.0) for s in sc]
    l_ak = [jnp.where(strict, s[:c2, c2:], 0.0) for s in sc]
    m_rbk = [jnp.concatenate([jnp.where(incl, s[c2:, :c2], 0.0), jnp.where(incl, s[c2:, c2:], 0.0)], axis=1) for s in sc]
    t_inv = [jnp.where(eye, 1.0, 0.0) + xi for xi in x]
    lv = [_mm(l_ak[i], s_v[i], terms) for i in range(n)]
    for _ in range(max(1, (c2 // 2 - 1).bit_length()) - 1):
        x = [_mm(xi, xi, terms) for xi in x]
        t_inv = [t_inv[i] + _mm(t_inv[i], x[i], terms) for i in range(n)]
    pq = [_mm(t_inv[i], jnp.concatenate([s_a[i], lv[i]], axis=1), terms) for i in range(n)]
    pqv = [jnp.concatenate([pq[i], jnp.concatenate([jnp.zeros_like(s_v[i]), s_v[i]], axis=1)], axis=0) for i in range(n)]
    ah = [_mm(jnp.transpose(jnp.concatenate([_stack_heads(units[i][4]), _stack_heads(units[i][5])], axis=0)), pqv[i], terms)
          for i in range(n)]
    ry = [_mm(m_rbk[i], pqv[i], terms) for i in range(n)]
    zs, ys = list(zs), []
    for i in range(n):
        z = zs[slots[i]]
        y = _mm(s_r[i] + ry[i][:, :c2], z, 3) + ry[i][:, c2:]
        ys.append(y[:c2 // 2] + y[c2 // 2:])
        zs[slots[i]] = _mm(jnp.where(eye, units[i][7], 0.0) + ah[i][:, :c2], z, 3) + ah[i][:, c2:]
    return ys, zs


def _rwkv_kernel(p_ref, prev_ref, s0_ref, mu_ref, w0_ref, w2_ref, a0_ref, a2_ref, g2_ref,
                 kk_ref, ka_ref, rk_ref, lw_ref, lb_ref,
                 out_ref, st_ref,
                 state, carry, r_s, w_s, k_s, v_s, a_s, b_s, y_s, bonus_s, gate_s, *, ns, tt):
    ti = pl.program_id(1)
    npair = RW_WIDTH // LANES
    g2 = _group_ones2()

    @pl.when(ti == 0)
    def _():
        for s in range(ns):
            for j in range(npair):
                state[(s * npair + j) * HEAD:(s * npair + j + 1) * HEAD, :] = jnp.concatenate(
                    [s0_ref[s, 2 * j], s0_ref[s, 2 * j + 1]], axis=1)
        carry[...] = prev_ref[...]

    row0 = lax.broadcasted_iota(jnp.int32, (tt, RW_COLS), 0) == 0
    prm = (mu_ref, w0_ref, w2_ref, a0_ref, a2_ref, g2_ref, kk_ref, ka_ref, rk_ref)
    for s in range(ns):
        p = p_ref[s]
        shifted = jnp.where(row0, carry[s], pltpu.roll(p, 1, 0))
        carry[s] = p[tt - 1:tt, :]
        r_s[s], log_w, k_s[s], v_s[s], a_s[s], b_s[s], bonus_s[s], gate_s[s] = _rwkv_token_vectors(p, shifted, prm, g2)
        w_s[s] = jnp.exp(log_w)

    tiles = [(s, j) for s in range(ns) for j in range(npair)]
    nt = len(tiles)
    sub = lax.broadcasted_iota(jnp.int32, (nt * HEAD, LANES), 0)
    e1 = jnp.where((_lane_iota((nt * HEAD, LANES)) & (HEAD - 1)) == (sub & (HEAD - 1)), 1.0, 0.0)
    seqs = dict(r=r_s, w=w_s, k=k_s, v=v_s, a=a_s, b=b_s)

    def block(tb, c):
        base = pl.multiple_of(tb * 8, 8)
        rows8 = {q: [ref[s, pl.ds(base, 8), j * LANES:(j + 1) * LANES] for (s, j) in tiles] for q, ref in seqs.items()}
        ys = []
        for i in range(8):
            bc = lambda q: jnp.concatenate(
                [jnp.broadcast_to(rows8[q][n][i:i + 1], (HEAD, LANES)) for n in range(nt)], axis=0)
            st = state[...]
            sa = jnp.dot(_split_hi_lo(st * bc("a")), g2, preferred_element_type=F32)
            vcol = jnp.dot((e1 * bc("v")).astype(BF16), g2[:LANES], preferred_element_type=F32)
            st = st * bc("w") + sa * bc("b") + vcol * bc("k")
            state[...] = st
            yred = jnp.dot((st * bc("r")).astype(BF16), g2[:LANES], preferred_element_type=F32)
            ys.append(jnp.sum((e1 * yred).reshape(nt, HEAD, LANES), axis=1))
        for n, (s, j) in enumerate(tiles):
            y_s[s, pl.ds(base, 8), j * LANES:(j + 1) * LANES] = jnp.concatenate([y[n:n + 1] for y in ys], axis=0)
        return c

    lax.fori_loop(0, tt // 8, block, 0)

    for s in range(ns):
        out_ref[s] = _rwkv_finish(y_s[s], bonus_s[s], gate_s[s], lw_ref[...], lb_ref[...], g2)

    @pl.when(ti == pl.num_programs(1) - 1)
    def _():
        for s in range(ns):
            for j in range(npair):
                tile = state[(s * npair + j) * HEAD:(s * npair + j + 1) * HEAD, :]
                st_ref[s, 2 * j] = tile[:, :HEAD]
                st_ref[s, 2 * j + 1] = tile[:, HEAD:]


def _rwkv_chunk_kernel(p_ref, prev_ref, z0_ref, mu_ref, w0_ref, w2_ref, a0_ref, a2_ref, g2_ref,
                       kk_ref, ka_ref, rk_ref, lw_ref, lb_ref,
                       out_ref, zt_ref,
                       zstate, carry, r_s, g_s, k_s, v_s, a_s, b_s, y_s, bonus_s, gate_s, *, tt, chunk, cpt, terms):
    ti = pl.program_id(1)
    npair = RW_WIDTH // LANES
    g2 = _group_ones2()

    @pl.when(ti == 0)
    def _():
        zstate[...] = z0_ref[0]
        carry[...] = prev_ref[0]

    p = p_ref[0]
    row0 = lax.broadcasted_iota(jnp.int32, (tt, RW_COLS), 0) == 0
    shifted = jnp.where(row0, carry[...], pltpu.roll(p, 1, 0))
    carry[...] = p[tt - 1:tt, :]
    prm = (mu_ref, w0_ref, w2_ref, a0_ref, a2_ref, g2_ref, kk_ref, ka_ref, rk_ref)
    (r_s[...], g_s[...], k_s[...], v_s[...], a_s[...], b_s[...], bonus_s[...], gate_s[...]) = _rwkv_token_vectors(
        p, shifted, prm, g2)

    ci = lax.broadcasted_iota(jnp.int32, (chunk, 3 * chunk), 0)
    cj = lax.broadcasted_iota(jnp.int32, (chunk, 3 * chunk), 1)
    cj = jnp.where(cj >= 2 * chunk, cj - 2 * chunk, jnp.where(cj >= chunk, cj - chunk, cj))
    ltri3 = jnp.where(cj <= ci, 1.0, 0.0).astype(BF16)

    lanes = [slice(j * LANES, (j + 1) * LANES) for j in range(npair)]

    def scaled(c):
        rows = pl.ds(pl.multiple_of(c * chunk, chunk), chunk)
        g = g_s[rows, :]
        g_hi = g.astype(BF16)
        g_r = g - g_hi.astype(F32)
        g_mid = g_r.astype(BF16)
        g_lo = (g_r - g_mid.astype(F32)).astype(BF16)
        cum = jnp.dot(ltri3, jnp.concatenate([g_hi, g_mid, g_lo], axis=0), preferred_element_type=F32)
        total = cum[chunk - 1:chunk, :]
        e_in, e_ex, e_inv, e_rest = jnp.exp(cum), jnp.exp(cum - g), jnp.exp(-cum), jnp.exp(total - cum)
        a_t, r_t = a_s[rows, :] * e_ex, r_s[rows, :] * e_in
        b, k, v = b_s[rows, :], k_s[rows, :], v_s[rows, :]
        ops = (a_t, r_t, b * e_inv, k * e_inv, b * e_rest, k * e_rest, v, jnp.exp(total))
        return rows, [tuple(t[:, sl] for t in ops) for sl in lanes]

    def body(trip, carry_):
        parts = [scaled(trip * cpt + c) for c in range(cpt)]
        ys, z_new = _chunk_units([u for _, units in parts for u in units], list(range(npair)) * cpt,
                                 [zstate[sl, :] for sl in lanes], terms)
        for c, (rows, _) in enumerate(parts):
            for j, sl in enumerate(lanes):
                y_s[rows, sl] = ys[c * npair + j]
        for j, sl in enumerate(lanes):
            zstate[sl, :] = z_new[j]
        return carry_

    lax.fori_loop(0, tt // (cpt * chunk), body, 0)
    out_ref[0] = _rwkv_finish(y_s[...], bonus_s[...], gate_s[...], lw_ref[...], lb_ref[...], g2)

    @pl.when(ti == pl.num_programs(1) - 1)
    def _():
        zt_ref[0] = zstate[...]


def _rwkv(prw, prev, s0, rw, ns, tt):
    S, T, _ = prw.shape
    npair = RW_WIDTH // LANES
    vec = lambda n: _const_spec((1, n))
    blk = lambda w: pl.BlockSpec((ns, tt, w), lambda i, t: (i, t, 0))
    st_spec = pl.BlockSpec((ns, 2 * npair, HEAD, HEAD), lambda i, t: (i, 0, 0, 0))
    seq = lambda: pltpu.VMEM((ns, tt, RW_WIDTH), F32)
    return pl.pallas_call(
        functools.partial(_rwkv_kernel, ns=ns, tt=tt),
        grid=(S // ns, T // tt),
        in_specs=[blk(RW_COLS), pl.BlockSpec((ns, 1, RW_COLS), lambda i, t: (i, 0, 0)), st_spec,
                  vec(RW_COLS), vec(RW_WIDTH), _const_spec((LANES, RW_WIDTH)), vec(RW_WIDTH),
                  _const_spec((LANES, RW_WIDTH)), _const_spec((LANES, RW_WIDTH)),
                  vec(RW_WIDTH), vec(RW_WIDTH), vec(RW_WIDTH), vec(RW_WIDTH), vec(RW_WIDTH)],
        out_specs=(blk(RW_WIDTH), st_spec),
        out_shape=(jax.ShapeDtypeStruct((S, T, RW_WIDTH), F32),
                   jax.ShapeDtypeStruct((S, 2 * npair, HEAD, HEAD), F32)),
        scratch_shapes=[pltpu.VMEM((ns * npair * HEAD, LANES), F32), pltpu.VMEM((ns, 1, RW_COLS), F32)]
                       + [seq() for _ in range(9)],
        compiler_params=_cparams(("parallel", "arbitrary")),
        name="rwkv",
    )(prw, prev, s0, *rw)


def _rwkv_chunked(prw, prev, z0, rw, tt, chunk, terms):
    S, T, _ = prw.shape
    vec = lambda n: _const_spec((1, n))
    blk = lambda w: pl.BlockSpec((1, tt, w), lambda i, t: (i, t, 0))
    z_spec = pl.BlockSpec((1, RW_WIDTH, LANES), lambda i, t: (i, 0, 0))
    seq = lambda: pltpu.VMEM((tt, RW_WIDTH), F32)
    return pl.pallas_call(
        functools.partial(_rwkv_chunk_kernel, tt=tt, chunk=chunk, cpt=_tile(tt // chunk, RWKV_CHUNKS_PER_TRIP),
                          terms=terms),
        grid=(S, T // tt),
        in_specs=[blk(RW_COLS), pl.BlockSpec((1, 1, RW_COLS), lambda i, t: (i, 0, 0)), z_spec,
                  vec(RW_COLS), vec(RW_WIDTH), _const_spec((LANES, RW_WIDTH)), vec(RW_WIDTH),
                  _const_spec((LANES, RW_WIDTH)), _const_spec((LANES, RW_WIDTH)),
                  vec(RW_WIDTH), vec(RW_WIDTH), vec(RW_WIDTH), vec(RW_WIDTH), vec(RW_WIDTH)],
        out_specs=(blk(RW_WIDTH), z_spec),
        out_shape=(jax.ShapeDtypeStruct((S, T, RW_WIDTH), F32), jax.ShapeDtypeStruct((S, RW_WIDTH, LANES), F32)),
        scratch_shapes=[pltpu.VMEM((RW_WIDTH, LANES), F32), pltpu.VMEM((1, RW_COLS), F32)] + [seq() for _ in range(9)],
        compiler_params=_cparams(("parallel", "arbitrary")),
        name="rwkv_chunked",
    )(prw, prev, z0, *rw)


def _state_from_blockdiag(z, S):
    z = z.reshape(S, 4, 2, HEAD, 2, HEAD)
    z = jnp.stack([z[:, :, 0, :, 0, :], z[:, :, 1, :, 1, :]], axis=2)
    return z.reshape(S, 8, HEAD, HEAD).transpose(0, 1, 3, 2)


def _lambda(lamp_ref, lam_init):
    lp = lamp_ref[...]
    s1 = jnp.sum(lp[0:1] * lp[1:2], axis=-1, keepdims=True)
    s2 = jnp.sum(lp[2:3] * lp[3:4], axis=-1, keepdims=True)
    return jnp.exp(s1) - jnp.exp(s2) + lam_init


def _stack_halves(q):
    lo = _lane_iota(q.shape) < HEAD
    zero = jnp.zeros_like(q)
    return jnp.concatenate([jnp.where(lo, q, zero), jnp.where(lo, zero, q)], axis=0)


def _diff_finish(acc, l, lam, sn, lam_init, rows):
    o = acc[:rows] / l[:rows] - lam * (acc[rows:] / l[rows:])
    return _rms_rows(o, sn) * (1.0 - lam_init)


def _diff_prompt_kernel(q_ref, k_ref, vt_ref, lamp_ref, sn_ref, o_ref, m_ref, acc_ref, *, tq, hp, lam_init):
    i = pl.program_id(2)
    heads = [slice(h * LANES, (h + 1) * LANES) for h in range(hp)]
    q2 = [_stack_halves(q_ref[0, :, sl]) for sl in heads]
    start = pl.multiple_of(i * tq, tq)
    s = [_nt_dot(k_ref[0, pl.ds(start, tq), sl], q2[h]) for h, sl in enumerate(heads)]
    key = lax.broadcasted_iota(jnp.int32, s[0].shape, 0)
    qry = lax.broadcasted_iota(jnp.int32, s[0].shape, 1)
    causal = key <= jnp.where(qry >= tq, qry - tq, qry)
    s = [jnp.where(causal, sh, NEG_BIG) for sh in s]
    m = [jnp.max(sh, axis=0, keepdims=True) for sh in s]
    p = [jnp.exp2(s[h] - m[h]) for h in range(hp)]
    ones = jnp.ones((ONES_ROWS, tq), BF16)

    def values(off, sl):
        return jnp.concatenate([vt_ref[0, sl, pl.ds(off, tq)], ones], axis=0)

    for h, sl in enumerate(heads):
        m_ref[h] = m[h]
        acc_ref[h] = jnp.dot(values(start, sl), p[h].astype(BF16), preferred_element_type=F32)

    def body(j, c):
        off = pl.multiple_of(j * tq, tq)
        s = [_nt_dot(k_ref[0, pl.ds(off, tq), sl], q2[h]) for h, sl in enumerate(heads)]
        m_old = [m_ref[h] for h in range(hp)]
        m_new = [jnp.maximum(m_old[h], jnp.max(s[h], axis=0, keepdims=True)) for h in range(hp)]
        alpha = [jnp.exp2(m_old[h] - m_new[h]) for h in range(hp)]
        p = [jnp.exp2(s[h] - m_new[h]) for h in range(hp)]
        pv = [jnp.dot(values(off, sl), p[h].astype(BF16), preferred_element_type=F32) for h, sl in enumerate(heads)]
        for h in range(hp):
            acc_ref[h] = alpha[h] * acc_ref[h] + pv[h]
            m_ref[h] = m_new[h]
        return c

    lax.fori_loop(0, i, body, 0)
    lam = _lambda(lamp_ref, lam_init)
    for h, sl in enumerate(heads):
        acc = acc_ref[h]
        acc = acc[:LANES] / acc[LANES:LANES + 1]
        o_t = acc[:, :tq] - lam * acc[:, tq:]
        ms = jnp.mean(o_t * o_t, axis=0, keepdims=True)
        o_t = o_t * lax.rsqrt(ms + RMS_EPS)
        o_ref[0, :, sl] = jnp.transpose(o_t) * sn_ref[...] * (1.0 - lam_init)


def _diff_prompt(qb, kb, vt, lamp, sn, tq, hp, lam_init):
    B, T, _ = qb.shape
    tile = pl.BlockSpec((1, tq, hp * LANES), lambda b, h, i: (b, i, h))
    return pl.pallas_call(
        functools.partial(_diff_prompt_kernel, tq=tq, hp=hp, lam_init=lam_init),
        grid=(B, N_DIFF_HEADS // hp, T // tq),
        in_specs=[tile, pl.BlockSpec((1, T, hp * LANES), lambda b, h, i: (b, 0, h)),
                  pl.BlockSpec((1, hp * LANES, T), lambda b, h, i: (b, h, 0)),
                  _const_spec((4, HEAD)), _const_spec((1, LANES))],
        out_specs=tile,
        out_shape=jax.ShapeDtypeStruct((B, T, DIFF_WIDTH), F32),
        scratch_shapes=[pltpu.VMEM((hp, 1, 2 * tq), F32), pltpu.VMEM((hp, LANES + ONES_ROWS, 2 * tq), F32)],
        compiler_params=_cparams(("parallel", "parallel", "arbitrary")),
        name="diff_prompt",
    )(qb, kb, vt, lamp, sn)


def _diff_sample_kernel(pt_ref, q_ref, kn_ref, vn_ref, lamp_ref, sn_ref, *rest, pp, s_new, lam_init):
    del pt_ref
    kpages, vpages = rest[:pp], rest[pp:2 * pp]
    o_ref, m_ref, l_ref, acc_ref = rest[2 * pp:]
    j = pl.program_id(1)
    nh = N_DIFF_HEADS
    rows = 2 * s_new
    q = q_ref[0]
    q2 = [_stack_halves(q[:, h * LANES:(h + 1) * LANES]) for h in range(nh)]

    def update(scores, values):
        s = jnp.concatenate([jnp.concatenate(sc, axis=1) if len(sc) > 1 else sc[0] for sc in scores], axis=0)
        m_old = m_ref[...]
        m_new = jnp.maximum(m_old, jnp.max(s, axis=-1, keepdims=True))
        alpha = jnp.exp2(m_old - m_new)
        p = jnp.exp2(s - m_new)
        pv = []
        for h in range(nh):
            ph = p[h * rows:(h + 1) * rows].astype(BF16)
            acc = jnp.dot(ph[:, :PAGE], values[h][0], preferred_element_type=F32)
            for c in range(1, len(values[h])):
                acc = acc + jnp.dot(ph[:, c * PAGE:(c + 1) * PAGE], values[h][c], preferred_element_type=F32)
            pv.append(acc)
        l_ref[...] = alpha * l_ref[...] + jnp.sum(p, axis=-1, keepdims=True)
        acc_ref[...] = alpha * acc_ref[...] + jnp.concatenate(pv, axis=0)
        m_ref[...] = m_new

    @pl.when(j == 0)
    def _():
        m_ref[...] = jnp.full(m_ref.shape, NEG_BIG, F32)
        l_ref[...] = jnp.zeros(l_ref.shape, F32)
        acc_ref[...] = jnp.zeros(acc_ref.shape, F32)
        pad = jnp.zeros((PAGE - s_new, LANES), F32)
        row = lax.broadcasted_iota(jnp.int32, (rows, PAGE), 0)
        col = lax.broadcasted_iota(jnp.int32, (rows, PAGE), 1)
        mask = col <= jnp.where(row >= s_new, row - s_new, row)
        new_rows = lambda ref, h: jnp.concatenate([ref[0, _head_rows(s_new, h, nh), :], pad], axis=0).astype(BF16)
        update([[jnp.where(mask, _nt_dot(q2[h], new_rows(kn_ref, h)), NEG_BIG)] for h in range(nh)],
               [[new_rows(vn_ref, h)] for h in range(nh)])

    page_rows = lambda ref, h: ref[_head_rows(PAGE, h, nh), :].astype(BF16)
    update([[_nt_dot(q2[h], page_rows(kp, h)) for kp in kpages] for h in range(nh)],
           [[page_rows(vp, h) for vp in vpages] for h in range(nh)])

    @pl.when(j == pl.num_programs(1) - 1)
    def _():
        lam = _lambda(lamp_ref, lam_init)
        acc, l = acc_ref[...], l_ref[...]
        for h in range(nh):
            o_ref[0, :, h * LANES:(h + 1) * LANES] = _diff_finish(
                acc[h * rows:(h + 1) * rows], l[h * rows:(h + 1) * rows], lam, sn_ref[...], lam_init, s_new)


def _diff_sample(page_table, qb, k_new, v_new, pool_k, pool_v, lamp, sn, pp, lam_init):
    Bs, S, _ = qb.shape
    n_pages = page_table.shape[1]
    nh = N_DIFF_HEADS
    qspec = pl.BlockSpec((1, S, DIFF_WIDTH), lambda b, j, pt: (b, 0, 0))
    new = pl.BlockSpec((1, S * nh, LANES), lambda b, j, pt: (b, 0, 0))
    page = lambda c: pl.BlockSpec((None, PAGE * nh, LANES), lambda b, j, pt, c=c: (pt[b, j * pp + c], 0, 0))
    cst = lambda shape: pl.BlockSpec(shape, lambda b, j, pt: (0,) * len(shape))
    return pl.pallas_call(
        functools.partial(_diff_sample_kernel, pp=pp, s_new=S, lam_init=lam_init),
        grid_spec=pltpu.PrefetchScalarGridSpec(
            num_scalar_prefetch=1,
            grid=(Bs, n_pages // pp),
            in_specs=[qspec, new, new, cst((4, HEAD)), cst((1, LANES))] + [page(c) for c in range(pp)] * 2,
            out_specs=qspec,
            scratch_shapes=[pltpu.VMEM((nh * 2 * S, 1), F32), pltpu.VMEM((nh * 2 * S, 1), F32),
                            pltpu.VMEM((nh * 2 * S, LANES), F32)]),
        out_shape=jax.ShapeDtypeStruct((Bs, S, DIFF_WIDTH), F32),
        compiler_params=_cparams(("parallel", "arbitrary")),
        name="diff_sample",
    )(page_table, qb, k_new, v_new, lamp, sn, *([pool_k] * pp), *([pool_v] * pp))


def _mem_kv_kernel(x_ref, g_ref, w_ref, kn_ref, k_ref, v_ref):
    h = _rms_rows(x_ref[...], g_ref[...]).astype(BF16)
    kv = jnp.dot(h, w_ref[...], preferred_element_type=F32)
    tm = x_ref.shape[0]
    for hh in range(N_MEM_HEADS):
        sl = slice(hh * LANES, (hh + 1) * LANES)
        k_ref[_head_rows(tm, hh, N_MEM_HEADS), :] = _rms_rows(kv[:, sl], kn_ref[...])
        v_ref[_head_rows(tm, hh, N_MEM_HEADS), :] = kv[:, MEM_WIDTH + hh * LANES:MEM_WIDTH + (hh + 1) * LANES]


def _mem_kv(mem, g, w_bf, kn, tm):
    n, d = mem.shape
    row = lambda i: (i, 0)
    return pl.pallas_call(
        _mem_kv_kernel,
        grid=(n // tm,),
        in_specs=[pl.BlockSpec((tm, d), row), _const_spec((1, d)), _const_spec(w_bf.shape), _const_spec((1, LANES))],
        out_specs=(pl.BlockSpec((tm * N_MEM_HEADS, LANES), row),) * 2,
        out_shape=(jax.ShapeDtypeStruct((n * N_MEM_HEADS, LANES), F32),) * 2,
        compiler_params=_cparams(("parallel",)),
        name="mem_kv",
    )(mem, g, w_bf, kn)


def _mem_attn_kernel(x_ref, r_ref, a_ref, wmix_ref, mk_ref, mv_ref, g_ref, wq_ref, qn_ref, wo_ref, y_ref, o_scr,
                     *, nb, tr, n_mem):
    d = x_ref.shape[-1]
    rows = nb * tr
    x = x_ref[...].reshape(rows, d)
    x = x + jnp.dot(r_ref[...].reshape(rows, RW_WIDTH).astype(BF16), wmix_ref[:RW_WIDTH, :], preferred_element_type=F32)
    x = x + jnp.dot(a_ref[...].reshape(rows, DIFF_WIDTH).astype(BF16), wmix_ref[RW_WIDTH:, :], preferred_element_type=F32)
    h = _rms_rows(x, g_ref[...]).astype(BF16)
    q = jnp.dot(h, wq_ref[...], preferred_element_type=F32)
    heads = [slice(hh * LANES, (hh + 1) * LANES) for hh in range(N_MEM_HEADS)]
    qh = [(_rms_rows(q[:, sl], qn_ref[...]) * (LANES ** -0.5 * LOG2E)).astype(BF16) for sl in heads]
    units = [(hh, n) for hh in range(N_MEM_HEADS) for n in range(nb)]
    mem_rows = lambda ref, hh, n: ref[n, _head_rows(n_mem, hh, N_MEM_HEADS), :].astype(BF16)
    s = [_nt_dot(qh[hh][n * tr:(n + 1) * tr], mem_rows(mk_ref, hh, n)) for hh, n in units]
    p = [jnp.exp2(su - jnp.max(su, axis=-1, keepdims=True)) for su in s]
    o = [jnp.dot(p[u].astype(BF16), mem_rows(mv_ref, hh, n), preferred_element_type=F32) for u, (hh, n) in enumerate(units)]
    for u, (hh, n) in enumerate(units):
        o_scr[n * tr:(n + 1) * tr, heads[hh]] = o[u] / jnp.sum(p[u], axis=-1, keepdims=True)
    y = x + jnp.dot(o_scr[...].astype(BF16), wo_ref[...], preferred_element_type=F32)
    y_ref[...] = y.reshape(nb, tr, d)


def _mem_attn(x, r_out, attn, wmix_bf, mk, mv, g, wq_bf, qn, wo_bf, nb, tr):
    G, R, d = x.shape
    M4 = mk.shape[1]
    tile = lambda w: pl.BlockSpec((nb, tr, w), lambda i, t: (i, t, 0))
    mem = pl.BlockSpec((nb, M4, LANES), lambda i, t: (i, 0, 0))
    return pl.pallas_call(
        functools.partial(_mem_attn_kernel, nb=nb, tr=tr, n_mem=M4 // N_MEM_HEADS),
        grid=(G // nb, R // tr),
        in_specs=[tile(d), tile(RW_WIDTH), tile(DIFF_WIDTH), _const_spec(wmix_bf.shape), mem, mem,
                  _const_spec((1, d)), _const_spec(wq_bf.shape), _const_spec((1, LANES)), _const_spec(wo_bf.shape)],
        out_specs=tile(d),
        out_shape=jax.ShapeDtypeStruct((G, R, d), F32),
        scratch_shapes=[pltpu.VMEM((nb * tr, MEM_WIDTH), F32)],
        compiler_params=_cparams(("parallel", "arbitrary")),
        name="mem_attn",
    )(x, r_out, attn, wmix_bf, mk, mv, g, wq_bf, qn, wo_bf)


def _ffn_kernel(x_ref, g_ref, wgu_ref, wd_ref, y_ref, *, d_ff, fc):
    x = x_ref[...]
    h = _rms_rows(x, g_ref[...]).astype(BF16)
    y = x
    for c in range(d_ff // fc):
        gt = jnp.dot(h, wgu_ref[:, c * fc:(c + 1) * fc], preferred_element_type=F32)
        up = jnp.dot(h, wgu_ref[:, d_ff + c * fc:d_ff + (c + 1) * fc], preferred_element_type=F32)
        act = (gt * _sigmoid(gt) * up).astype(BF16)
        y = y + jnp.dot(act, wd_ref[c * fc:(c + 1) * fc, :], preferred_element_type=F32)
    y_ref[...] = y


def _ffn(x, g, wgu_bf, wd_bf, tm):
    n, d = x.shape
    d_ff = wd_bf.shape[0]
    fc = d_ff // 2 if (d_ff // 2) % LANES == 0 else d_ff
    row = lambda i: (i, 0)
    return pl.pallas_call(
        functools.partial(_ffn_kernel, d_ff=d_ff, fc=fc),
        grid=(n // tm,),
        in_specs=[pl.BlockSpec((tm, d), row), _const_spec((1, d)), _const_spec(wgu_bf.shape), _const_spec(wd_bf.shape)],
        out_specs=pl.BlockSpec((tm, d), row),
        out_shape=jax.ShapeDtypeStruct((n, d), F32),
        compiler_params=_cparams(("parallel",)),
        name="ffn",
    )(x, g, wgu_bf, wd_bf)


def _tile(n, pref):
    t = min(n, pref)
    while n % t:
        t //= 2
    return t


def kernel(x_prompt, x_sample, cache_diff_k, cache_diff_v, cache_mem_k, cache_mem_v, state_rwkv, state_shift, page_table, mem_prompt, norm_mix, w_in, rwkv_mu, rwkv_w0, rwkv_w2, rwkv_a0, rwkv_a2, rwkv_g2, rwkv_k_k, rwkv_k_a, rwkv_r_k, rwkv_lnx_w, rwkv_lnx_b, diff_q_norm, diff_k_norm, diff_lam_q1, diff_lam_k1, diff_lam_q2, diff_lam_k2, diff_sub_norm, w_o, norm_mem, norm_memkv, w_mq, w_mkv, mem_q_norm, mem_k_norm, w_mo, norm_ffn, w_gate_up, w_down):
    B, T, D = x_prompt.shape
    Bs, S, _ = x_sample.shape
    depth = w_in.shape[0]
    n_pages = page_table.shape[1]
    past = n_pages * PAGE
    M = mem_prompt.shape[1]
    xp = x_prompt.reshape(B * T, D)
    xs = x_sample.reshape(Bs * S, D)
    mem = mem_prompt.reshape(B * M, D)

    tm_p = _tile(T, 256)
    tm_wide = _tile(T, 512)
    tm_s = _tile(Bs * S, 256)
    cos_p, sin_p = _rope_tables(jnp.arange(T, dtype=jnp.int32))
    cos_s, sin_s = _rope_tables(past + jnp.arange(S, dtype=jnp.int32))
    cos_s, sin_s = jnp.tile(cos_s, (tm_s // S, 1)), jnp.tile(sin_s, (tm_s // S, 1))
    tq = _tile(T, 512)
    pp = _tile(n_pages, 32)
    ns_s = _tile(Bs, 4)
    nb_mem = _tile(Bs, 16)
    tt_p = _tile(T, 256)

    outs = [[] for _ in range(10)]
    for l in range(depth):
        lam_init = 0.8 - 0.6 * math.exp(-0.3 * l)
        row = lambda a: a[l].reshape(1, -1)
        bf = lambda a: a[l].astype(BF16)
        zpad = jnp.zeros((HEAD, RW_WIDTH), F32)
        rw = (row(rwkv_mu), row(rwkv_w0), jnp.concatenate([rwkv_w2[l], zpad], axis=0).astype(BF16), row(rwkv_a0),
              jnp.concatenate([zpad, rwkv_a2[l]], axis=0).astype(BF16), bf(rwkv_g2), row(rwkv_k_k), row(rwkv_k_a),
              row(rwkv_r_k), row(rwkv_lnx_w), row(rwkv_lnx_b))
        lamp = jnp.stack([diff_lam_q1[l], diff_lam_k1[l], diff_lam_q2[l], diff_lam_k2[l]])
        w_in_bf, w_o_bf, w_mq_bf, w_mkv_bf, w_mo_bf = bf(w_in), bf(w_o), bf(w_mq), bf(w_mkv), bf(w_mo)
        w_gu_bf, w_d_bf = bf(w_gate_up), bf(w_down)
        qn, kn, sn = row(diff_q_norm), row(diff_k_norm), row(diff_sub_norm)

        prw, qb, k, v, kb, vt = _in_proj(xp, row(norm_mix), w_in_bf, qn, kn, cos_p, sin_p, tm_p, seq_len=T)
        prw3 = prw.reshape(B, T, RW_COLS)
        r_out, z_p = _rwkv_chunked(prw3, jnp.zeros((B, 1, RW_COLS), F32), jnp.zeros((B, RW_WIDTH, LANES), F32), rw,
                                   tt_p, _tile(tt_p, RWKV_CHUNK), RWKV_TERMS)
        o = _diff_prompt(qb.reshape(B, T, -1), kb.reshape(B, T, -1), vt, lamp, sn, tq, 4, lam_init)
        mk, mv = _mem_kv(mem, row(norm_memkv), w_mkv_bf, row(mem_k_norm), _tile(M, 256))
        x2 = _mem_attn(xp.reshape(B, T, D), r_out, o, w_o_bf,
                       mk.reshape(B, M * N_MEM_HEADS, LANES), mv.reshape(B, M * N_MEM_HEADS, LANES),
                       row(norm_mem), w_mq_bf, row(mem_q_norm), w_mo_bf, 1, tm_wide)
        xp = _ffn(x2.reshape(B * T, D), row(norm_ffn), w_gu_bf, w_d_bf, tm_wide)
        outs[0].append(k.reshape(B, T, N_DIFF_HEADS, LANES))
        outs[1].append(v.reshape(B, T, N_DIFF_HEADS, LANES))
        outs[2].append(mk.reshape(B, M, N_MEM_HEADS, LANES))
        outs[3].append(mv.reshape(B, M, N_MEM_HEADS, LANES))
        outs[4].append(_state_from_blockdiag(z_p, B))
        outs[5].append(prw3[:, -1])

        prw, qb, k, v = _in_proj(xs, row(norm_mix), w_in_bf, qn, kn, cos_s, sin_s, tm_s)
        prw3 = prw.reshape(Bs, S, RW_COLS)
        r_out, st_s = _rwkv(prw3, state_shift[l][:, None, :], state_rwkv[l], rw, ns_s, S)
        o = _diff_sample(page_table, qb.reshape(Bs, S, -1), k.reshape(Bs, S * N_DIFF_HEADS, LANES),
                         v.reshape(Bs, S * N_DIFF_HEADS, LANES),
                         cache_diff_k[l].reshape(-1, PAGE * N_DIFF_HEADS, LANES),
                         cache_diff_v[l].reshape(-1, PAGE * N_DIFF_HEADS, LANES), lamp, sn, pp, lam_init)
        x2 = _mem_attn(xs.reshape(Bs, S, D), r_out, o, w_o_bf, cache_mem_k[l].reshape(Bs, M * N_MEM_HEADS, LANES),
                       cache_mem_v[l].reshape(Bs, M * N_MEM_HEADS, LANES),
                       row(norm_mem), w_mq_bf, row(mem_q_norm), w_mo_bf, nb_mem, S)
        xs = _ffn(x2.reshape(Bs * S, D), row(norm_ffn), w_gu_bf, w_d_bf, tm_s)
        outs[6].append(k.reshape(Bs, S, N_DIFF_HEADS, LANES))
        outs[7].append(v.reshape(Bs, S, N_DIFF_HEADS, LANES))
        outs[8].append(st_s)
        outs[9].append(prw3[:, -1])

    st = [jnp.stack(o) if depth > 1 else o[0][None] for o in outs]
    return (xp.reshape(B, T, D), xs.reshape(Bs, S, D), st[0], st[1], st[2], st[3], st[4], st[5],
            st[6], st[7], st[8], st[9])
```

```python
import functools
import math

import jax
import jax.numpy as jnp
from jax import lax
from jax.experimental import pallas as pl
from jax.experimental.pallas import tpu as pltpu

F32 = jnp.float32
BF16 = jnp.bfloat16

LANES = 128
HEAD = 64
RW_WIDTH = 512
RW_COLS = 3 * RW_WIDTH + 64 + 64 + 128
DIFF_WIDTH = 512
N_DIFF_HEADS = 4
MEM_WIDTH = 512
N_MEM_HEADS = 4
PAGE = 128
ROPE_THETA = 10000.0
RMS_EPS = 1e-6
LNX_EPS = 1e-5 * HEAD
NEG_BIG = -1e30
VMEM_LIMIT = 56 * 1024 * 1024
RWKV_CHUNK = 64
RWKV_TERMS = 1
RWKV_CHUNKS_PER_TRIP = 4
QUERY_PANEL = 256
ONES_ROWS = 16
LOG2E = 1.4426950408889634


def _cparams(sem):
    return pltpu.CompilerParams(dimension_semantics=sem, vmem_limit_bytes=VMEM_LIMIT)


def _const_spec(shape):
    nd = len(shape)
    return pl.BlockSpec(shape, lambda *_: (0,) * nd)


def _nt_dot(a, b):
    return lax.dot_general(a, b, (((1,), (1,)), ((), ())), preferred_element_type=F32)


def _rms_rows(x, g):
    ms = jnp.mean(x * x, axis=-1, keepdims=True)
    return x * lax.rsqrt(ms + RMS_EPS) * g


def _lane_iota(shape):
    return lax.broadcasted_iota(jnp.int32, shape, len(shape) - 1)


def _rms_half(xh, gain):
    lo = _lane_iota(xh.shape) < HEAD
    x2 = xh * xh
    s0 = jnp.sum(jnp.where(lo, x2, 0.0), axis=-1, keepdims=True)
    s1 = jnp.sum(jnp.where(lo, 0.0, x2), axis=-1, keepdims=True)
    inv = jnp.where(lo, lax.rsqrt(s0 * (1.0 / HEAD) + RMS_EPS), lax.rsqrt(s1 * (1.0 / HEAD) + RMS_EPS))
    return xh * inv * gain


def _rope_block(xh, cos, sin_signed):
    first = (_lane_iota(xh.shape) & 32) == 0
    partner = jnp.where(first, pltpu.roll(xh, LANES - 32, 1), pltpu.roll(xh, 32, 1))
    return xh * cos + partner * sin_signed


def _rope_tables(pos):
    half = HEAD // 2
    inv = ROPE_THETA ** (-jnp.arange(half, dtype=F32) / half)
    ang = pos.astype(F32)[:, None] * inv[None, :]
    cos, sin = jnp.cos(ang), jnp.sin(ang)
    return jnp.tile(cos, (1, 4)), jnp.concatenate([-sin, sin, -sin, sin], axis=1)


def _split_hi_lo(x):
    hi = x.astype(BF16)
    lo = (x - hi.astype(F32)).astype(BF16)
    return jnp.concatenate([hi, lo], axis=1)


def _group_ones2():
    r = lax.broadcasted_iota(jnp.int32, (2 * LANES, LANES), 0)
    c = lax.broadcasted_iota(jnp.int32, (2 * LANES, LANES), 1)
    return jnp.where(((r & (LANES - 1)) >> 6) == (c >> 6), 1.0, 0.0).astype(BF16)


def _group_sum(x, g2):
    return jnp.dot(_split_hi_lo(x), g2, preferred_element_type=F32)


def _group_sum_wide(x, g2):
    return jnp.concatenate(
        [_group_sum(x[:, c * LANES:(c + 1) * LANES], g2) for c in range(x.shape[1] // LANES)], axis=1)


def _sigmoid(x):
    return 1.0 / (1.0 + jnp.exp(-x))


def _head_rows(n_rows, h, n_heads):
    return pl.ds(h, n_rows, stride=n_heads)


def _in_proj_kernel(x_ref, g_ref, w_ref, qn_ref, kn_ref, cos_ref, sin_ref,
                    prw_ref, qb_ref, k_ref, v_ref, *flash_refs):
    tm = x_ref.shape[0]
    sub = min(tm, LANES)
    qn, kn = qn_ref[...], kn_ref[...]
    for r0 in range(0, tm, sub):
        rows = slice(r0, r0 + sub)
        h = _rms_rows(x_ref[rows, :], g_ref[...]).astype(BF16)
        pd = jnp.dot(h, w_ref[:, RW_COLS:], preferred_element_type=F32)
        prw_ref[rows, :] = jnp.dot(h, w_ref[:, :RW_COLS], preferred_element_type=F32)
        cos, sin = cos_ref[rows, :], sin_ref[rows, :]
        for hh in range(N_DIFF_HEADS):
            sl = slice(hh * LANES, (hh + 1) * LANES)
            q = _rope_block(_rms_half(pd[:, sl], qn), cos, sin) * (HEAD ** -0.5 * LOG2E)
            k = _rope_block(_rms_half(pd[:, DIFF_WIDTH + hh * LANES:DIFF_WIDTH + (hh + 1) * LANES], kn), cos, sin)
            v = pd[:, 2 * DIFF_WIDTH + hh * LANES:2 * DIFF_WIDTH + (hh + 1) * LANES]
            head_rows = pl.ds(r0 * N_DIFF_HEADS + hh, sub, stride=N_DIFF_HEADS)
            qb_ref[rows, sl] = q.astype(BF16)
            k_ref[head_rows, :] = k
            v_ref[head_rows, :] = v
            if flash_refs:
                kb_ref, vt_ref = flash_refs
                kb_ref[rows, sl] = k.astype(BF16)
                vt_ref[0, sl, rows] = jnp.transpose(v).astype(BF16)


def _in_proj(x, g, w_bf, qn, kn, cos, sin, tm, seq_len=None):
    n, d = x.shape
    period = cos.shape[0] // tm
    row = lambda i: (i, 0)
    outs = [jax.ShapeDtypeStruct((n, RW_COLS), F32), jax.ShapeDtypeStruct((n, DIFF_WIDTH), BF16),
            jax.ShapeDtypeStruct((n * N_DIFF_HEADS, LANES), F32), jax.ShapeDtypeStruct((n * N_DIFF_HEADS, LANES), F32)]
    out_specs = [pl.BlockSpec((tm, RW_COLS), row), pl.BlockSpec((tm, DIFF_WIDTH), row),
                 pl.BlockSpec((tm * N_DIFF_HEADS, LANES), row), pl.BlockSpec((tm * N_DIFF_HEADS, LANES), row)]
    if seq_len is not None:
        per_seq = seq_len // tm
        outs += [jax.ShapeDtypeStruct((n, DIFF_WIDTH), BF16), jax.ShapeDtypeStruct((n // seq_len, DIFF_WIDTH, seq_len), BF16)]
        out_specs += [pl.BlockSpec((tm, DIFF_WIDTH), row),
                      pl.BlockSpec((1, DIFF_WIDTH, tm), lambda i: (i // per_seq, 0, i % per_seq))]
    return pl.pallas_call(
        _in_proj_kernel,
        grid=(n // tm,),
        in_specs=[pl.BlockSpec((tm, d), row), _const_spec((1, d)), _const_spec(w_bf.shape),
                  _const_spec((1, LANES)), _const_spec((1, LANES)),
                  pl.BlockSpec((tm, LANES), lambda i: (i % period, 0)),
                  pl.BlockSpec((tm, LANES), lambda i: (i % period, 0))],
        out_specs=tuple(out_specs),
        out_shape=tuple(outs),
        compiler_params=_cparams(("parallel",)),
        name="in_proj",
    )(x, g, w_bf, qn, kn, cos, sin)


def _rwkv_token_vectors(p, shifted, prm, g2):
    mu_ref, w0_ref, w2_ref, a0_ref, a2_ref, g2_ref, kk_ref, ka_ref, rk_ref = prm
    ps = p + (shifted - p) * mu_ref[...]
    r = ps[:, :RW_WIDTH]
    k = ps[:, RW_WIDTH:2 * RW_WIDTH]
    v = ps[:, 2 * RW_WIDTH:3 * RW_WIDTH]
    wa = ps[:, 3 * RW_WIDTH:3 * RW_WIDTH + LANES]
    gd = ps[:, 3 * RW_WIDTH + LANES:]
    z = -(w0_ref[...] + jnp.dot(jnp.tanh(wa).astype(BF16), w2_ref[...], preferred_element_type=F32))
    w_log = -(jnp.maximum(z, 0.0) + jnp.log1p(jnp.exp(-jnp.abs(z)))) - 0.5
    a = _sigmoid(a0_ref[...] + jnp.dot(wa.astype(BF16), a2_ref[...], preferred_element_type=F32))
    gate = jnp.dot(_sigmoid(gd).astype(BF16), g2_ref[...], preferred_element_type=F32)
    kk = k * kk_ref[...]
    kk = kk / jnp.maximum(jnp.sqrt(_group_sum_wide(kk * kk, g2)), 1e-12)
    k2 = k * (1.0 + (a - 1.0) * ka_ref[...])
    bonus = _group_sum_wide(r * k2 * rk_ref[...], g2) * v
    return r, -jnp.exp(w_log), k2, v, -kk, kk * a, bonus, gate


def _rwkv_finish(y, bonus, gate, lw, lb, g2):
    mean = _group_sum_wide(y, g2) * (1.0 / HEAD)
    d = y - mean
    var = _group_sum_wide(d * d, g2) * (1.0 / HEAD)
    return (d * lax.rsqrt(var + LNX_EPS) * lw + lb + bonus) * gate


def _mm(x, w, terms):
    if terms == 1:
        return jnp.dot(x.astype(BF16), w.astype(BF16), preferred_element_type=F32)
    w_hi = w.astype(BF16)
    out = jnp.dot(_split_hi_lo(x), jnp.concatenate([w_hi, w_hi], axis=0), preferred_element_type=F32)
    if terms == 3:
        w_lo = (w - w_hi.astype(F32)).astype(BF16)
        out = out + jnp.dot(x.astype(BF16), w_lo, preferred_element_type=F32)
    return out


def _stack_heads(x):
    lo = _lane_iota(x.shape) < HEAD
    return jnp.concatenate([jnp.where(lo, x, 0.0), jnp.where(lo, 0.0, x)], axis=0)


def _nt_mm(x, w, terms):
    if terms == 1:
        return _nt_dot(x.astype(BF16), w.astype(BF16))
    w_hi = w.astype(BF16)
    out = _nt_dot(_split_hi_lo(x), jnp.concatenate([w_hi, w_hi], axis=1))
    if terms == 3:
        out = out + _nt_dot(x.astype(BF16), (w - w_hi.astype(F32)).astype(BF16))
    return out


def _chunk_units(units, slots, zs, terms):
    n = len(units)
    c2 = 2 * units[0][0].shape[0]
    ti = lax.broadcasted_iota(jnp.int32, (c2, c2), 0)
    tj = lax.broadcasted_iota(jnp.int32, (c2, c2), 1)
    eye = ti == tj
    strict = (ti & (c2 // 2 - 1)) > (tj & (c2 // 2 - 1))
    incl = (ti & (c2 // 2 - 1)) >= (tj & (c2 // 2 - 1))
    s_a = [_stack_heads(u[0]) for u in units]
    s_r = [_stack_heads(u[1]) for u in units]
    s_v = [_stack_heads(u[6]) for u in units]
    sc = [_nt_mm(jnp.concatenate([s_a[i], s_r[i]], axis=0),
                 jnp.concatenate([_stack_heads(units[i][2]), _stack_heads(units[i][3])], axis=0), terms)
          for i in range(n)]
    x = [jnp.where(strict, s[:c2, :c2], 0.0) for s in sc]
    l_ak = [jnp.where(strict, s[:c2, c2:], 0.0) for s in sc]
    m_rbk = [jnp.concatenate([jnp.where(incl, s[c2:, :c2], 0.0), jnp.where(incl, s[c2:, c2:], 0.0)], axis=1) for s in sc]
    t_inv = [jnp.where(eye, 1.0, 0.0) + xi for xi in x]
    lv = [_mm(l_ak[i], s_v[i], terms) for i in range(n)]
    for _ in range(max(1, (c2 // 2 - 1).bit_length()) - 1):
        x = [_mm(xi, xi, terms) for xi in x]
        t_inv = [t_inv[i] + _mm(t_inv[i], x[i], terms) for i in range(n)]
    pq = [_mm(t_inv[i], jnp.concatenate([s_a[i], lv[i]], axis=1), terms) for i in range(n)]
    pqv = [jnp.concatenate([pq[i], jnp.concatenate([jnp.zeros_like(s_v[i]), s_v[i]], axis=1)], axis=0) for i in range(n)]
    ah = [_mm(jnp.transpose(jnp.concatenate([_stack_heads(units[i][4]), _stack_heads(units[i][5])], axis=0)), pqv[i], terms)
          for i in range(n)]
    ry = [_mm(m_rbk[i], pqv[i], terms) for i in range(n)]
    zs, ys = list(zs), []
    for i in range(n):
        z = zs[slots[i]]
        y = _mm(s_r[i] + ry[i][:, :c2], z, 3) + ry[i][:, c2:]
        ys.append(y[:c2 // 2] + y[c2 // 2:])
        zs[slots[i]] = _mm(jnp.where(eye, units[i][7], 0.0) + ah[i][:, :c2], z, 3) + ah[i][:, c2:]
    return ys, zs


def _rwkv_kernel(p_ref, prev_ref, s0_ref, mu_ref, w0_ref, w2_ref, a0_ref, a2_ref, g2_ref,
                 kk_ref, ka_ref, rk_ref, lw_ref, lb_ref,
                 out_ref, st_ref,
                 state, carry, r_s, w_s, k_s, v_s, a_s, b_s, y_s, bonus_s, gate_s, *, ns, tt):
    ti = pl.program_id(1)
    npair = RW_WIDTH // LANES
    g2 = _group_ones2()

    @pl.when(ti == 0)
    def _():
        for s in range(ns):
            for j in range(npair):
                state[(s * npair + j) * HEAD:(s * npair + j + 1) * HEAD, :] = jnp.concatenate(
                    [s0_ref[s, 2 * j], s0_ref[s, 2 * j + 1]], axis=1)
        carry[...] = prev_ref[...]

    row0 = lax.broadcasted_iota(jnp.int32, (tt, RW_COLS), 0) == 0
    prm = (mu_ref, w0_ref, w2_ref, a0_ref, a2_ref, g2_ref, kk_ref, ka_ref, rk_ref)
    for s in range(ns):
        p = p_ref[s]
        shifted = jnp.where(row0, carry[s], pltpu.roll(p, 1, 0))
        carry[s] = p[tt - 1:tt, :]
        r_s[s], log_w, k_s[s], v_s[s], a_s[s], b_s[s], bonus_s[s], gate_s[s] = _rwkv_token_vectors(p, shifted, prm, g2)
        w_s[s] = jnp.exp(log_w)

    tiles = [(s, j) for s in range(ns) for j in range(npair)]
    nt = len(tiles)
    sub = lax.broadcasted_iota(jnp.int32, (nt * HEAD, LANES), 0)
    e1 = jnp.where((_lane_iota((nt * HEAD, LANES)) & (HEAD - 1)) == (sub & (HEAD - 1)), 1.0, 0.0)
    seqs = dict(r=r_s, w=w_s, k=k_s, v=v_s, a=a_s, b=b_s)

    def block(tb, c):
        base = pl.multiple_of(tb * 8, 8)
        rows8 = {q: [ref[s, pl.ds(base, 8), j * LANES:(j + 1) * LANES] for (s, j) in tiles] for q, ref in seqs.items()}
        ys = []
        for i in range(8):
            bc = lambda q: jnp.concatenate(
                [jnp.broadcast_to(rows8[q][n][i:i + 1], (HEAD, LANES)) for n in range(nt)], axis=0)
            st = state[...]
            sa = jnp.dot(_split_hi_lo(st * bc("a")), g2, preferred_element_type=F32)
            vcol = jnp.dot((e1 * bc("v")).astype(BF16), g2[:LANES], preferred_element_type=F32)
            st = st * bc("w") + sa * bc("b") + vcol * bc("k")
            state[...] = st
            yred = jnp.dot((st * bc("r")).astype(BF16), g2[:LANES], preferred_element_type=F32)
            ys.append(jnp.sum((e1 * yred).reshape(nt, HEAD, LANES), axis=1))
        for n, (s, j) in enumerate(tiles):
            y_s[s, pl.ds(base, 8), j * LANES:(j + 1) * LANES] = jnp.concatenate([y[n:n + 1] for y in ys], axis=0)
        return c

    lax.fori_loop(0, tt // 8, block, 0)

    for s in range(ns):
        out_ref[s] = _rwkv_finish(y_s[s], bonus_s[s], gate_s[s], lw_ref[...], lb_ref[...], g2)

    @pl.when(ti == pl.num_programs(1) - 1)
    def _():
        for s in range(ns):
            for j in range(npair):
                tile = state[(s * npair + j) * HEAD:(s * npair + j + 1) * HEAD, :]
                st_ref[s, 2 * j] = tile[:, :HEAD]
                st_ref[s, 2 * j + 1] = tile[:, HEAD:]


def _rwkv_chunk_kernel(p_ref, prev_ref, z0_ref, mu_ref, w0_ref, w2_ref, a0_ref, a2_ref, g2_ref,
                       kk_ref, ka_ref, rk_ref, lw_ref, lb_ref,
                       out_ref, zt_ref,
                       zstate, carry, r_s, g_s, k_s, v_s, a_s, b_s, y_s, bonus_s, gate_s, *, tt, chunk, cpt, terms):
    ti = pl.program_id(1)
    npair = RW_WIDTH // LANES
    g2 = _group_ones2()

    @pl.when(ti == 0)
    def _():
        zstate[...] = z0_ref[0]
        carry[...] = prev_ref[0]

    p = p_ref[0]
    row0 = lax.broadcasted_iota(jnp.int32, (tt, RW_COLS), 0) == 0
    shifted = jnp.where(row0, carry[...], pltpu.roll(p, 1, 0))
    carry[...] = p[tt - 1:tt, :]
    prm = (mu_ref, w0_ref, w2_ref, a0_ref, a2_ref, g2_ref, kk_ref, ka_ref, rk_ref)
    (r_s[...], g_s[...], k_s[...], v_s[...], a_s[...], b_s[...], bonus_s[...], gate_s[...]) = _rwkv_token_vectors(
        p, shifted, prm, g2)

    ci = lax.broadcasted_iota(jnp.int32, (chunk, 3 * chunk), 0)
    cj = lax.broadcasted_iota(jnp.int32, (chunk, 3 * chunk), 1)
    cj = jnp.where(cj >= 2 * chunk, cj - 2 * chunk, jnp.where(cj >= chunk, cj - chunk, cj))
    ltri3 = jnp.where(cj <= ci, 1.0, 0.0).astype(BF16)

    lanes = [slice(j * LANES, (j + 1) * LANES) for j in range(npair)]

    def scaled(c):
        rows = pl.ds(pl.multiple_of(c * chunk, chunk), chunk)
        g = g_s[rows, :]
        g_hi = g.astype(BF16)
        g_r = g - g_hi.astype(F32)
        g_mid = g_r.astype(BF16)
        g_lo = (g_r - g_mid.astype(F32)).astype(BF16)
        cum = jnp.dot(ltri3, jnp.concatenate([g_hi, g_mid, g_lo], axis=0), preferred_element_type=F32)
        total = cum[chunk - 1:chunk, :]
        e_in, e_ex, e_inv, e_rest = jnp.exp(cum), jnp.exp(cum - g), jnp.exp(-cum), jnp.exp(total - cum)
        a_t, r_t = a_s[rows, :] * e_ex, r_s[rows, :] * e_in
        b, k, v = b_s[rows, :], k_s[rows, :], v_s[rows, :]
        ops = (a_t, r_t, b * e_inv, k * e_inv, b * e_rest, k * e_rest, v, jnp.exp(total))
        return rows, [tuple(t[:, sl] for t in ops) for sl in lanes]

    def body(trip, carry_):
        parts = [scaled(trip * cpt + c) for c in range(cpt)]
        ys, z_new = _chunk_units([u for _, units in parts for u in units], list(range(npair)) * cpt,
                                 [zstate[sl, :] for sl in lanes], terms)
        for c, (rows, _) in enumerate(parts):
            for j, sl in enumerate(lanes):
                y_s[rows, sl] = ys[c * npair + j]
        for j, sl in enumerate(lanes):
            zstate[sl, :] = z_new[j]
        return carry_

    lax.fori_loop(0, tt // (cpt * chunk), body, 0)
    out_ref[0] = _rwkv_finish(y_s[...], bonus_s[...], gate_s[...], lw_ref[...], lb_ref[...], g2)

    @pl.when(ti == pl.num_programs(1) - 1)
    def _():
        zt_ref[0] = zstate[...]


def _rwkv(prw, prev, s0, rw, ns, tt):
    S, T, _ = prw.shape
    npair = RW_WIDTH // LANES
    vec = lambda n: _const_spec((1, n))
    blk = lambda w: pl.BlockSpec((ns, tt, w), lambda i, t: (i, t, 0))
    st_spec = pl.BlockSpec((ns, 2 * npair, HEAD, HEAD), lambda i, t: (i, 0, 0, 0))
    seq = lambda: pltpu.VMEM((ns, tt, RW_WIDTH), F32)
    return pl.pallas_call(
        functools.partial(_rwkv_kernel, ns=ns, tt=tt),
        grid=(S // ns, T // tt),
        in_specs=[blk(RW_COLS), pl.BlockSpec((ns, 1, RW_COLS), lambda i, t: (i, 0, 0)), st_spec,
                  vec(RW_COLS), vec(RW_WIDTH), _const_spec((LANES, RW_WIDTH)), vec(RW_WIDTH),
                  _const_spec((LANES, RW_WIDTH)), _const_spec((LANES, RW_WIDTH)),
                  vec(RW_WIDTH), vec(RW_WIDTH), vec(RW_WIDTH), vec(RW_WIDTH), vec(RW_WIDTH)],
        out_specs=(blk(RW_WIDTH), st_spec),
        out_shape=(jax.ShapeDtypeStruct((S, T, RW_WIDTH), F32),
                   jax.ShapeDtypeStruct((S, 2 * npair, HEAD, HEAD), F32)),
        scratch_shapes=[pltpu.VMEM((ns * npair * HEAD, LANES), F32), pltpu.VMEM((ns, 1, RW_COLS), F32)]
                       + [seq() for _ in range(9)],
        compiler_params=_cparams(("parallel", "arbitrary")),
        name="rwkv",
    )(prw, prev, s0, *rw)


def _rwkv_chunked(prw, prev, z0, rw, tt, chunk, terms):
    S, T, _ = prw.shape
    vec = lambda n: _const_spec((1, n))
    blk = lambda w: pl.BlockSpec((1, tt, w), lambda i, t: (i, t, 0))
    z_spec = pl.BlockSpec((1, RW_WIDTH, LANES), lambda i, t: (i, 0, 0))
    seq = lambda: pltpu.VMEM((tt, RW_WIDTH), F32)
    return pl.pallas_call(
        functools.partial(_rwkv_chunk_kernel, tt=tt, chunk=chunk, cpt=_tile(tt // chunk, RWKV_CHUNKS_PER_TRIP),
                          terms=terms),
        grid=(S, T // tt),
        in_specs=[blk(RW_COLS), pl.BlockSpec((1, 1, RW_COLS), lambda i, t: (i, 0, 0)), z_spec,
                  vec(RW_COLS), vec(RW_WIDTH), _const_spec((LANES, RW_WIDTH)), vec(RW_WIDTH),
                  _const_spec((LANES, RW_WIDTH)), _const_spec((LANES, RW_WIDTH)),
                  vec(RW_WIDTH), vec(RW_WIDTH), vec(RW_WIDTH), vec(RW_WIDTH), vec(RW_WIDTH)],
        out_specs=(blk(RW_WIDTH), z_spec),
        out_shape=(jax.ShapeDtypeStruct((S, T, RW_WIDTH), F32), jax.ShapeDtypeStruct((S, RW_WIDTH, LANES), F32)),
        scratch_shapes=[pltpu.VMEM((RW_WIDTH, LANES), F32), pltpu.VMEM((1, RW_COLS), F32)] + [seq() for _ in range(9)],
        compiler_params=_cparams(("parallel", "arbitrary")),
        name="rwkv_chunked",
    )(prw, prev, z0, *rw)


def _state_from_blockdiag(z, S):
    z = z.reshape(S, 4, 2, HEAD, 2, HEAD)
    z = jnp.stack([z[:, :, 0, :, 0, :], z[:, :, 1, :, 1, :]], axis=2)
    return z.reshape(S, 8, HEAD, HEAD).transpose(0, 1, 3, 2)


def _lambda(lamp_ref, lam_init):
    lp = lamp_ref[...]
    s1 = jnp.sum(lp[0:1] * lp[1:2], axis=-1, keepdims=True)
    s2 = jnp.sum(lp[2:3] * lp[3:4], axis=-1, keepdims=True)
    return jnp.exp(s1) - jnp.exp(s2) + lam_init


def _stack_halves(q):
    lo = _lane_iota(q.shape) < HEAD
    zero = jnp.zeros_like(q)
    return jnp.concatenate([jnp.where(lo, q, zero), jnp.where(lo, zero, q)], axis=0)


def _diff_finish(acc, l, lam, sn, lam_init, rows):
    o = acc[:rows] / l[:rows] - lam * (acc[rows:] / l[rows:])
    return _rms_rows(o, sn) * (1.0 - lam_init)


def _diff_prompt_kernel(q_ref, k_ref, vt_ref, lamp_ref, sn_ref, o_ref, m_ref, acc_ref, *, tq, hp, lam_init):
    i = pl.program_id(2)
    heads = [slice(h * LANES, (h + 1) * LANES) for h in range(hp)]
    q2 = [_stack_halves(q_ref[0, :, sl]) for sl in heads]
    ones = jnp.ones((ONES_ROWS, tq), BF16)
    panel = min(tq, QUERY_PANEL)
    units = [(h, slice(c, c + panel)) for h in range(hp) for c in range(0, 2 * tq, panel)]

    def process(off, diagonal):
        keys = [k_ref[0, pl.ds(off, tq), sl] for sl in heads]
        vals = [jnp.concatenate([vt_ref[0, sl, pl.ds(off, tq)], ones], axis=0) for sl in heads]
        first_q = [pan.start % tq for _, pan in units]
        n_keys = [min(tq, fq + panel) if diagonal else tq for fq in first_q]
        s = [_nt_dot(keys[h][:n_keys[u]], q2[h][pan]) for u, (h, pan) in enumerate(units)]
        if diagonal:
            for u in range(len(units)):
                key = lax.broadcasted_iota(jnp.int32, s[u].shape, 0)
                qry = lax.broadcasted_iota(jnp.int32, s[u].shape, 1) + first_q[u]
                s[u] = jnp.where(key <= qry, s[u], NEG_BIG)
        m_old = [m_ref[h, :, pan] for h, pan in units]
        m_new = [jnp.maximum(m_old[u], jnp.max(s[u], axis=0, keepdims=True)) for u in range(len(units))]
        alpha = [jnp.exp2(m_old[u] - m_new[u]) for u in range(len(units))]
        p = [jnp.exp2(s[u] - m_new[u]) for u in range(len(units))]
        pv = [jnp.dot(vals[h][:, :n_keys[u]], p[u].astype(BF16), preferred_element_type=F32)
              for u, (h, _) in enumerate(units)]
        for u, (h, pan) in enumerate(units):
            acc_ref[h, :, pan] = alpha[u] * acc_ref[h, :, pan] + pv[u]
            m_ref[h, :, pan] = m_new[u]

    m_ref[...] = jnp.full(m_ref.shape, NEG_BIG, F32)
    acc_ref[...] = jnp.zeros(acc_ref.shape, F32)
    process(pl.multiple_of(i * tq, tq), True)

    def body(j, c):
        process(pl.multiple_of(j * tq, tq), False)
        return c

    lax.fori_loop(0, i, body, 0)
    lam = _lambda(lamp_ref, lam_init)
    for h, sl in enumerate(heads):
        acc = acc_ref[h]
        acc = acc[:LANES] / acc[LANES:LANES + 1]
        o_t = acc[:, :tq] - lam * acc[:, tq:]
        ms = jnp.mean(o_t * o_t, axis=0, keepdims=True)
        o_t = o_t * lax.rsqrt(ms + RMS_EPS)
        o_ref[0, :, sl] = jnp.transpose(o_t) * sn_ref[...] * (1.0 - lam_init)


def _diff_prompt(qb, kb, vt, lamp, sn, tq, hp, lam_init):
    B, T, _ = qb.shape
    tile = pl.BlockSpec((1, tq, hp * LANES), lambda b, h, i: (b, i, h))
    return pl.pallas_call(
        functools.partial(_diff_prompt_kernel, tq=tq, hp=hp, lam_init=lam_init),
        grid=(B, N_DIFF_HEADS // hp, T // tq),
        in_specs=[tile, pl.BlockSpec((1, T, hp * LANES), lambda b, h, i: (b, 0, h)),
                  pl.BlockSpec((1, hp * LANES, T), lambda b, h, i: (b, h, 0)),
                  _const_spec((4, HEAD)), _const_spec((1, LANES))],
        out_specs=tile,
        out_shape=jax.ShapeDtypeStruct((B, T, DIFF_WIDTH), F32),
        scratch_shapes=[pltpu.VMEM((hp, 1, 2 * tq), F32), pltpu.VMEM((hp, LANES + ONES_ROWS, 2 * tq), F32)],
        compiler_params=_cparams(("parallel", "parallel", "arbitrary")),
        name="diff_prompt",
    )(qb, kb, vt, lamp, sn)


def _diff_sample_kernel(pt_ref, q_ref, kn_ref, vn_ref, lamp_ref, sn_ref, *rest, pp, s_new, lam_init):
    del pt_ref
    kpages, vpages = rest[:pp], rest[pp:2 * pp]
    o_ref, m_ref, l_ref, acc_ref = rest[2 * pp:]
    j = pl.program_id(1)
    nh = N_DIFF_HEADS
    rows = 2 * s_new
    q = q_ref[0]
    q2 = [_stack_halves(q[:, h * LANES:(h + 1) * LANES]) for h in range(nh)]

    def update(scores, values):
        s = jnp.concatenate([jnp.concatenate(sc, axis=1) if len(sc) > 1 else sc[0] for sc in scores], axis=0)
        m_old = m_ref[...]
        m_new = jnp.maximum(m_old, jnp.max(s, axis=-1, keepdims=True))
        alpha = jnp.exp2(m_old - m_new)
        p = jnp.exp2(s - m_new)
        pv = []
        for h in range(nh):
            ph = p[h * rows:(h + 1) * rows].astype(BF16)
            acc = jnp.dot(ph[:, :PAGE], values[h][0], preferred_element_type=F32)
            for c in range(1, len(values[h])):
                acc = acc + jnp.dot(ph[:, c * PAGE:(c + 1) * PAGE], values[h][c], preferred_element_type=F32)
            pv.append(acc)
        l_ref[...] = alpha * l_ref[...] + jnp.sum(p, axis=-1, keepdims=True)
        acc_ref[...] = alpha * acc_ref[...] + jnp.concatenate(pv, axis=0)
        m_ref[...] = m_new

    @pl.when(j == 0)
    def _():
        m_ref[...] = jnp.full(m_ref.shape, NEG_BIG, F32)
        l_ref[...] = jnp.zeros(l_ref.shape, F32)
        acc_ref[...] = jnp.zeros(acc_ref.shape, F32)
        pad = jnp.zeros((PAGE - s_new, LANES), F32)
        row = lax.broadcasted_iota(jnp.int32, (rows, PAGE), 0)
        col = lax.broadcasted_iota(jnp.int32, (rows, PAGE), 1)
        mask = col <= jnp.where(row >= s_new, row - s_new, row)
        new_rows = lambda ref, h: jnp.concatenate([ref[0, _head_rows(s_new, h, nh), :], pad], axis=0).astype(BF16)
        update([[jnp.where(mask, _nt_dot(q2[h], new_rows(kn_ref, h)), NEG_BIG)] for h in range(nh)],
               [[new_rows(vn_ref, h)] for h in range(nh)])

    page_rows = lambda ref, h: ref[_head_rows(PAGE, h, nh), :].astype(BF16)
    update([[_nt_dot(q2[h], page_rows(kp, h)) for kp in kpages] for h in range(nh)],
           [[page_rows(vp, h) for vp in vpages] for h in range(nh)])

    @pl.when(j == pl.num_programs(1) - 1)
    def _():
        lam = _lambda(lamp_ref, lam_init)
        acc, l = acc_ref[...], l_ref[...]
        for h in range(nh):
            o_ref[0, :, h * LANES:(h + 1) * LANES] = _diff_finish(
                acc[h * rows:(h + 1) * rows], l[h * rows:(h + 1) * rows], lam, sn_ref[...], lam_init, s_new)


def _diff_sample(page_table, qb, k_new, v_new, pool_k, pool_v, lamp, sn, pp, lam_init):
    Bs, S, _ = qb.shape
    n_pages = page_table.shape[1]
    nh = N_DIFF_HEADS
    qspec = pl.BlockSpec((1, S, DIFF_WIDTH), lambda b, j, pt: (b, 0, 0))
    new = pl.BlockSpec((1, S * nh, LANES), lambda b, j, pt: (b, 0, 0))
    page = lambda c: pl.BlockSpec((None, PAGE * nh, LANES), lambda b, j, pt, c=c: (pt[b, j * pp + c], 0, 0))
    cst = lambda shape: pl.BlockSpec(shape, lambda b, j, pt: (0,) * len(shape))
    return pl.pallas_call(
        functools.partial(_diff_sample_kernel, pp=pp, s_new=S, lam_init=lam_init),
        grid_spec=pltpu.PrefetchScalarGridSpec(
            num_scalar_prefetch=1,
            grid=(Bs, n_pages // pp),
            in_specs=[qspec, new, new, cst((4, HEAD)), cst((1, LANES))] + [page(c) for c in range(pp)] * 2,
            out_specs=qspec,
            scratch_shapes=[pltpu.VMEM((nh * 2 * S, 1), F32), pltpu.VMEM((nh * 2 * S, 1), F32),
                            pltpu.VMEM((nh * 2 * S, LANES), F32)]),
        out_shape=jax.ShapeDtypeStruct((Bs, S, DIFF_WIDTH), F32),
        compiler_params=_cparams(("parallel", "arbitrary")),
        name="diff_sample",
    )(page_table, qb, k_new, v_new, lamp, sn, *([pool_k] * pp), *([pool_v] * pp))


def _mem_kv_kernel(x_ref, g_ref, w_ref, kn_ref, k_ref, v_ref):
    h = _rms_rows(x_ref[...], g_ref[...]).astype(BF16)
    kv = jnp.dot(h, w_ref[...], preferred_element_type=F32)
    tm = x_ref.shape[0]
    for hh in range(N_MEM_HEADS):
        sl = slice(hh * LANES, (hh + 1) * LANES)
        k_ref[_head_rows(tm, hh, N_MEM_HEADS), :] = _rms_rows(kv[:, sl], kn_ref[...])
        v_ref[_head_rows(tm, hh, N_MEM_HEADS), :] = kv[:, MEM_WIDTH + hh * LANES:MEM_WIDTH + (hh + 1) * LANES]


def _mem_kv(mem, g, w_bf, kn, tm):
    n, d = mem.shape
    row = lambda i: (i, 0)
    return pl.pallas_call(
        _mem_kv_kernel,
        grid=(n // tm,),
        in_specs=[pl.BlockSpec((tm, d), row), _const_spec((1, d)), _const_spec(w_bf.shape), _const_spec((1, LANES))],
        out_specs=(pl.BlockSpec((tm * N_MEM_HEADS, LANES), row),) * 2,
        out_shape=(jax.ShapeDtypeStruct((n * N_MEM_HEADS, LANES), F32),) * 2,
        compiler_params=_cparams(("parallel",)),
        name="mem_kv",
    )(mem, g, w_bf, kn)


def _mem_attn_kernel(x_ref, r_ref, a_ref, wmix_ref, mk_ref, mv_ref, g_ref, wq_ref, qn_ref, wo_ref, y_ref, o_scr,
                     *, nb, tr, n_mem):
    d = x_ref.shape[-1]
    rows = nb * tr
    x = x_ref[...].reshape(rows, d)
    x = x + jnp.dot(r_ref[...].reshape(rows, RW_WIDTH).astype(BF16), wmix_ref[:RW_WIDTH, :], preferred_element_type=F32)
    x = x + jnp.dot(a_ref[...].reshape(rows, DIFF_WIDTH).astype(BF16), wmix_ref[RW_WIDTH:, :], preferred_element_type=F32)
    h = _rms_rows(x, g_ref[...]).astype(BF16)
    q = jnp.dot(h, wq_ref[...], preferred_element_type=F32)
    heads = [slice(hh * LANES, (hh + 1) * LANES) for hh in range(N_MEM_HEADS)]
    qh = [(_rms_rows(q[:, sl], qn_ref[...]) * (LANES ** -0.5 * LOG2E)).astype(BF16) for sl in heads]
    units = [(hh, n) for hh in range(N_MEM_HEADS) for n in range(nb)]
    mem_rows = lambda ref, hh, n: ref[n, _head_rows(n_mem, hh, N_MEM_HEADS), :].astype(BF16)
    s = [_nt_dot(qh[hh][n * tr:(n + 1) * tr], mem_rows(mk_ref, hh, n)) for hh, n in units]
    p = [jnp.exp2(su - jnp.max(su, axis=-1, keepdims=True)) for su in s]
    o = [jnp.dot(p[u].astype(BF16), mem_rows(mv_ref, hh, n), preferred_element_type=F32) for u, (hh, n) in enumerate(units)]
    for u, (hh, n) in enumerate(units):
        o_scr[n * tr:(n + 1) * tr, heads[hh]] = o[u] / jnp.sum(p[u], axis=-1, keepdims=True)
    y = x + jnp.dot(o_scr[...].astype(BF16), wo_ref[...], preferred_element_type=F32)
    y_ref[...] = y.reshape(nb, tr, d)


def _mem_attn(x, r_out, attn, wmix_bf, mk, mv, g, wq_bf, qn, wo_bf, nb, tr):
    G, R, d = x.shape
    M4 = mk.shape[1]
    tile = lambda w: pl.BlockSpec((nb, tr, w), lambda i, t: (i, t, 0))
    mem = pl.BlockSpec((nb, M4, LANES), lambda i, t: (i, 0, 0))
    return pl.pallas_call(
        functools.partial(_mem_attn_kernel, nb=nb, tr=tr, n_mem=M4 // N_MEM_HEADS),
        grid=(G // nb, R // tr),
        in_specs=[tile(d), tile(RW_WIDTH), tile(DIFF_WIDTH), _const_spec(wmix_bf.shape), mem, mem,
                  _const_spec((1, d)), _const_spec(wq_bf.shape), _const_spec((1, LANES)), _const_spec(wo_bf.shape)],
        out_specs=tile(d),
        out_shape=jax.ShapeDtypeStruct((G, R, d), F32),
        scratch_shapes=[pltpu.VMEM((nb * tr, MEM_WIDTH), F32)],
        compiler_params=_cparams(("parallel", "arbitrary")),
        name="mem_attn",
    )(x, r_out, attn, wmix_bf, mk, mv, g, wq_bf, qn, wo_bf)


def _ffn_kernel(x_ref, g_ref, wgu_ref, wd_ref, y_ref, *, d_ff, fc):
    x = x_ref[...]
    h = _rms_rows(x, g_ref[...]).astype(BF16)
    y = x
    for c in range(d_ff // fc):
        gt = jnp.dot(h, wgu_ref[:, c * fc:(c + 1) * fc], preferred_element_type=F32)
        up = jnp.dot(h, wgu_ref[:, d_ff + c * fc:d_ff + (c + 1) * fc], preferred_element_type=F32)
        act = (gt * _sigmoid(gt) * up).astype(BF16)
        y = y + jnp.dot(act, wd_ref[c * fc:(c + 1) * fc, :], preferred_element_type=F32)
    y_ref[...] = y


def _ffn(x, g, wgu_bf, wd_bf, tm):
    n, d = x.shape
    d_ff = wd_bf.shape[0]
    fc = d_ff // 2 if (d_ff // 2) % LANES == 0 else d_ff
    row = lambda i: (i, 0)
    return pl.pallas_call(
        functools.partial(_ffn_kernel, d_ff=d_ff, fc=fc),
        grid=(n // tm,),
        in_specs=[pl.BlockSpec((tm, d), row), _const_spec((1, d)), _const_spec(wgu_bf.shape), _const_spec(wd_bf.shape)],
        out_specs=pl.BlockSpec((tm, d), row),
        out_shape=jax.ShapeDtypeStruct((n, d), F32),
        compiler_params=_cparams(("parallel",)),
        name="ffn",
    )(x, g, wgu_bf, wd_bf)


def _tile(n, pref):
    t = min(n, pref)
    while n % t:
        t //= 2
    return t


def kernel(x_prompt, x_sample, cache_diff_k, cache_diff_v, cache_mem_k, cache_mem_v, state_rwkv, state_shift, page_table, mem_prompt, norm_mix, w_in, rwkv_mu, rwkv_w0, rwkv_w2, rwkv_a0, rwkv_a2, rwkv_g2, rwkv_k_k, rwkv_k_a, rwkv_r_k, rwkv_lnx_w, rwkv_lnx_b, diff_q_norm, diff_k_norm, diff_lam_q1, diff_lam_k1, diff_lam_q2, diff_lam_k2, diff_sub_norm, w_o, norm_mem, norm_memkv, w_mq, w_mkv, mem_q_norm, mem_k_norm, w_mo, norm_ffn, w_gate_up, w_down):
    B, T, D = x_prompt.shape
    Bs, S, _ = x_sample.shape
    depth = w_in.shape[0]
    n_pages = page_table.shape[1]
    past = n_pages * PAGE
    M = mem_prompt.shape[1]
    xp = x_prompt.reshape(B * T, D)
    xs = x_sample.reshape(Bs * S, D)
    mem = mem_prompt.reshape(B * M, D)

    tm_p = _tile(T, 512)
    tm_wide = _tile(T, 512)
    tm_s = _tile(Bs * S, 256)
    cos_p, sin_p = _rope_tables(jnp.arange(T, dtype=jnp.int32))
    cos_s, sin_s = _rope_tables(past + jnp.arange(S, dtype=jnp.int32))
    cos_s, sin_s = jnp.tile(cos_s, (tm_s // S, 1)), jnp.tile(sin_s, (tm_s // S, 1))
    tq = _tile(T, 512)
    pp = _tile(n_pages, 32)
    ns_s = _tile(Bs, 4)
    nb_mem = _tile(Bs, 16)
    tt_p = _tile(T, 256)

    outs = [[] for _ in range(10)]
    for l in range(depth):
        lam_init = 0.8 - 0.6 * math.exp(-0.3 * l)
        row = lambda a: a[l].reshape(1, -1)
        bf = lambda a: a[l].astype(BF16)
        zpad = jnp.zeros((HEAD, RW_WIDTH), F32)
        rw = (row(rwkv_mu), row(rwkv_w0), jnp.concatenate([rwkv_w2[l], zpad], axis=0).astype(BF16), row(rwkv_a0),
              jnp.concatenate([zpad, rwkv_a2[l]], axis=0).astype(BF16), bf(rwkv_g2), row(rwkv_k_k), row(rwkv_k_a),
              row(rwkv_r_k), row(rwkv_lnx_w), row(rwkv_lnx_b))
        lamp = jnp.stack([diff_lam_q1[l], diff_lam_k1[l], diff_lam_q2[l], diff_lam_k2[l]])
        w_in_bf, w_o_bf, w_mq_bf, w_mkv_bf, w_mo_bf = bf(w_in), bf(w_o), bf(w_mq), bf(w_mkv), bf(w_mo)
        w_gu_bf, w_d_bf = bf(w_gate_up), bf(w_down)
        qn, kn, sn = row(diff_q_norm), row(diff_k_norm), row(diff_sub_norm)

        prw, qb, k, v, kb, vt = _in_proj(xp, row(norm_mix), w_in_bf, qn, kn, cos_p, sin_p, tm_p, seq_len=T)
        prw3 = prw.reshape(B, T, RW_COLS)
        r_out, z_p = _rwkv_chunked(prw3, jnp.zeros((B, 1, RW_COLS), F32), jnp.zeros((B, RW_WIDTH, LANES), F32), rw,
                                   tt_p, _tile(tt_p, RWKV_CHUNK), RWKV_TERMS)
        o = _diff_prompt(qb.reshape(B, T, -1), kb.reshape(B, T, -1), vt, lamp, sn, tq, 4, lam_init)
        mk, mv = _mem_kv(mem, row(norm_memkv), w_mkv_bf, row(mem_k_norm), _tile(M, 256))
        x2 = _mem_attn(xp.reshape(B, T, D), r_out, o, w_o_bf,
                       mk.reshape(B, M * N_MEM_HEADS, LANES), mv.reshape(B, M * N_MEM_HEADS, LANES),
                       row(norm_mem), w_mq_bf, row(mem_q_norm), w_mo_bf, 1, tm_wide)
        xp = _ffn(x2.reshape(B * T, D), row(norm_ffn), w_gu_bf, w_d_bf, tm_wide)
        outs[0].append(k.reshape(B, T, N_DIFF_HEADS, LANES))
        outs[1].append(v.reshape(B, T, N_DIFF_HEADS, LANES))
        outs[2].append(mk.reshape(B, M, N_MEM_HEADS, LANES))
        outs[3].append(mv.reshape(B, M, N_MEM_HEADS, LANES))
        outs[4].append(_state_from_blockdiag(z_p, B))
        outs[5].append(prw3[:, -1])

        prw, qb, k, v = _in_proj(xs, row(norm_mix), w_in_bf, qn, kn, cos_s, sin_s, tm_s)
        prw3 = prw.reshape(Bs, S, RW_COLS)
        r_out, st_s = _rwkv(prw3, state_shift[l][:, None, :], state_rwkv[l], rw, ns_s, S)
        o = _diff_sample(page_table, qb.reshape(Bs, S, -1), k.reshape(Bs, S * N_DIFF_HEADS, LANES),
                         v.reshape(Bs, S * N_DIFF_HEADS, LANES),
                         cache_diff_k[l].reshape(-1, PAGE * N_DIFF_HEADS, LANES),
                         cache_diff_v[l].reshape(-1, PAGE * N_DIFF_HEADS, LANES), lamp, sn, pp, lam_init)
        x2 = _mem_attn(xs.reshape(Bs, S, D), r_out, o, w_o_bf, cache_mem_k[l].reshape(Bs, M * N_MEM_HEADS, LANES),
                       cache_mem_v[l].reshape(Bs, M * N_MEM_HEADS, LANES),
                       row(norm_mem), w_mq_bf, row(mem_q_norm), w_mo_bf, nb_mem, S)
        xs = _ffn(x2.reshape(Bs * S, D), row(norm_ffn), w_gu_bf, w_d_bf, tm_s)
        outs[6].append(k.reshape(Bs, S, N_DIFF_HEADS, LANES))
        outs[7].append(v.reshape(Bs, S, N_DIFF_HEADS, LANES))
        outs[8].append(st_s)
        outs[9].append(prw3[:, -1])

    st = [jnp.stack(o) if depth > 1 else o[0][None] for o in outs]
    return (xp.reshape(B, T, D), xs.reshape(Bs, S, D), st[0], st[1], st[2], st[3], st[4], st[5],
            st[6], st[7], st[8], st[9])
```

```python
import functools
import math

import jax
import jax.numpy as jnp
from jax import lax
from jax.experimental import pallas as pl
from jax.experimental.pallas import tpu as pltpu

F32 = jnp.float32
BF16 = jnp.bfloat16

LANES = 128
HEAD = 64
RW_WIDTH = 512
RW_COLS = 3 * RW_WIDTH + 64 + 64 + 128
DIFF_WIDTH = 512
N_DIFF_HEADS = 4
MEM_WIDTH = 512
N_MEM_HEADS = 4
PAGE = 128
ROPE_THETA = 10000.0
RMS_EPS = 1e-6
LNX_EPS = 1e-5 * HEAD
NEG_BIG = -1e30
VMEM_LIMIT = 56 * 1024 * 1024
RWKV_CHUNK = 64
RWKV_TERMS = 1
RWKV_CHUNKS_PER_TRIP = 4
PAGES_PER_MATMUL = 2
QUERY_PANEL = 256
ONES_ROWS = 16
LOG2E = 1.4426950408889634


def _cparams(sem):
    return pltpu.CompilerParams(dimension_semantics=sem, vmem_limit_bytes=VMEM_LIMIT)


def _const_spec(shape):
    nd = len(shape)
    return pl.BlockSpec(shape, lambda *_: (0,) * nd)


def _nt_dot(a, b):
    return lax.dot_general(a, b, (((1,), (1,)), ((), ())), preferred_element_type=F32)


def _rms_rows(x, g):
    ms = jnp.mean(x * x, axis=-1, keepdims=True)
    return x * lax.rsqrt(ms + RMS_EPS) * g


def _lane_iota(shape):
    return lax.broadcasted_iota(jnp.int32, shape, len(shape) - 1)


def _rms_half(xh, gain):
    lo = _lane_iota(xh.shape) < HEAD
    x2 = xh * xh
    s0 = jnp.sum(jnp.where(lo, x2, 0.0), axis=-1, keepdims=True)
    s1 = jnp.sum(jnp.where(lo, 0.0, x2), axis=-1, keepdims=True)
    inv = jnp.where(lo, lax.rsqrt(s0 * (1.0 / HEAD) + RMS_EPS), lax.rsqrt(s1 * (1.0 / HEAD) + RMS_EPS))
    return xh * inv * gain


def _rope_block(xh, cos, sin_signed):
    first = (_lane_iota(xh.shape) & 32) == 0
    partner = jnp.where(first, pltpu.roll(xh, LANES - 32, 1), pltpu.roll(xh, 32, 1))
    return xh * cos + partner * sin_signed


def _rope_tables(pos):
    half = HEAD // 2
    inv = ROPE_THETA ** (-jnp.arange(half, dtype=F32) / half)
    ang = pos.astype(F32)[:, None] * inv[None, :]
    cos, sin = jnp.cos(ang), jnp.sin(ang)
    return jnp.tile(cos, (1, 4)), jnp.concatenate([-sin, sin, -sin, sin], axis=1)


def _split_hi_lo(x):
    hi = x.astype(BF16)
    lo = (x - hi.astype(F32)).astype(BF16)
    return jnp.concatenate([hi, lo], axis=1)


def _group_ones2():
    r = lax.broadcasted_iota(jnp.int32, (2 * LANES, LANES), 0)
    c = lax.broadcasted_iota(jnp.int32, (2 * LANES, LANES), 1)
    return jnp.where(((r & (LANES - 1)) >> 6) == (c >> 6), 1.0, 0.0).astype(BF16)


def _group_sum(x, g2):
    return jnp.dot(_split_hi_lo(x), g2, preferred_element_type=F32)


def _group_sum_wide(x, g2):
    return jnp.concatenate(
        [_group_sum(x[:, c * LANES:(c + 1) * LANES], g2) for c in range(x.shape[1] // LANES)], axis=1)


def _sigmoid(x):
    return 1.0 / (1.0 + jnp.exp(-x))


def _head_rows(n_rows, h, n_heads):
    return pl.ds(h, n_rows, stride=n_heads)


def _in_proj_kernel(x_ref, g_ref, w_ref, qn_ref, kn_ref, cos_ref, sin_ref,
                    prw_ref, qb_ref, k_ref, v_ref, *flash_refs):
    tm = x_ref.shape[0]
    sub = min(tm, LANES)
    qn, kn = qn_ref[...], kn_ref[...]
    for r0 in range(0, tm, sub):
        rows = slice(r0, r0 + sub)
        h = _rms_rows(x_ref[rows, :], g_ref[...]).astype(BF16)
        pd = jnp.dot(h, w_ref[:, RW_COLS:], preferred_element_type=F32)
        prw_ref[rows, :] = jnp.dot(h, w_ref[:, :RW_COLS], preferred_element_type=F32)
        cos, sin = cos_ref[rows, :], sin_ref[rows, :]
        for hh in range(N_DIFF_HEADS):
            sl = slice(hh * LANES, (hh + 1) * LANES)
            q = _rope_block(_rms_half(pd[:, sl], qn), cos, sin) * (HEAD ** -0.5 * LOG2E)
            k = _rope_block(_rms_half(pd[:, DIFF_WIDTH + hh * LANES:DIFF_WIDTH + (hh + 1) * LANES], kn), cos, sin)
            v = pd[:, 2 * DIFF_WIDTH + hh * LANES:2 * DIFF_WIDTH + (hh + 1) * LANES]
            head_rows = pl.ds(r0 * N_DIFF_HEADS + hh, sub, stride=N_DIFF_HEADS)
            qb_ref[rows, sl] = q.astype(BF16)
            k_ref[head_rows, :] = k
            v_ref[head_rows, :] = v
            if flash_refs:
                kb_ref, vt_ref = flash_refs
                kb_ref[rows, sl] = k.astype(BF16)
                vt_ref[0, sl, rows] = jnp.transpose(v).astype(BF16)


def _in_proj(x, g, w_bf, qn, kn, cos, sin, tm, seq_len=None):
    n, d = x.shape
    period = cos.shape[0] // tm
    row = lambda i: (i, 0)
    outs = [jax.ShapeDtypeStruct((n, RW_COLS), F32), jax.ShapeDtypeStruct((n, DIFF_WIDTH), BF16),
            jax.ShapeDtypeStruct((n * N_DIFF_HEADS, LANES), F32), jax.ShapeDtypeStruct((n * N_DIFF_HEADS, LANES), F32)]
    out_specs = [pl.BlockSpec((tm, RW_COLS), row), pl.BlockSpec((tm, DIFF_WIDTH), row),
                 pl.BlockSpec((tm * N_DIFF_HEADS, LANES), row), pl.BlockSpec((tm * N_DIFF_HEADS, LANES), row)]
    if seq_len is not None:
        per_seq = seq_len // tm
        outs += [jax.ShapeDtypeStruct((n, DIFF_WIDTH), BF16), jax.ShapeDtypeStruct((n // seq_len, DIFF_WIDTH, seq_len), BF16)]
        out_specs += [pl.BlockSpec((tm, DIFF_WIDTH), row),
                      pl.BlockSpec((1, DIFF_WIDTH, tm), lambda i: (i // per_seq, 0, i % per_seq))]
    return pl.pallas_call(
        _in_proj_kernel,
        grid=(n // tm,),
        in_specs=[pl.BlockSpec((tm, d), row), _const_spec((1, d)), _const_spec(w_bf.shape),
                  _const_spec((1, LANES)), _const_spec((1, LANES)),
                  pl.BlockSpec((tm, LANES), lambda i: (i % period, 0)),
                  pl.BlockSpec((tm, LANES), lambda i: (i % period, 0))],
        out_specs=tuple(out_specs),
        out_shape=tuple(outs),
        compiler_params=_cparams(("parallel",)),
        name="in_proj",
    )(x, g, w_bf, qn, kn, cos, sin)


def _rwkv_token_vectors(p, shifted, prm, g2):
    mu_ref, w0_ref, w2_ref, a0_ref, a2_ref, g2_ref, kk_ref, ka_ref, rk_ref = prm
    ps = p + (shifted - p) * mu_ref[...]
    r = ps[:, :RW_WIDTH]
    k = ps[:, RW_WIDTH:2 * RW_WIDTH]
    v = ps[:, 2 * RW_WIDTH:3 * RW_WIDTH]
    wa = ps[:, 3 * RW_WIDTH:3 * RW_WIDTH + LANES]
    gd = ps[:, 3 * RW_WIDTH + LANES:]
    z = -(w0_ref[...] + jnp.dot(jnp.tanh(wa).astype(BF16), w2_ref[...], preferred_element_type=F32))
    w_log = -(jnp.maximum(z, 0.0) + jnp.log1p(jnp.exp(-jnp.abs(z)))) - 0.5
    a = _sigmoid(a0_ref[...] + jnp.dot(wa.astype(BF16), a2_ref[...], preferred_element_type=F32))
    gate = jnp.dot(_sigmoid(gd).astype(BF16), g2_ref[...], preferred_element_type=F32)
    kk = k * kk_ref[...]
    kk = kk / jnp.maximum(jnp.sqrt(_group_sum_wide(kk * kk, g2)), 1e-12)
    k2 = k * (1.0 + (a - 1.0) * ka_ref[...])
    bonus = _group_sum_wide(r * k2 * rk_ref[...], g2) * v
    return r, -jnp.exp(w_log), k2, v, -kk, kk * a, bonus, gate


def _rwkv_finish(y, bonus, gate, lw, lb, g2):
    mean = _group_sum_wide(y, g2) * (1.0 / HEAD)
    d = y - mean
    var = _group_sum_wide(d * d, g2) * (1.0 / HEAD)
    return (d * lax.rsqrt(var + LNX_EPS) * lw + lb + bonus) * gate


def _mm(x, w, terms):
    if terms == 1:
        return jnp.dot(x.astype(BF16), w.astype(BF16), preferred_element_type=F32)
    w_hi = w.astype(BF16)
    out = jnp.dot(_split_hi_lo(x), jnp.concatenate([w_hi, w_hi], axis=0), preferred_element_type=F32)
    if terms == 3:
        w_lo = (w - w_hi.astype(F32)).astype(BF16)
        out = out + jnp.dot(x.astype(BF16), w_lo, preferred_element_type=F32)
    return out


def _stack_heads(x):
    lo = _lane_iota(x.shape) < HEAD
    return jnp.concatenate([jnp.where(lo, x, 0.0), jnp.where(lo, 0.0, x)], axis=0)


def _nt_mm(x, w, terms):
    if terms == 1:
        return _nt_dot(x.astype(BF16), w.astype(BF16))
    w_hi = w.astype(BF16)
    out = _nt_dot(_split_hi_lo(x), jnp.concatenate([w_hi, w_hi], axis=1))
    if terms == 3:
        out = out + _nt_dot(x.astype(BF16), (w - w_hi.astype(F32)).astype(BF16))
    return out


def _chunk_units(units, slots, zs, terms):
    n = len(units)
    c2 = 2 * units[0][0].shape[0]
    ti = lax.broadcasted_iota(jnp.int32, (c2, c2), 0)
    tj = lax.broadcasted_iota(jnp.int32, (c2, c2), 1)
    eye = ti == tj
    strict = (ti & (c2 // 2 - 1)) > (tj & (c2 // 2 - 1))
    incl = (ti & (c2 // 2 - 1)) >= (tj & (c2 // 2 - 1))
    s_a = [_stack_heads(u[0]) for u in units]
    s_r = [_stack_heads(u[1]) for u in units]
    s_v = [_stack_heads(u[6]) for u in units]
    sc = [_nt_mm(jnp.concatenate([s_a[i], s_r[i]], axis=0),
                 jnp.concatenate([_stack_heads(units[i][2]), _stack_heads(units[i][3])], axis=0), terms)
          for i in range(n)]
    x = [jnp.where(strict, s[:c2, :c2], 0.0) for s in sc]
    l_ak = [jnp.where(strict, s[:c2, c2:], 0.0) for s in sc]
    m_rbk = [jnp.concatenate([jnp.where(incl, s[c2:, :c2], 0.0), jnp.where(incl, s[c2:, c2:], 0.0)], axis=1) for s in sc]
    t_inv = [jnp.where(eye, 1.0, 0.0) + xi for xi in x]
    lv = [_mm(l_ak[i], s_v[i], terms) for i in range(n)]
    for _ in range(max(1, (c2 // 2 - 1).bit_length()) - 1):
        x = [_mm(xi, xi, terms) for xi in x]
        t_inv = [t_inv[i] + _mm(t_inv[i], x[i], terms) for i in range(n)]
    pq = [_mm(t_inv[i], jnp.concatenate([s_a[i], lv[i]], axis=1), terms) for i in range(n)]
    pqv = [jnp.concatenate([pq[i], jnp.concatenate([jnp.zeros_like(s_v[i]), s_v[i]], axis=1)], axis=0) for i in range(n)]
    ah = [_mm(jnp.transpose(jnp.concatenate([_stack_heads(units[i][4]), _stack_heads(units[i][5])], axis=0)), pqv[i], terms)
          for i in range(n)]
    ry = [_mm(m_rbk[i], pqv[i], terms) for i in range(n)]
    zs, ys = list(zs), []
    for i in range(n):
        z = zs[slots[i]]
        y = _mm(s_r[i] + ry[i][:, :c2], z, 3) + ry[i][:, c2:]
        ys.append(y[:c2 // 2] + y[c2 // 2:])
        zs[slots[i]] = _mm(jnp.where(eye, units[i][7], 0.0) + ah[i][:, :c2], z, 3) + ah[i][:, c2:]
    return ys, zs


def _rwkv_kernel(p_ref, prev_ref, s0_ref, mu_ref, w0_ref, w2_ref, a0_ref, a2_ref, g2_ref,
                 kk_ref, ka_ref, rk_ref, lw_ref, lb_ref,
                 out_ref, st_ref,
                 state, carry, r_s, w_s, k_s, v_s, a_s, b_s, y_s, bonus_s, gate_s, *, ns, tt):
    ti = pl.program_id(1)
    npair = RW_WIDTH // LANES
    g2 = _group_ones2()

    @pl.when(ti == 0)
    def _():
        for s in range(ns):
            for j in range(npair):
                state[(s * npair + j) * HEAD:(s * npair + j + 1) * HEAD, :] = jnp.concatenate(
                    [s0_ref[s, 2 * j], s0_ref[s, 2 * j + 1]], axis=1)
        carry[...] = prev_ref[...]

    row0 = lax.broadcasted_iota(jnp.int32, (tt, RW_COLS), 0) == 0
    prm = (mu_ref, w0_ref, w2_ref, a0_ref, a2_ref, g2_ref, kk_ref, ka_ref, rk_ref)
    for s in range(ns):
        p = p_ref[s]
        shifted = jnp.where(row0, carry[s], pltpu.roll(p, 1, 0))
        carry[s] = p[tt - 1:tt, :]
        r_s[s], log_w, k_s[s], v_s[s], a_s[s], b_s[s], bonus_s[s], gate_s[s] = _rwkv_token_vectors(p, shifted, prm, g2)
        w_s[s] = jnp.exp(log_w)

    tiles = [(s, j) for s in range(ns) for j in range(npair)]
    nt = len(tiles)
    sub = lax.broadcasted_iota(jnp.int32, (nt * HEAD, LANES), 0)
    e1 = jnp.where((_lane_iota((nt * HEAD, LANES)) & (HEAD - 1)) == (sub & (HEAD - 1)), 1.0, 0.0)
    seqs = dict(r=r_s, w=w_s, k=k_s, v=v_s, a=a_s, b=b_s)

    def block(tb, c):
        base = pl.multiple_of(tb * 8, 8)
        rows8 = {q: [ref[s, pl.ds(base, 8), j * LANES:(j + 1) * LANES] for (s, j) in tiles] for q, ref in seqs.items()}
        ys = []
        for i in range(8):
            bc = lambda q: jnp.concatenate(
                [jnp.broadcast_to(rows8[q][n][i:i + 1], (HEAD, LANES)) for n in range(nt)], axis=0)
            st = state[...]
            sa = jnp.dot(_split_hi_lo(st * bc("a")), g2, preferred_element_type=F32)
            vcol = jnp.dot((e1 * bc("v")).astype(BF16), g2[:LANES], preferred_element_type=F32)
            st = st * bc("w") + sa * bc("b") + vcol * bc("k")
            state[...] = st
            yred = jnp.dot((st * bc("r")).astype(BF16), g2[:LANES], preferred_element_type=F32)
            ys.append(jnp.sum((e1 * yred).reshape(nt, HEAD, LANES), axis=1))
        for n, (s, j) in enumerate(tiles):
            y_s[s, pl.ds(base, 8), j * LANES:(j + 1) * LANES] = jnp.concatenate([y[n:n + 1] for y in ys], axis=0)
        return c

    lax.fori_loop(0, tt // 8, block, 0)

    for s in range(ns):
        out_ref[s] = _rwkv_finish(y_s[s], bonus_s[s], gate_s[s], lw_ref[...], lb_ref[...], g2)

    @pl.when(ti == pl.num_programs(1) - 1)
    def _():
        for s in range(ns):
            for j in range(npair):
                tile = state[(s * npair + j) * HEAD:(s * npair + j + 1) * HEAD, :]
                st_ref[s, 2 * j] = tile[:, :HEAD]
                st_ref[s, 2 * j + 1] = tile[:, HEAD:]


def _rwkv_chunk_kernel(p_ref, prev_ref, z0_ref, mu_ref, w0_ref, w2_ref, a0_ref, a2_ref, g2_ref,
                       kk_ref, ka_ref, rk_ref, lw_ref, lb_ref,
                       out_ref, zt_ref,
                       zstate, carry, r_s, g_s, k_s, v_s, a_s, b_s, y_s, bonus_s, gate_s, *, tt, chunk, cpt, terms):
    ti = pl.program_id(1)
    npair = RW_WIDTH // LANES
    g2 = _group_ones2()

    @pl.when(ti == 0)
    def _():
        zstate[...] = z0_ref[0]
        carry[...] = prev_ref[0]

    p = p_ref[0]
    row0 = lax.broadcasted_iota(jnp.int32, (tt, RW_COLS), 0) == 0
    shifted = jnp.where(row0, carry[...], pltpu.roll(p, 1, 0))
    carry[...] = p[tt - 1:tt, :]
    prm = (mu_ref, w0_ref, w2_ref, a0_ref, a2_ref, g2_ref, kk_ref, ka_ref, rk_ref)
    (r_s[...], g_s[...], k_s[...], v_s[...], a_s[...], b_s[...], bonus_s[...], gate_s[...]) = _rwkv_token_vectors(
        p, shifted, prm, g2)

    ci = lax.broadcasted_iota(jnp.int32, (chunk, 3 * chunk), 0)
    cj = lax.broadcasted_iota(jnp.int32, (chunk, 3 * chunk), 1)
    cj = jnp.where(cj >= 2 * chunk, cj - 2 * chunk, jnp.where(cj >= chunk, cj - chunk, cj))
    ltri3 = jnp.where(cj <= ci, 1.0, 0.0).astype(BF16)

    lanes = [slice(j * LANES, (j + 1) * LANES) for j in range(npair)]

    def scaled(c):
        rows = pl.ds(pl.multiple_of(c * chunk, chunk), chunk)
        g = g_s[rows, :]
        g_hi = g.astype(BF16)
        g_r = g - g_hi.astype(F32)
        g_mid = g_r.astype(BF16)
        g_lo = (g_r - g_mid.astype(F32)).astype(BF16)
        cum = jnp.dot(ltri3, jnp.concatenate([g_hi, g_mid, g_lo], axis=0), preferred_element_type=F32)
        total = cum[chunk - 1:chunk, :]
        e_in, e_ex, e_inv, e_rest = jnp.exp(cum), jnp.exp(cum - g), jnp.exp(-cum), jnp.exp(total - cum)
        a_t, r_t = a_s[rows, :] * e_ex, r_s[rows, :] * e_in
        b, k, v = b_s[rows, :], k_s[rows, :], v_s[rows, :]
        ops = (a_t, r_t, b * e_inv, k * e_inv, b * e_rest, k * e_rest, v, jnp.exp(total))
        return rows, [tuple(t[:, sl] for t in ops) for sl in lanes]

    def body(trip, carry_):
        parts = [scaled(trip * cpt + c) for c in range(cpt)]
        ys, z_new = _chunk_units([u for _, units in parts for u in units], list(range(npair)) * cpt,
                                 [zstate[sl, :] for sl in lanes], terms)
        for c, (rows, _) in enumerate(parts):
            for j, sl in enumerate(lanes):
                y_s[rows, sl] = ys[c * npair + j]
        for j, sl in enumerate(lanes):
            zstate[sl, :] = z_new[j]
        return carry_

    lax.fori_loop(0, tt // (cpt * chunk), body, 0)
    out_ref[0] = _rwkv_finish(y_s[...], bonus_s[...], gate_s[...], lw_ref[...], lb_ref[...], g2)

    @pl.when(ti == pl.num_programs(1) - 1)
    def _():
        zt_ref[0] = zstate[...]


def _rwkv(prw, prev, s0, rw, ns, tt):
    S, T, _ = prw.shape
    npair = RW_WIDTH // LANES
    vec = lambda n: _const_spec((1, n))
    blk = lambda w: pl.BlockSpec((ns, tt, w), lambda i, t: (i, t, 0))
    st_spec = pl.BlockSpec((ns, 2 * npair, HEAD, HEAD), lambda i, t: (i, 0, 0, 0))
    seq = lambda: pltpu.VMEM((ns, tt, RW_WIDTH), F32)
    return pl.pallas_call(
        functools.partial(_rwkv_kernel, ns=ns, tt=tt),
        grid=(S // ns, T // tt),
        in_specs=[blk(RW_COLS), pl.BlockSpec((ns, 1, RW_COLS), lambda i, t: (i, 0, 0)), st_spec,
                  vec(RW_COLS), vec(RW_WIDTH), _const_spec((LANES, RW_WIDTH)), vec(RW_WIDTH),
                  _const_spec((LANES, RW_WIDTH)), _const_spec((LANES, RW_WIDTH)),
                  vec(RW_WIDTH), vec(RW_WIDTH), vec(RW_WIDTH), vec(RW_WIDTH), vec(RW_WIDTH)],
        out_specs=(blk(RW_WIDTH), st_spec),
        out_shape=(jax.ShapeDtypeStruct((S, T, RW_WIDTH), F32),
                   jax.ShapeDtypeStruct((S, 2 * npair, HEAD, HEAD), F32)),
        scratch_shapes=[pltpu.VMEM((ns * npair * HEAD, LANES), F32), pltpu.VMEM((ns, 1, RW_COLS), F32)]
                       + [seq() for _ in range(9)],
        compiler_params=_cparams(("parallel", "arbitrary")),
        name="rwkv",
    )(prw, prev, s0, *rw)


def _rwkv_chunked(prw, prev, z0, rw, tt, chunk, terms):
    S, T, _ = prw.shape
    vec = lambda n: _const_spec((1, n))
    blk = lambda w: pl.BlockSpec((1, tt, w), lambda i, t: (i, t, 0))
    z_spec = pl.BlockSpec((1, RW_WIDTH, LANES), lambda i, t: (i, 0, 0))
    seq = lambda: pltpu.VMEM((tt, RW_WIDTH), F32)
    return pl.pallas_call(
        functools.partial(_rwkv_chunk_kernel, tt=tt, chunk=chunk, cpt=_tile(tt // chunk, RWKV_CHUNKS_PER_TRIP),
                          terms=terms),
        grid=(S, T // tt),
        in_specs=[blk(RW_COLS), pl.BlockSpec((1, 1, RW_COLS), lambda i, t: (i, 0, 0)), z_spec,
                  vec(RW_COLS), vec(RW_WIDTH), _const_spec((LANES, RW_WIDTH)), vec(RW_WIDTH),
                  _const_spec((LANES, RW_WIDTH)), _const_spec((LANES, RW_WIDTH)),
                  vec(RW_WIDTH), vec(RW_WIDTH), vec(RW_WIDTH), vec(RW_WIDTH), vec(RW_WIDTH)],
        out_specs=(blk(RW_WIDTH), z_spec),
        out_shape=(jax.ShapeDtypeStruct((S, T, RW_WIDTH), F32), jax.ShapeDtypeStruct((S, RW_WIDTH, LANES), F32)),
        scratch_shapes=[pltpu.VMEM((RW_WIDTH, LANES), F32), pltpu.VMEM((1, RW_COLS), F32)] + [seq() for _ in range(9)],
        compiler_params=_cparams(("parallel", "arbitrary")),
        name="rwkv_chunked",
    )(prw, prev, z0, *rw)


def _state_from_blockdiag(z, S):
    z = z.reshape(S, 4, 2, HEAD, 2, HEAD)
    z = jnp.stack([z[:, :, 0, :, 0, :], z[:, :, 1, :, 1, :]], axis=2)
    return z.reshape(S, 8, HEAD, HEAD).transpose(0, 1, 3, 2)


def _lambda(lamp_ref, lam_init):
    lp = lamp_ref[...]
    s1 = jnp.sum(lp[0:1] * lp[1:2], axis=-1, keepdims=True)
    s2 = jnp.sum(lp[2:3] * lp[3:4], axis=-1, keepdims=True)
    return jnp.exp(s1) - jnp.exp(s2) + lam_init


def _stack_halves(q):
    lo = _lane_iota(q.shape) < HEAD
    zero = jnp.zeros_like(q)
    return jnp.concatenate([jnp.where(lo, q, zero), jnp.where(lo, zero, q)], axis=0)


def _diff_finish(acc, l, lam, sn, lam_init, rows):
    o = acc[:rows] / l[:rows] - lam * (acc[rows:] / l[rows:])
    return _rms_rows(o, sn) * (1.0 - lam_init)


def _diff_prompt_kernel(q_ref, k_ref, vt_ref, lamp_ref, sn_ref, o_ref, m_ref, acc_ref, *, tq, hp, lam_init):
    i = pl.program_id(2)
    heads = [slice(h * LANES, (h + 1) * LANES) for h in range(hp)]
    q2 = [_stack_halves(q_ref[0, :, sl]) for sl in heads]
    ones = jnp.ones((ONES_ROWS, tq), BF16)
    panel = min(tq, QUERY_PANEL)
    units = [(h, slice(c, c + panel)) for h in range(hp) for c in range(0, 2 * tq, panel)]

    def process(off, diagonal):
        keys = [k_ref[0, pl.ds(off, tq), sl] for sl in heads]
        vals = [jnp.concatenate([vt_ref[0, sl, pl.ds(off, tq)], ones], axis=0) for sl in heads]
        first_q = [pan.start % tq for _, pan in units]
        n_keys = [min(tq, fq + panel) if diagonal else tq for fq in first_q]
        s = [_nt_dot(keys[h][:n_keys[u]], q2[h][pan]) for u, (h, pan) in enumerate(units)]
        if diagonal:
            for u in range(len(units)):
                key = lax.broadcasted_iota(jnp.int32, s[u].shape, 0)
                qry = lax.broadcasted_iota(jnp.int32, s[u].shape, 1) + first_q[u]
                s[u] = jnp.where(key <= qry, s[u], NEG_BIG)
        m_old = [m_ref[h, :, pan] for h, pan in units]
        m_new = [jnp.maximum(m_old[u], jnp.max(s[u], axis=0, keepdims=True)) for u in range(len(units))]
        alpha = [jnp.exp2(m_old[u] - m_new[u]) for u in range(len(units))]
        p = [jnp.exp2(s[u] - m_new[u]) for u in range(len(units))]
        pv = [jnp.dot(vals[h][:, :n_keys[u]], p[u].astype(BF16), preferred_element_type=F32)
              for u, (h, _) in enumerate(units)]
        for u, (h, pan) in enumerate(units):
            acc_ref[h, :, pan] = alpha[u] * acc_ref[h, :, pan] + pv[u]
            m_ref[h, :, pan] = m_new[u]

    m_ref[...] = jnp.full(m_ref.shape, NEG_BIG, F32)
    acc_ref[...] = jnp.zeros(acc_ref.shape, F32)
    process(pl.multiple_of(i * tq, tq), True)

    def body(j, c):
        process(pl.multiple_of(j * tq, tq), False)
        return c

    lax.fori_loop(0, i, body, 0)
    lam = _lambda(lamp_ref, lam_init)
    for h, sl in enumerate(heads):
        acc = acc_ref[h]
        acc = acc[:LANES] / acc[LANES:LANES + 1]
        o_t = acc[:, :tq] - lam * acc[:, tq:]
        ms = jnp.mean(o_t * o_t, axis=0, keepdims=True)
        o_t = o_t * lax.rsqrt(ms + RMS_EPS)
        o_ref[0, :, sl] = jnp.transpose(o_t) * sn_ref[...] * (1.0 - lam_init)


def _diff_prompt(qb, kb, vt, lamp, sn, tq, hp, lam_init):
    B, T, _ = qb.shape
    tile = pl.BlockSpec((1, tq, hp * LANES), lambda b, h, i: (b, i, h))
    return pl.pallas_call(
        functools.partial(_diff_prompt_kernel, tq=tq, hp=hp, lam_init=lam_init),
        grid=(B, N_DIFF_HEADS // hp, T // tq),
        in_specs=[tile, pl.BlockSpec((1, T, hp * LANES), lambda b, h, i: (b, 0, h)),
                  pl.BlockSpec((1, hp * LANES, T), lambda b, h, i: (b, h, 0)),
                  _const_spec((4, HEAD)), _const_spec((1, LANES))],
        out_specs=tile,
        out_shape=jax.ShapeDtypeStruct((B, T, DIFF_WIDTH), F32),
        scratch_shapes=[pltpu.VMEM((hp, 1, 2 * tq), F32), pltpu.VMEM((hp, LANES + ONES_ROWS, 2 * tq), F32)],
        compiler_params=_cparams(("parallel", "parallel", "arbitrary")),
        name="diff_prompt",
    )(qb, kb, vt, lamp, sn)


def _diff_sample_kernel(pt_ref, q_ref, kn_ref, vn_ref, lamp_ref, sn_ref, *rest, pp, s_new, lam_init):
    del pt_ref
    kpages, vpages = rest[:pp], rest[pp:2 * pp]
    o_ref, m_ref, l_ref, acc_ref = rest[2 * pp:]
    j = pl.program_id(1)
    nh = N_DIFF_HEADS
    rows = 2 * s_new
    q = q_ref[0]
    q2 = [_stack_halves(q[:, h * LANES:(h + 1) * LANES]) for h in range(nh)]

    def update(scores, values):
        s = jnp.concatenate([jnp.concatenate(sc, axis=1) if len(sc) > 1 else sc[0] for sc in scores], axis=0)
        m_old = m_ref[...]
        m_new = jnp.maximum(m_old, jnp.max(s, axis=-1, keepdims=True))
        alpha = jnp.exp2(m_old - m_new)
        p = jnp.exp2(s - m_new)
        pv = []
        for h in range(nh):
            ph = p[h * rows:(h + 1) * rows].astype(BF16)
            acc, col = None, 0
            for vh in values[h]:
                part = jnp.dot(ph[:, col:col + vh.shape[0]], vh, preferred_element_type=F32)
                acc = part if acc is None else acc + part
                col += vh.shape[0]
            pv.append(acc)
        l_ref[...] = alpha * l_ref[...] + jnp.sum(p, axis=-1, keepdims=True)
        acc_ref[...] = alpha * acc_ref[...] + jnp.concatenate(pv, axis=0)
        m_ref[...] = m_new

    @pl.when(j == 0)
    def _():
        m_ref[...] = jnp.full(m_ref.shape, NEG_BIG, F32)
        l_ref[...] = jnp.zeros(l_ref.shape, F32)
        acc_ref[...] = jnp.zeros(acc_ref.shape, F32)
        pad = jnp.zeros((PAGE - s_new, LANES), F32)
        row = lax.broadcasted_iota(jnp.int32, (rows, PAGE), 0)
        col = lax.broadcasted_iota(jnp.int32, (rows, PAGE), 1)
        mask = col <= jnp.where(row >= s_new, row - s_new, row)
        new_rows = lambda ref, h: jnp.concatenate([ref[0, _head_rows(s_new, h, nh), :], pad], axis=0).astype(BF16)
        update([[jnp.where(mask, _nt_dot(q2[h], new_rows(kn_ref, h)), NEG_BIG)] for h in range(nh)],
               [[new_rows(vn_ref, h)] for h in range(nh)])

    grp = PAGES_PER_MATMUL if pp % PAGES_PER_MATMUL == 0 else 1
    groups = [range(c, c + grp) for c in range(0, pp, grp)]
    head_block = lambda refs, h, g: jnp.concatenate(
        [refs[c][_head_rows(PAGE, h, nh), :].astype(BF16) for c in g], axis=0)
    update([[_nt_dot(q2[h], head_block(kpages, h, g)) for g in groups] for h in range(nh)],
           [[head_block(vpages, h, g) for g in groups] for h in range(nh)])

    @pl.when(j == pl.num_programs(1) - 1)
    def _():
        lam = _lambda(lamp_ref, lam_init)
        acc, l = acc_ref[...], l_ref[...]
        for h in range(nh):
            o_ref[0, :, h * LANES:(h + 1) * LANES] = _diff_finish(
                acc[h * rows:(h + 1) * rows], l[h * rows:(h + 1) * rows], lam, sn_ref[...], lam_init, s_new)


def _diff_sample(page_table, qb, k_new, v_new, pool_k, pool_v, lamp, sn, pp, lam_init):
    Bs, S, _ = qb.shape
    n_pages = page_table.shape[1]
    nh = N_DIFF_HEADS
    qspec = pl.BlockSpec((1, S, DIFF_WIDTH), lambda b, j, pt: (b, 0, 0))
    new = pl.BlockSpec((1, S * nh, LANES), lambda b, j, pt: (b, 0, 0))
    page = lambda c: pl.BlockSpec((None, PAGE * nh, LANES), lambda b, j, pt, c=c: (pt[b, j * pp + c], 0, 0))
    cst = lambda shape: pl.BlockSpec(shape, lambda b, j, pt: (0,) * len(shape))
    return pl.pallas_call(
        functools.partial(_diff_sample_kernel, pp=pp, s_new=S, lam_init=lam_init),
        grid_spec=pltpu.PrefetchScalarGridSpec(
            num_scalar_prefetch=1,
            grid=(Bs, n_pages // pp),
            in_specs=[qspec, new, new, cst((4, HEAD)), cst((1, LANES))] + [page(c) for c in range(pp)] * 2,
            out_specs=qspec,
            scratch_shapes=[pltpu.VMEM((nh * 2 * S, 1), F32), pltpu.VMEM((nh * 2 * S, 1), F32),
                            pltpu.VMEM((nh * 2 * S, LANES), F32)]),
        out_shape=jax.ShapeDtypeStruct((Bs, S, DIFF_WIDTH), F32),
        compiler_params=_cparams(("parallel", "arbitrary")),
        name="diff_sample",
    )(page_table, qb, k_new, v_new, lamp, sn, *([pool_k] * pp), *([pool_v] * pp))


def _mem_kv_kernel(x_ref, g_ref, w_ref, kn_ref, k_ref, v_ref):
    h = _rms_rows(x_ref[...], g_ref[...]).astype(BF16)
    kv = jnp.dot(h, w_ref[...], preferred_element_type=F32)
    tm = x_ref.shape[0]
    for hh in range(N_MEM_HEADS):
        sl = slice(hh * LANES, (hh + 1) * LANES)
        k_ref[_head_rows(tm, hh, N_MEM_HEADS), :] = _rms_rows(kv[:, sl], kn_ref[...])
        v_ref[_head_rows(tm, hh, N_MEM_HEADS), :] = kv[:, MEM_WIDTH + hh * LANES:MEM_WIDTH + (hh + 1) * LANES]


def _mem_kv(mem, g, w_bf, kn, tm):
    n, d = mem.shape
    row = lambda i: (i, 0)
    return pl.pallas_call(
        _mem_kv_kernel,
        grid=(n // tm,),
        in_specs=[pl.BlockSpec((tm, d), row), _const_spec((1, d)), _const_spec(w_bf.shape), _const_spec((1, LANES))],
        out_specs=(pl.BlockSpec((tm * N_MEM_HEADS, LANES), row),) * 2,
        out_shape=(jax.ShapeDtypeStruct((n * N_MEM_HEADS, LANES), F32),) * 2,
        compiler_params=_cparams(("parallel",)),
        name="mem_kv",
    )(mem, g, w_bf, kn)


def _mem_attn_kernel(x_ref, r_ref, a_ref, wmix_ref, mk_ref, mv_ref, g_ref, wq_ref, qn_ref, wo_ref, y_ref, o_scr,
                     *, nb, tr, n_mem):
    d = x_ref.shape[-1]
    rows = nb * tr
    x = x_ref[...].reshape(rows, d)
    x = x + jnp.dot(r_ref[...].reshape(rows, RW_WIDTH).astype(BF16), wmix_ref[:RW_WIDTH, :], preferred_element_type=F32)
    x = x + jnp.dot(a_ref[...].reshape(rows, DIFF_WIDTH).astype(BF16), wmix_ref[RW_WIDTH:, :], preferred_element_type=F32)
    h = _rms_rows(x, g_ref[...]).astype(BF16)
    q = jnp.dot(h, wq_ref[...], preferred_element_type=F32)
    heads = [slice(hh * LANES, (hh + 1) * LANES) for hh in range(N_MEM_HEADS)]
    qh = [(_rms_rows(q[:, sl], qn_ref[...]) * (LANES ** -0.5 * LOG2E)).astype(BF16) for sl in heads]
    units = [(hh, n) for hh in range(N_MEM_HEADS) for n in range(nb)]
    mem_rows = lambda ref, hh, n: ref[n, _head_rows(n_mem, hh, N_MEM_HEADS), :].astype(BF16)
    s = [_nt_dot(qh[hh][n * tr:(n + 1) * tr], mem_rows(mk_ref, hh, n)) for hh, n in units]
    p = [jnp.exp2(su - jnp.max(su, axis=-1, keepdims=True)) for su in s]
    o = [jnp.dot(p[u].astype(BF16), mem_rows(mv_ref, hh, n), preferred_element_type=F32) for u, (hh, n) in enumerate(units)]
    for u, (hh, n) in enumerate(units):
        o_scr[n * tr:(n + 1) * tr, heads[hh]] = o[u] / jnp.sum(p[u], axis=-1, keepdims=True)
    y = x + jnp.dot(o_scr[...].astype(BF16), wo_ref[...], preferred_element_type=F32)
    y_ref[...] = y.reshape(nb, tr, d)


def _mem_attn(x, r_out, attn, wmix_bf, mk, mv, g, wq_bf, qn, wo_bf, nb, tr):
    G, R, d = x.shape
    M4 = mk.shape[1]
    tile = lambda w: pl.BlockSpec((nb, tr, w), lambda i, t: (i, t, 0))
    mem = pl.BlockSpec((nb, M4, LANES), lambda i, t: (i, 0, 0))
    return pl.pallas_call(
        functools.partial(_mem_attn_kernel, nb=nb, tr=tr, n_mem=M4 // N_MEM_HEADS),
        grid=(G // nb, R // tr),
        in_specs=[tile(d), tile(RW_WIDTH), tile(DIFF_WIDTH), _const_spec(wmix_bf.shape), mem, mem,
                  _const_spec((1, d)), _const_spec(wq_bf.shape), _const_spec((1, LANES)), _const_spec(wo_bf.shape)],
        out_specs=tile(d),
        out_shape=jax.ShapeDtypeStruct((G, R, d), F32),
        scratch_shapes=[pltpu.VMEM((nb * tr, MEM_WIDTH), F32)],
        compiler_params=_cparams(("parallel", "arbitrary")),
        name="mem_attn",
    )(x, r_out, attn, wmix_bf, mk, mv, g, wq_bf, qn, wo_bf)


def _ffn_kernel(x_ref, g_ref, wgu_ref, wd_ref, y_ref, *, d_ff, fc):
    x = x_ref[...]
    h = _rms_rows(x, g_ref[...]).astype(BF16)
    y = x
    for c in range(d_ff // fc):
        gt = jnp.dot(h, wgu_ref[:, c * fc:(c + 1) * fc], preferred_element_type=F32)
        up = jnp.dot(h, wgu_ref[:, d_ff + c * fc:d_ff + (c + 1) * fc], preferred_element_type=F32)
        act = (gt * _sigmoid(gt) * up).astype(BF16)
        y = y + jnp.dot(act, wd_ref[c * fc:(c + 1) * fc, :], preferred_element_type=F32)
    y_ref[...] = y


def _ffn(x, g, wgu_bf, wd_bf, tm):
    n, d = x.shape
    d_ff = wd_bf.shape[0]
    fc = d_ff // 2 if (d_ff // 2) % LANES == 0 else d_ff
    row = lambda i: (i, 0)
    return pl.pallas_call(
        functools.partial(_ffn_kernel, d_ff=d_ff, fc=fc),
        grid=(n // tm,),
        in_specs=[pl.BlockSpec((tm, d), row), _const_spec((1, d)), _const_spec(wgu_bf.shape), _const_spec(wd_bf.shape)],
        out_specs=pl.BlockSpec((tm, d), row),
        out_shape=jax.ShapeDtypeStruct((n, d), F32),
        compiler_params=_cparams(("parallel",)),
        name="ffn",
    )(x, g, wgu_bf, wd_bf)


def _tile(n, pref):
    t = min(n, pref)
    while n % t:
        t //= 2
    return t


def kernel(x_prompt, x_sample, cache_diff_k, cache_diff_v, cache_mem_k, cache_mem_v, state_rwkv, state_shift, page_table, mem_prompt, norm_mix, w_in, rwkv_mu, rwkv_w0, rwkv_w2, rwkv_a0, rwkv_a2, rwkv_g2, rwkv_k_k, rwkv_k_a, rwkv_r_k, rwkv_lnx_w, rwkv_lnx_b, diff_q_norm, diff_k_norm, diff_lam_q1, diff_lam_k1, diff_lam_q2, diff_lam_k2, diff_sub_norm, w_o, norm_mem, norm_memkv, w_mq, w_mkv, mem_q_norm, mem_k_norm, w_mo, norm_ffn, w_gate_up, w_down):
    B, T, D = x_prompt.shape
    Bs, S, _ = x_sample.shape
    depth = w_in.shape[0]
    n_pages = page_table.shape[1]
    past = n_pages * PAGE
    M = mem_prompt.shape[1]
    xp = x_prompt.reshape(B * T, D)
    xs = x_sample.reshape(Bs * S, D)
    mem = mem_prompt.reshape(B * M, D)

    tm_p = _tile(T, 512)
    tm_wide = _tile(T, 512)
    tm_s = _tile(Bs * S, 256)
    cos_p, sin_p = _rope_tables(jnp.arange(T, dtype=jnp.int32))
    cos_s, sin_s = _rope_tables(past + jnp.arange(S, dtype=jnp.int32))
    cos_s, sin_s = jnp.tile(cos_s, (tm_s // S, 1)), jnp.tile(sin_s, (tm_s // S, 1))
    tq = _tile(T, 512)
    pp = _tile(n_pages, 32)
    ns_s = _tile(Bs, 4)
    nb_mem = _tile(Bs, 16)
    tt_p = _tile(T, 256)

    outs = [[] for _ in range(10)]
    for l in range(depth):
        lam_init = 0.8 - 0.6 * math.exp(-0.3 * l)
        row = lambda a: a[l].reshape(1, -1)
        bf = lambda a: a[l].astype(BF16)
        zpad = jnp.zeros((HEAD, RW_WIDTH), F32)
        rw = (row(rwkv_mu), row(rwkv_w0), jnp.concatenate([rwkv_w2[l], zpad], axis=0).astype(BF16), row(rwkv_a0),
              jnp.concatenate([zpad, rwkv_a2[l]], axis=0).astype(BF16), bf(rwkv_g2), row(rwkv_k_k), row(rwkv_k_a),
              row(rwkv_r_k), row(rwkv_lnx_w), row(rwkv_lnx_b))
        lamp = jnp.stack([diff_lam_q1[l], diff_lam_k1[l], diff_lam_q2[l], diff_lam_k2[l]])
        w_in_bf, w_o_bf, w_mq_bf, w_mkv_bf, w_mo_bf = bf(w_in), bf(w_o), bf(w_mq), bf(w_mkv), bf(w_mo)
        w_gu_bf, w_d_bf = bf(w_gate_up), bf(w_down)
        qn, kn, sn = row(diff_q_norm), row(diff_k_norm), row(diff_sub_norm)

        prw, qb, k, v, kb, vt = _in_proj(xp, row(norm_mix), w_in_bf, qn, kn, cos_p, sin_p, tm_p, seq_len=T)
        prw3 = prw.reshape(B, T, RW_COLS)
        r_out, z_p = _rwkv_chunked(prw3, jnp.zeros((B, 1, RW_COLS), F32), jnp.zeros((B, RW_WIDTH, LANES), F32), rw,
                                   tt_p, _tile(tt_p, RWKV_CHUNK), RWKV_TERMS)
        o = _diff_prompt(qb.reshape(B, T, -1), kb.reshape(B, T, -1), vt, lamp, sn, tq, 4, lam_init)
        mk, mv = _mem_kv(mem, row(norm_memkv), w_mkv_bf, row(mem_k_norm), _tile(M, 256))
        x2 = _mem_attn(xp.reshape(B, T, D), r_out, o, w_o_bf,
                       mk.reshape(B, M * N_MEM_HEADS, LANES), mv.reshape(B, M * N_MEM_HEADS, LANES),
                       row(norm_mem), w_mq_bf, row(mem_q_norm), w_mo_bf, 1, tm_wide)
        xp = _ffn(x2.reshape(B * T, D), row(norm_ffn), w_gu_bf, w_d_bf, tm_wide)
        outs[0].append(k.reshape(B, T, N_DIFF_HEADS, LANES))
        outs[1].append(v.reshape(B, T, N_DIFF_HEADS, LANES))
        outs[2].append(mk.reshape(B, M, N_MEM_HEADS, LANES))
        outs[3].append(mv.reshape(B, M, N_MEM_HEADS, LANES))
        outs[4].append(_state_from_blockdiag(z_p, B))
        outs[5].append(prw3[:, -1])

        prw, qb, k, v = _in_proj(xs, row(norm_mix), w_in_bf, qn, kn, cos_s, sin_s, tm_s)
        prw3 = prw.reshape(Bs, S, RW_COLS)
        r_out, st_s = _rwkv(prw3, state_shift[l][:, None, :], state_rwkv[l], rw, ns_s, S)
        o = _diff_sample(page_table, qb.reshape(Bs, S, -1), k.reshape(Bs, S * N_DIFF_HEADS, LANES),
                         v.reshape(Bs, S * N_DIFF_HEADS, LANES),
                         cache_diff_k[l].reshape(-1, PAGE * N_DIFF_HEADS, LANES),
                         cache_diff_v[l].reshape(-1, PAGE * N_DIFF_HEADS, LANES), lamp, sn, pp, lam_init)
        x2 = _mem_attn(xs.reshape(Bs, S, D), r_out, o, w_o_bf, cache_mem_k[l].reshape(Bs, M * N_MEM_HEADS, LANES),
                       cache_mem_v[l].reshape(Bs, M * N_MEM_HEADS, LANES),
                       row(norm_mem), w_mq_bf, row(mem_q_norm), w_mo_bf, nb_mem, S)
        xs = _ffn(x2.reshape(Bs * S, D), row(norm_ffn), w_gu_bf, w_d_bf, tm_s)
        outs[6].append(k.reshape(Bs, S, N_DIFF_HEADS, LANES))
        outs[7].append(v.reshape(Bs, S, N_DIFF_HEADS, LANES))
        outs[8].append(st_s)
        outs[9].append(prw3[:, -1])

    st = [jnp.stack(o) if depth > 1 else o[0][None] for o in outs]
    return (xp.reshape(B, T, D), xs.reshape(Bs, S, D), st[0], st[1], st[2], st[3], st[4], st[5],
            st[6], st[7], st[8], st[9])
```

```python
import functools
import math

import jax
import jax.numpy as jnp
from jax import lax
from jax.experimental import pallas as pl
from jax.experimental.pallas import tpu as pltpu

F32 = jnp.float32
BF16 = jnp.bfloat16

LANES = 128
HEAD = 64
RW_WIDTH = 512
RW_COLS = 3 * RW_WIDTH + 64 + 64 + 128
DIFF_WIDTH = 512
N_DIFF_HEADS = 4
MEM_WIDTH = 512
N_MEM_HEADS = 4
PAGE = 128
ROPE_THETA = 10000.0
RMS_EPS = 1e-6
LNX_EPS = 1e-5 * HEAD
NEG_BIG = -1e30
VMEM_LIMIT = 56 * 1024 * 1024
RWKV_CHUNK = 64
RWKV_TERMS = 1
RWKV_CHUNKS_PER_TRIP = 4
PAGES_PER_MATMUL = 2
QUERY_PANEL = 256
ONES_ROWS = 16
LOG2E = 1.4426950408889634


def _cparams(sem):
    return pltpu.CompilerParams(dimension_semantics=sem, vmem_limit_bytes=VMEM_LIMIT)


def _const_spec(shape):
    nd = len(shape)
    return pl.BlockSpec(shape, lambda *_: (0,) * nd)


def _nt_dot(a, b):
    return lax.dot_general(a, b, (((1,), (1,)), ((), ())), preferred_element_type=F32)


def _rms_rows(x, g):
    ms = jnp.mean(x * x, axis=-1, keepdims=True)
    return x * lax.rsqrt(ms + RMS_EPS) * g


def _lane_iota(shape):
    return lax.broadcasted_iota(jnp.int32, shape, len(shape) - 1)


def _rms_half(xh, gain):
    lo = _lane_iota(xh.shape) < HEAD
    x2 = xh * xh
    s0 = jnp.sum(jnp.where(lo, x2, 0.0), axis=-1, keepdims=True)
    s1 = jnp.sum(jnp.where(lo, 0.0, x2), axis=-1, keepdims=True)
    inv = jnp.where(lo, lax.rsqrt(s0 * (1.0 / HEAD) + RMS_EPS), lax.rsqrt(s1 * (1.0 / HEAD) + RMS_EPS))
    return xh * inv * gain


def _rope_block(xh, cos, sin_signed):
    first = (_lane_iota(xh.shape) & 32) == 0
    partner = jnp.where(first, pltpu.roll(xh, LANES - 32, 1), pltpu.roll(xh, 32, 1))
    return xh * cos + partner * sin_signed


def _rope_tables(pos):
    half = HEAD // 2
    inv = ROPE_THETA ** (-jnp.arange(half, dtype=F32) / half)
    ang = pos.astype(F32)[:, None] * inv[None, :]
    cos, sin = jnp.cos(ang), jnp.sin(ang)
    return jnp.tile(cos, (1, 4)), jnp.concatenate([-sin, sin, -sin, sin], axis=1)


def _split_hi_lo(x):
    hi = x.astype(BF16)
    lo = (x - hi.astype(F32)).astype(BF16)
    return jnp.concatenate([hi, lo], axis=1)


def _group_ones2():
    r = lax.broadcasted_iota(jnp.int32, (2 * LANES, LANES), 0)
    c = lax.broadcasted_iota(jnp.int32, (2 * LANES, LANES), 1)
    return jnp.where(((r & (LANES - 1)) >> 6) == (c >> 6), 1.0, 0.0).astype(BF16)


def _group_sum(x, g2):
    return jnp.dot(_split_hi_lo(x), g2, preferred_element_type=F32)


def _group_sum_wide(x, g2):
    return jnp.concatenate(
        [_group_sum(x[:, c * LANES:(c + 1) * LANES], g2) for c in range(x.shape[1] // LANES)], axis=1)


def _sigmoid(x):
    return 1.0 / (1.0 + jnp.exp(-x))


def _head_rows(n_rows, h, n_heads):
    return pl.ds(h, n_rows, stride=n_heads)


def _in_proj_kernel(x_ref, g_ref, w_ref, qn_ref, kn_ref, cos_ref, sin_ref,
                    prw_ref, qb_ref, k_ref, v_ref, *flash_refs):
    tm = x_ref.shape[0]
    sub = min(tm, LANES)
    qn, kn = qn_ref[...], kn_ref[...]
    for r0 in range(0, tm, sub):
        rows = slice(r0, r0 + sub)
        h = _rms_rows(x_ref[rows, :], g_ref[...]).astype(BF16)
        pd = jnp.dot(h, w_ref[:, RW_COLS:], preferred_element_type=F32)
        prw_ref[rows, :] = jnp.dot(h, w_ref[:, :RW_COLS], preferred_element_type=F32)
        cos, sin = cos_ref[rows, :], sin_ref[rows, :]
        for hh in range(N_DIFF_HEADS):
            sl = slice(hh * LANES, (hh + 1) * LANES)
            q = _rope_block(_rms_half(pd[:, sl], qn), cos, sin) * (HEAD ** -0.5 * LOG2E)
            k = _rope_block(_rms_half(pd[:, DIFF_WIDTH + hh * LANES:DIFF_WIDTH + (hh + 1) * LANES], kn), cos, sin)
            v = pd[:, 2 * DIFF_WIDTH + hh * LANES:2 * DIFF_WIDTH + (hh + 1) * LANES]
            head_rows = pl.ds(r0 * N_DIFF_HEADS + hh, sub, stride=N_DIFF_HEADS)
            qb_ref[rows, sl] = q.astype(BF16)
            k_ref[head_rows, :] = k
            v_ref[head_rows, :] = v
            if flash_refs:
                kb_ref, vt_ref = flash_refs
                kb_ref[rows, sl] = k.astype(BF16)
                vt_ref[0, sl, rows] = jnp.transpose(v).astype(BF16)


def _in_proj(x, g, w_bf, qn, kn, cos, sin, tm, seq_len=None):
    n, d = x.shape
    period = cos.shape[0] // tm
    row = lambda i: (i, 0)
    outs = [jax.ShapeDtypeStruct((n, RW_COLS), F32), jax.ShapeDtypeStruct((n, DIFF_WIDTH), BF16),
            jax.ShapeDtypeStruct((n * N_DIFF_HEADS, LANES), F32), jax.ShapeDtypeStruct((n * N_DIFF_HEADS, LANES), F32)]
    out_specs = [pl.BlockSpec((tm, RW_COLS), row), pl.BlockSpec((tm, DIFF_WIDTH), row),
                 pl.BlockSpec((tm * N_DIFF_HEADS, LANES), row), pl.BlockSpec((tm * N_DIFF_HEADS, LANES), row)]
    if seq_len is not None:
        per_seq = seq_len // tm
        outs += [jax.ShapeDtypeStruct((n, DIFF_WIDTH), BF16), jax.ShapeDtypeStruct((n // seq_len, DIFF_WIDTH, seq_len), BF16)]
        out_specs += [pl.BlockSpec((tm, DIFF_WIDTH), row),
                      pl.BlockSpec((1, DIFF_WIDTH, tm), lambda i: (i // per_seq, 0, i % per_seq))]
    return pl.pallas_call(
        _in_proj_kernel,
        grid=(n // tm,),
        in_specs=[pl.BlockSpec((tm, d), row), _const_spec((1, d)), _const_spec(w_bf.shape),
                  _const_spec((1, LANES)), _const_spec((1, LANES)),
                  pl.BlockSpec((tm, LANES), lambda i: (i % period, 0)),
                  pl.BlockSpec((tm, LANES), lambda i: (i % period, 0))],
        out_specs=tuple(out_specs),
        out_shape=tuple(outs),
        compiler_params=_cparams(("parallel",)),
        name="in_proj",
    )(x, g, w_bf, qn, kn, cos, sin)


def _rwkv_token_vectors(p, shifted, prm, g2):
    mu_ref, w0_ref, w2_ref, a0_ref, a2_ref, g2_ref, kk_ref, ka_ref, rk_ref = prm
    ps = p + (shifted - p) * mu_ref[...]
    r = ps[:, :RW_WIDTH]
    k = ps[:, RW_WIDTH:2 * RW_WIDTH]
    v = ps[:, 2 * RW_WIDTH:3 * RW_WIDTH]
    wa = ps[:, 3 * RW_WIDTH:3 * RW_WIDTH + LANES]
    gd = ps[:, 3 * RW_WIDTH + LANES:]
    z = -(w0_ref[...] + jnp.dot(jnp.tanh(wa).astype(BF16), w2_ref[...], preferred_element_type=F32))
    w_log = -(jnp.maximum(z, 0.0) + jnp.log1p(jnp.exp(-jnp.abs(z)))) - 0.5
    a = _sigmoid(a0_ref[...] + jnp.dot(wa.astype(BF16), a2_ref[...], preferred_element_type=F32))
    gate = jnp.dot(_sigmoid(gd).astype(BF16), g2_ref[...], preferred_element_type=F32)
    kk = k * kk_ref[...]
    kk = kk / jnp.maximum(jnp.sqrt(_group_sum_wide(kk * kk, g2)), 1e-12)
    k2 = k * (1.0 + (a - 1.0) * ka_ref[...])
    bonus = _group_sum_wide(r * k2 * rk_ref[...], g2) * v
    return r, -jnp.exp(w_log), k2, v, -kk, kk * a, bonus, gate


def _rwkv_finish(y, bonus, gate, lw, lb, g2):
    mean = _group_sum_wide(y, g2) * (1.0 / HEAD)
    d = y - mean
    var = _group_sum_wide(d * d, g2) * (1.0 / HEAD)
    return (d * lax.rsqrt(var + LNX_EPS) * lw + lb + bonus) * gate


def _mm(x, w, terms):
    if terms == 1:
        return jnp.dot(x.astype(BF16), w.astype(BF16), preferred_element_type=F32)
    w_hi = w.astype(BF16)
    out = jnp.dot(_split_hi_lo(x), jnp.concatenate([w_hi, w_hi], axis=0), preferred_element_type=F32)
    if terms == 3:
        w_lo = (w - w_hi.astype(F32)).astype(BF16)
        out = out + jnp.dot(x.astype(BF16), w_lo, preferred_element_type=F32)
    return out


def _stack_heads(x):
    lo = _lane_iota(x.shape) < HEAD
    return jnp.concatenate([jnp.where(lo, x, 0.0), jnp.where(lo, 0.0, x)], axis=0)


def _nt_mm(x, w, terms):
    if terms == 1:
        return _nt_dot(x.astype(BF16), w.astype(BF16))
    w_hi = w.astype(BF16)
    out = _nt_dot(_split_hi_lo(x), jnp.concatenate([w_hi, w_hi], axis=1))
    if terms == 3:
        out = out + _nt_dot(x.astype(BF16), (w - w_hi.astype(F32)).astype(BF16))
    return out


def _chunk_units(units, slots, zs, terms):
    n = len(units)
    c2 = 2 * units[0][0].shape[0]
    ti = lax.broadcasted_iota(jnp.int32, (c2, c2), 0)
    tj = lax.broadcasted_iota(jnp.int32, (c2, c2), 1)
    eye = ti == tj
    strict = (ti & (c2 // 2 - 1)) > (tj & (c2 // 2 - 1))
    incl = (ti & (c2 // 2 - 1)) >= (tj & (c2 // 2 - 1))
    s_a = [_stack_heads(u[0]) for u in units]
    s_r = [_stack_heads(u[1]) for u in units]
    s_v = [_stack_heads(u[6]) for u in units]
    sc = [_nt_mm(jnp.concatenate([s_a[i], s_r[i]], axis=0),
                 jnp.concatenate([_stack_heads(units[i][2]), _stack_heads(units[i][3])], axis=0), terms)
          for i in range(n)]
    x = [jnp.where(strict, s[:c2, :c2], 0.0) for s in sc]
    l_ak = [jnp.where(strict, s[:c2, c2:], 0.0) for s in sc]
    m_rbk = [jnp.concatenate([jnp.where(incl, s[c2:, :c2], 0.0), jnp.where(incl, s[c2:, c2:], 0.0)], axis=1) for s in sc]
    t_inv = [jnp.where(eye, 1.0, 0.0) + xi for xi in x]
    lv = [_mm(l_ak[i], s_v[i], terms) for i in range(n)]
    for _ in range(max(1, (c2 // 2 - 1).bit_length()) - 1):
        x = [_mm(xi, xi, terms) for xi in x]
        t_inv = [t_inv[i] + _mm(t_inv[i], x[i], terms) for i in range(n)]
    pq = [_mm(t_inv[i], jnp.concatenate([s_a[i], lv[i]], axis=1), terms) for i in range(n)]
    pqv = [jnp.concatenate([pq[i], jnp.concatenate([jnp.zeros_like(s_v[i]), s_v[i]], axis=1)], axis=0) for i in range(n)]
    ah = [_mm(jnp.transpose(jnp.concatenate([_stack_heads(units[i][4]), _stack_heads(units[i][5])], axis=0)), pqv[i], terms)
          for i in range(n)]
    ry = [_mm(m_rbk[i], pqv[i], terms) for i in range(n)]
    zs, ys = list(zs), []
    for i in range(n):
        z = zs[slots[i]]
        y = _mm(s_r[i] + ry[i][:, :c2], z, 3) + ry[i][:, c2:]
        ys.append(y[:c2 // 2] + y[c2 // 2:])
        zs[slots[i]] = _mm(jnp.where(eye, units[i][7], 0.0) + ah[i][:, :c2], z, 3) + ah[i][:, c2:]
    return ys, zs


def _rwkv_kernel(p_ref, prev_ref, s0_ref, mu_ref, w0_ref, w2_ref, a0_ref, a2_ref, g2_ref,
                 kk_ref, ka_ref, rk_ref, lw_ref, lb_ref,
                 out_ref, st_ref,
                 state, carry, r_s, w_s, k_s, v_s, a_s, b_s, y_s, bonus_s, gate_s, *, ns, tt):
    ti = pl.program_id(1)
    npair = RW_WIDTH // LANES
    g2 = _group_ones2()

    @pl.when(ti == 0)
    def _():
        for s in range(ns):
            for j in range(npair):
                state[(s * npair + j) * HEAD:(s * npair + j + 1) * HEAD, :] = jnp.concatenate(
                    [s0_ref[s, 2 * j], s0_ref[s, 2 * j + 1]], axis=1)
        carry[...] = prev_ref[...]

    row0 = lax.broadcasted_iota(jnp.int32, (tt, RW_COLS), 0) == 0
    prm = (mu_ref, w0_ref, w2_ref, a0_ref, a2_ref, g2_ref, kk_ref, ka_ref, rk_ref)
    for s in range(ns):
        p = p_ref[s]
        shifted = jnp.where(row0, carry[s], pltpu.roll(p, 1, 0))
        carry[s] = p[tt - 1:tt, :]
        r_s[s], log_w, k_s[s], v_s[s], a_s[s], b_s[s], bonus_s[s], gate_s[s] = _rwkv_token_vectors(p, shifted, prm, g2)
        w_s[s] = jnp.exp(log_w)

    tiles = [(s, j) for s in range(ns) for j in range(npair)]
    nt = len(tiles)
    sub = lax.broadcasted_iota(jnp.int32, (nt * HEAD, LANES), 0)
    e1 = jnp.where((_lane_iota((nt * HEAD, LANES)) & (HEAD - 1)) == (sub & (HEAD - 1)), 1.0, 0.0)
    seqs = dict(r=r_s, w=w_s, k=k_s, v=v_s, a=a_s, b=b_s)

    def block(tb, c):
        base = pl.multiple_of(tb * 8, 8)
        rows8 = {q: [ref[s, pl.ds(base, 8), j * LANES:(j + 1) * LANES] for (s, j) in tiles] for q, ref in seqs.items()}
        ys = []
        for i in range(8):
            bc = lambda q: jnp.concatenate(
                [jnp.broadcast_to(rows8[q][n][i:i + 1], (HEAD, LANES)) for n in range(nt)], axis=0)
            st = state[...]
            sa = jnp.dot((st * bc("a")).astype(BF16), g2[:LANES], preferred_element_type=F32)
            vcol = jnp.dot((e1 * bc("v")).astype(BF16), g2[:LANES], preferred_element_type=F32)
            st = st * bc("w") + sa * bc("b") + vcol * bc("k")
            state[...] = st
            yred = jnp.dot((st * bc("r")).astype(BF16), g2[:LANES], preferred_element_type=F32)
            ys.append(jnp.sum((e1 * yred).reshape(nt, HEAD, LANES), axis=1))
        for n, (s, j) in enumerate(tiles):
            y_s[s, pl.ds(base, 8), j * LANES:(j + 1) * LANES] = jnp.concatenate([y[n:n + 1] for y in ys], axis=0)
        return c

    lax.fori_loop(0, tt // 8, block, 0)

    for s in range(ns):
        out_ref[s] = _rwkv_finish(y_s[s], bonus_s[s], gate_s[s], lw_ref[...], lb_ref[...], g2)

    @pl.when(ti == pl.num_programs(1) - 1)
    def _():
        for s in range(ns):
            for j in range(npair):
                tile = state[(s * npair + j) * HEAD:(s * npair + j + 1) * HEAD, :]
                st_ref[s, 2 * j] = tile[:, :HEAD]
                st_ref[s, 2 * j + 1] = tile[:, HEAD:]


def _rwkv_chunk_kernel(p_ref, prev_ref, z0_ref, mu_ref, w0_ref, w2_ref, a0_ref, a2_ref, g2_ref,
                       kk_ref, ka_ref, rk_ref, lw_ref, lb_ref,
                       out_ref, zt_ref,
                       zstate, carry, r_s, g_s, k_s, v_s, a_s, b_s, y_s, bonus_s, gate_s, *, tt, chunk, cpt, terms):
    ti = pl.program_id(1)
    npair = RW_WIDTH // LANES
    g2 = _group_ones2()

    @pl.when(ti == 0)
    def _():
        zstate[...] = z0_ref[0]
        carry[...] = prev_ref[0]

    p = p_ref[0]
    row0 = lax.broadcasted_iota(jnp.int32, (tt, RW_COLS), 0) == 0
    shifted = jnp.where(row0, carry[...], pltpu.roll(p, 1, 0))
    carry[...] = p[tt - 1:tt, :]
    prm = (mu_ref, w0_ref, w2_ref, a0_ref, a2_ref, g2_ref, kk_ref, ka_ref, rk_ref)
    (r_s[...], g_s[...], k_s[...], v_s[...], a_s[...], b_s[...], bonus_s[...], gate_s[...]) = _rwkv_token_vectors(
        p, shifted, prm, g2)

    ci = lax.broadcasted_iota(jnp.int32, (chunk, 3 * chunk), 0)
    cj = lax.broadcasted_iota(jnp.int32, (chunk, 3 * chunk), 1)
    cj = jnp.where(cj >= 2 * chunk, cj - 2 * chunk, jnp.where(cj >= chunk, cj - chunk, cj))
    ltri3 = jnp.where(cj <= ci, 1.0, 0.0).astype(BF16)

    lanes = [slice(j * LANES, (j + 1) * LANES) for j in range(npair)]

    def scaled(c):
        rows = pl.ds(pl.multiple_of(c * chunk, chunk), chunk)
        g = g_s[rows, :]
        g_hi = g.astype(BF16)
        g_r = g - g_hi.astype(F32)
        g_mid = g_r.astype(BF16)
        g_lo = (g_r - g_mid.astype(F32)).astype(BF16)
        cum = jnp.dot(ltri3, jnp.concatenate([g_hi, g_mid, g_lo], axis=0), preferred_element_type=F32)
        total = cum[chunk - 1:chunk, :]
        e_in, e_ex, e_inv, e_rest = jnp.exp(cum), jnp.exp(cum - g), jnp.exp(-cum), jnp.exp(total - cum)
        a_t, r_t = a_s[rows, :] * e_ex, r_s[rows, :] * e_in
        b, k, v = b_s[rows, :], k_s[rows, :], v_s[rows, :]
        ops = (a_t, r_t, b * e_inv, k * e_inv, b * e_rest, k * e_rest, v, jnp.exp(total))
        return rows, [tuple(t[:, sl] for t in ops) for sl in lanes]

    def body(trip, carry_):
        parts = [scaled(trip * cpt + c) for c in range(cpt)]
        ys, z_new = _chunk_units([u for _, units in parts for u in units], list(range(npair)) * cpt,
                                 [zstate[sl, :] for sl in lanes], terms)
        for c, (rows, _) in enumerate(parts):
            for j, sl in enumerate(lanes):
                y_s[rows, sl] = ys[c * npair + j]
        for j, sl in enumerate(lanes):
            zstate[sl, :] = z_new[j]
        return carry_

    lax.fori_loop(0, tt // (cpt * chunk), body, 0)
    out_ref[0] = _rwkv_finish(y_s[...], bonus_s[...], gate_s[...], lw_ref[...], lb_ref[...], g2)

    @pl.when(ti == pl.num_programs(1) - 1)
    def _():
        zt_ref[0] = zstate[...]


def _rwkv(prw, prev, s0, rw, ns, tt):
    S, T, _ = prw.shape
    npair = RW_WIDTH // LANES
    vec = lambda n: _const_spec((1, n))
    blk = lambda w: pl.BlockSpec((ns, tt, w), lambda i, t: (i, t, 0))
    st_spec = pl.BlockSpec((ns, 2 * npair, HEAD, HEAD), lambda i, t: (i, 0, 0, 0))
    seq = lambda: pltpu.VMEM((ns, tt, RW_WIDTH), F32)
    return pl.pallas_call(
        functools.partial(_rwkv_kernel, ns=ns, tt=tt),
        grid=(S // ns, T // tt),
        in_specs=[blk(RW_COLS), pl.BlockSpec((ns, 1, RW_COLS), lambda i, t: (i, 0, 0)), st_spec,
                  vec(RW_COLS), vec(RW_WIDTH), _const_spec((LANES, RW_WIDTH)), vec(RW_WIDTH),
                  _const_spec((LANES, RW_WIDTH)), _const_spec((LANES, RW_WIDTH)),
                  vec(RW_WIDTH), vec(RW_WIDTH), vec(RW_WIDTH), vec(RW_WIDTH), vec(RW_WIDTH)],
        out_specs=(blk(RW_WIDTH), st_spec),
        out_shape=(jax.ShapeDtypeStruct((S, T, RW_WIDTH), F32),
                   jax.ShapeDtypeStruct((S, 2 * npair, HEAD, HEAD), F32)),
        scratch_shapes=[pltpu.VMEM((ns * npair * HEAD, LANES), F32), pltpu.VMEM((ns, 1, RW_COLS), F32)]
                       + [seq() for _ in range(9)],
        compiler_params=_cparams(("parallel", "arbitrary")),
        name="rwkv",
    )(prw, prev, s0, *rw)


def _rwkv_chunked(prw, prev, z0, rw, tt, chunk, terms):
    S, T, _ = prw.shape
    vec = lambda n: _const_spec((1, n))
    blk = lambda w: pl.BlockSpec((1, tt, w), lambda i, t: (i, t, 0))
    z_spec = pl.BlockSpec((1, RW_WIDTH, LANES), lambda i, t: (i, 0, 0))
    seq = lambda: pltpu.VMEM((tt, RW_WIDTH), F32)
    return pl.pallas_call(
        functools.partial(_rwkv_chunk_kernel, tt=tt, chunk=chunk, cpt=_tile(tt // chunk, RWKV_CHUNKS_PER_TRIP),
                          terms=terms),
        grid=(S, T // tt),
        in_specs=[blk(RW_COLS), pl.BlockSpec((1, 1, RW_COLS), lambda i, t: (i, 0, 0)), z_spec,
                  vec(RW_COLS), vec(RW_WIDTH), _const_spec((LANES, RW_WIDTH)), vec(RW_WIDTH),
                  _const_spec((LANES, RW_WIDTH)), _const_spec((LANES, RW_WIDTH)),
                  vec(RW_WIDTH), vec(RW_WIDTH), vec(RW_WIDTH), vec(RW_WIDTH), vec(RW_WIDTH)],
        out_specs=(blk(RW_WIDTH), z_spec),
        out_shape=(jax.ShapeDtypeStruct((S, T, RW_WIDTH), F32), jax.ShapeDtypeStruct((S, RW_WIDTH, LANES), F32)),
        scratch_shapes=[pltpu.VMEM((RW_WIDTH, LANES), F32), pltpu.VMEM((1, RW_COLS), F32)] + [seq() for _ in range(9)],
        compiler_params=_cparams(("parallel", "arbitrary")),
        name="rwkv_chunked",
    )(prw, prev, z0, *rw)


def _state_from_blockdiag(z, S):
    z = z.reshape(S, 4, 2, HEAD, 2, HEAD)
    z = jnp.stack([z[:, :, 0, :, 0, :], z[:, :, 1, :, 1, :]], axis=2)
    return z.reshape(S, 8, HEAD, HEAD).transpose(0, 1, 3, 2)


def _lambda(lamp_ref, lam_init):
    lp = lamp_ref[...]
    s1 = jnp.sum(lp[0:1] * lp[1:2], axis=-1, keepdims=True)
    s2 = jnp.sum(lp[2:3] * lp[3:4], axis=-1, keepdims=True)
    return jnp.exp(s1) - jnp.exp(s2) + lam_init


def _stack_halves(q):
    lo = _lane_iota(q.shape) < HEAD
    zero = jnp.zeros_like(q)
    return jnp.concatenate([jnp.where(lo, q, zero), jnp.where(lo, zero, q)], axis=0)


def _diff_finish(acc, l, lam, sn, lam_init, rows):
    o = acc[:rows] / l[:rows] - lam * (acc[rows:] / l[rows:])
    return _rms_rows(o, sn) * (1.0 - lam_init)


def _diff_prompt_kernel(q_ref, k_ref, vt_ref, lamp_ref, sn_ref, o_ref, m_ref, acc_ref, *, tq, hp, lam_init):
    i = pl.program_id(2)
    heads = [slice(h * LANES, (h + 1) * LANES) for h in range(hp)]
    q2 = [_stack_halves(q_ref[0, :, sl]) for sl in heads]
    ones = jnp.ones((ONES_ROWS, tq), BF16)
    panel = min(tq, QUERY_PANEL)
    units = [(h, slice(c, c + panel)) for h in range(hp) for c in range(0, 2 * tq, panel)]

    def process(off, diagonal):
        keys = [k_ref[0, pl.ds(off, tq), sl] for sl in heads]
        vals = [jnp.concatenate([vt_ref[0, sl, pl.ds(off, tq)], ones], axis=0) for sl in heads]
        first_q = [pan.start % tq for _, pan in units]
        n_keys = [min(tq, fq + panel) if diagonal else tq for fq in first_q]
        s = [_nt_dot(keys[h][:n_keys[u]], q2[h][pan]) for u, (h, pan) in enumerate(units)]
        if diagonal:
            for u in range(len(units)):
                key = lax.broadcasted_iota(jnp.int32, s[u].shape, 0)
                qry = lax.broadcasted_iota(jnp.int32, s[u].shape, 1) + first_q[u]
                s[u] = jnp.where(key <= qry, s[u], NEG_BIG)
        m_old = [m_ref[h, :, pan] for h, pan in units]
        m_new = [jnp.maximum(m_old[u], jnp.max(s[u], axis=0, keepdims=True)) for u in range(len(units))]
        alpha = [jnp.exp2(m_old[u] - m_new[u]) for u in range(len(units))]
        p = [jnp.exp2(s[u] - m_new[u]) for u in range(len(units))]
        pv = [jnp.dot(vals[h][:, :n_keys[u]], p[u].astype(BF16), preferred_element_type=F32)
              for u, (h, _) in enumerate(units)]
        for u, (h, pan) in enumerate(units):
            acc_ref[h, :, pan] = alpha[u] * acc_ref[h, :, pan] + pv[u]
            m_ref[h, :, pan] = m_new[u]

    m_ref[...] = jnp.full(m_ref.shape, NEG_BIG, F32)
    acc_ref[...] = jnp.zeros(acc_ref.shape, F32)
    process(pl.multiple_of(i * tq, tq), True)

    def body(j, c):
        process(pl.multiple_of(j * tq, tq), False)
        return c

    lax.fori_loop(0, i, body, 0)
    lam = _lambda(lamp_ref, lam_init)
    for h, sl in enumerate(heads):
        acc = acc_ref[h]
        acc = acc[:LANES] / acc[LANES:LANES + 1]
        o_t = acc[:, :tq] - lam * acc[:, tq:]
        ms = jnp.mean(o_t * o_t, axis=0, keepdims=True)
        o_t = o_t * lax.rsqrt(ms + RMS_EPS)
        o_ref[0, :, sl] = jnp.transpose(o_t) * sn_ref[...] * (1.0 - lam_init)


def _diff_prompt(qb, kb, vt, lamp, sn, tq, hp, lam_init):
    B, T, _ = qb.shape
    tile = pl.BlockSpec((1, tq, hp * LANES), lambda b, h, i: (b, i, h))
    return pl.pallas_call(
        functools.partial(_diff_prompt_kernel, tq=tq, hp=hp, lam_init=lam_init),
        grid=(B, N_DIFF_HEADS // hp, T // tq),
        in_specs=[tile, pl.BlockSpec((1, T, hp * LANES), lambda b, h, i: (b, 0, h)),
                  pl.BlockSpec((1, hp * LANES, T), lambda b, h, i: (b, h, 0)),
                  _const_spec((4, HEAD)), _const_spec((1, LANES))],
        out_specs=tile,
        out_shape=jax.ShapeDtypeStruct((B, T, DIFF_WIDTH), F32),
        scratch_shapes=[pltpu.VMEM((hp, 1, 2 * tq), F32), pltpu.VMEM((hp, LANES + ONES_ROWS, 2 * tq), F32)],
        compiler_params=_cparams(("parallel", "parallel", "arbitrary")),
        name="diff_prompt",
    )(qb, kb, vt, lamp, sn)


def _diff_sample_kernel(pt_ref, q_ref, kn_ref, vn_ref, lamp_ref, sn_ref, *rest, pp, s_new, lam_init):
    del pt_ref
    kpages, vpages = rest[:pp], rest[pp:2 * pp]
    o_ref, m_ref, l_ref, acc_ref = rest[2 * pp:]
    j = pl.program_id(1)
    nh = N_DIFF_HEADS
    rows = 2 * s_new
    q = q_ref[0]
    q2 = [_stack_halves(q[:, h * LANES:(h + 1) * LANES]) for h in range(nh)]

    def update(scores, values):
        s = jnp.concatenate([jnp.concatenate(sc, axis=1) if len(sc) > 1 else sc[0] for sc in scores], axis=0)
        m_old = m_ref[...]
        m_new = jnp.maximum(m_old, jnp.max(s, axis=-1, keepdims=True))
        alpha = jnp.exp2(m_old - m_new)
        p = jnp.exp2(s - m_new)
        pv = []
        for h in range(nh):
            ph = p[h * rows:(h + 1) * rows].astype(BF16)
            acc, col = None, 0
            for vh in values[h]:
                part = jnp.dot(ph[:, col:col + vh.shape[0]], vh, preferred_element_type=F32)
                acc = part if acc is None else acc + part
                col += vh.shape[0]
            pv.append(acc)
        l_ref[...] = alpha * l_ref[...] + jnp.sum(p, axis=-1, keepdims=True)
        acc_ref[...] = alpha * acc_ref[...] + jnp.concatenate(pv, axis=0)
        m_ref[...] = m_new

    @pl.when(j == 0)
    def _():
        m_ref[...] = jnp.full(m_ref.shape, NEG_BIG, F32)
        l_ref[...] = jnp.zeros(l_ref.shape, F32)
        acc_ref[...] = jnp.zeros(acc_ref.shape, F32)
        pad = jnp.zeros((PAGE - s_new, LANES), F32)
        row = lax.broadcasted_iota(jnp.int32, (rows, PAGE), 0)
        col = lax.broadcasted_iota(jnp.int32, (rows, PAGE), 1)
        mask = col <= jnp.where(row >= s_new, row - s_new, row)
        new_rows = lambda ref, h: jnp.concatenate([ref[0, _head_rows(s_new, h, nh), :], pad], axis=0).astype(BF16)
        update([[jnp.where(mask, _nt_dot(q2[h], new_rows(kn_ref, h)), NEG_BIG)] for h in range(nh)],
               [[new_rows(vn_ref, h)] for h in range(nh)])

    grp = PAGES_PER_MATMUL if pp % PAGES_PER_MATMUL == 0 else 1
    groups = [range(c, c + grp) for c in range(0, pp, grp)]
    head_block = lambda refs, h, g: jnp.concatenate(
        [refs[c][_head_rows(PAGE, h, nh), :].astype(BF16) for c in g], axis=0)
    update([[_nt_dot(q2[h], head_block(kpages, h, g)) for g in groups] for h in range(nh)],
           [[head_block(vpages, h, g) for g in groups] for h in range(nh)])

    @pl.when(j == pl.num_programs(1) - 1)
    def _():
        lam = _lambda(lamp_ref, lam_init)
        acc, l = acc_ref[...], l_ref[...]
        for h in range(nh):
            o_ref[0, :, h * LANES:(h + 1) * LANES] = _diff_finish(
                acc[h * rows:(h + 1) * rows], l[h * rows:(h + 1) * rows], lam, sn_ref[...], lam_init, s_new)


def _diff_sample(page_table, qb, k_new, v_new, pool_k, pool_v, lamp, sn, pp, lam_init):
    Bs, S, _ = qb.shape
    n_pages = page_table.shape[1]
    nh = N_DIFF_HEADS
    qspec = pl.BlockSpec((1, S, DIFF_WIDTH), lambda b, j, pt: (b, 0, 0))
    new = pl.BlockSpec((1, S * nh, LANES), lambda b, j, pt: (b, 0, 0))
    page = lambda c: pl.BlockSpec((None, PAGE * nh, LANES), lambda b, j, pt, c=c: (pt[b, j * pp + c], 0, 0))
    cst = lambda shape: pl.BlockSpec(shape, lambda b, j, pt: (0,) * len(shape))
    return pl.pallas_call(
        functools.partial(_diff_sample_kernel, pp=pp, s_new=S, lam_init=lam_init),
        grid_spec=pltpu.PrefetchScalarGridSpec(
            num_scalar_prefetch=1,
            grid=(Bs, n_pages // pp),
            in_specs=[qspec, new, new, cst((4, HEAD)), cst((1, LANES))] + [page(c) for c in range(pp)] * 2,
            out_specs=qspec,
            scratch_shapes=[pltpu.VMEM((nh * 2 * S, 1), F32), pltpu.VMEM((nh * 2 * S, 1), F32),
                            pltpu.VMEM((nh * 2 * S, LANES), F32)]),
        out_shape=jax.ShapeDtypeStruct((Bs, S, DIFF_WIDTH), F32),
        compiler_params=_cparams(("parallel", "arbitrary")),
        name="diff_sample",
    )(page_table, qb, k_new, v_new, lamp, sn, *([pool_k] * pp), *([pool_v] * pp))


def _mem_kv_kernel(x_ref, g_ref, w_ref, kn_ref, k_ref, v_ref):
    h = _rms_rows(x_ref[...], g_ref[...]).astype(BF16)
    kv = jnp.dot(h, w_ref[...], preferred_element_type=F32)
    tm = x_ref.shape[0]
    for hh in range(N_MEM_HEADS):
        sl = slice(hh * LANES, (hh + 1) * LANES)
        k_ref[_head_rows(tm, hh, N_MEM_HEADS), :] = _rms_rows(kv[:, sl], kn_ref[...])
        v_ref[_head_rows(tm, hh, N_MEM_HEADS), :] = kv[:, MEM_WIDTH + hh * LANES:MEM_WIDTH + (hh + 1) * LANES]


def _mem_kv(mem, g, w_bf, kn, tm):
    n, d = mem.shape
    row = lambda i: (i, 0)
    return pl.pallas_call(
        _mem_kv_kernel,
        grid=(n // tm,),
        in_specs=[pl.BlockSpec((tm, d), row), _const_spec((1, d)), _const_spec(w_bf.shape), _const_spec((1, LANES))],
        out_specs=(pl.BlockSpec((tm * N_MEM_HEADS, LANES), row),) * 2,
        out_shape=(jax.ShapeDtypeStruct((n * N_MEM_HEADS, LANES), F32),) * 2,
        compiler_params=_cparams(("parallel",)),
        name="mem_kv",
    )(mem, g, w_bf, kn)


def _mem_attn_kernel(x_ref, r_ref, a_ref, wmix_ref, mk_ref, mv_ref, g_ref, wq_ref, qn_ref, wo_ref, y_ref, o_scr,
                     *, nb, tr, n_mem):
    d = x_ref.shape[-1]
    rows = nb * tr
    x = x_ref[...].reshape(rows, d)
    x = x + jnp.dot(r_ref[...].reshape(rows, RW_WIDTH).astype(BF16), wmix_ref[:RW_WIDTH, :], preferred_element_type=F32)
    x = x + jnp.dot(a_ref[...].reshape(rows, DIFF_WIDTH).astype(BF16), wmix_ref[RW_WIDTH:, :], preferred_element_type=F32)
    h = _rms_rows(x, g_ref[...]).astype(BF16)
    q = jnp.dot(h, wq_ref[...], preferred_element_type=F32)
    heads = [slice(hh * LANES, (hh + 1) * LANES) for hh in range(N_MEM_HEADS)]
    qh = [(_rms_rows(q[:, sl], qn_ref[...]) * (LANES ** -0.5 * LOG2E)).astype(BF16) for sl in heads]
    units = [(hh, n) for hh in range(N_MEM_HEADS) for n in range(nb)]
    mem_rows = lambda ref, hh, n: ref[n, _head_rows(n_mem, hh, N_MEM_HEADS), :].astype(BF16)
    s = [_nt_dot(qh[hh][n * tr:(n + 1) * tr], mem_rows(mk_ref, hh, n)) for hh, n in units]
    p = [jnp.exp2(su - jnp.max(su, axis=-1, keepdims=True)) for su in s]
    o = [jnp.dot(p[u].astype(BF16), mem_rows(mv_ref, hh, n), preferred_element_type=F32) for u, (hh, n) in enumerate(units)]
    for u, (hh, n) in enumerate(units):
        o_scr[n * tr:(n + 1) * tr, heads[hh]] = o[u] / jnp.sum(p[u], axis=-1, keepdims=True)
    y = x + jnp.dot(o_scr[...].astype(BF16), wo_ref[...], preferred_element_type=F32)
    y_ref[...] = y.reshape(nb, tr, d)


def _mem_attn(x, r_out, attn, wmix_bf, mk, mv, g, wq_bf, qn, wo_bf, nb, tr):
    G, R, d = x.shape
    M4 = mk.shape[1]
    tile = lambda w: pl.BlockSpec((nb, tr, w), lambda i, t: (i, t, 0))
    mem = pl.BlockSpec((nb, M4, LANES), lambda i, t: (i, 0, 0))
    return pl.pallas_call(
        functools.partial(_mem_attn_kernel, nb=nb, tr=tr, n_mem=M4 // N_MEM_HEADS),
        grid=(G // nb, R // tr),
        in_specs=[tile(d), tile(RW_WIDTH), tile(DIFF_WIDTH), _const_spec(wmix_bf.shape), mem, mem,
                  _const_spec((1, d)), _const_spec(wq_bf.shape), _const_spec((1, LANES)), _const_spec(wo_bf.shape)],
        out_specs=tile(d),
        out_shape=jax.ShapeDtypeStruct((G, R, d), F32),
        scratch_shapes=[pltpu.VMEM((nb * tr, MEM_WIDTH), F32)],
        compiler_params=_cparams(("parallel", "arbitrary")),
        name="mem_attn",
    )(x, r_out, attn, wmix_bf, mk, mv, g, wq_bf, qn, wo_bf)


def _ffn_kernel(x_ref, g_ref, wgu_ref, wd_ref, y_ref, *, d_ff, fc):
    x = x_ref[...]
    h = _rms_rows(x, g_ref[...]).astype(BF16)
    y = x
    for c in range(d_ff // fc):
        gt = jnp.dot(h, wgu_ref[:, c * fc:(c + 1) * fc], preferred_element_type=F32)
        up = jnp.dot(h, wgu_ref[:, d_ff + c * fc:d_ff + (c + 1) * fc], preferred_element_type=F32)
        act = (gt * _sigmoid(gt) * up).astype(BF16)
        y = y + jnp.dot(act, wd_ref[c * fc:(c + 1) * fc, :], preferred_element_type=F32)
    y_ref[...] = y


def _ffn(x, g, wgu_bf, wd_bf, tm):
    n, d = x.shape
    d_ff = wd_bf.shape[0]
    fc = d_ff // 2 if (d_ff // 2) % LANES == 0 else d_ff
    row = lambda i: (i, 0)
    return pl.pallas_call(
        functools.partial(_ffn_kernel, d_ff=d_ff, fc=fc),
        grid=(n // tm,),
        in_specs=[pl.BlockSpec((tm, d), row), _const_spec((1, d)), _const_spec(wgu_bf.shape), _const_spec(wd_bf.shape)],
        out_specs=pl.BlockSpec((tm, d), row),
        out_shape=jax.ShapeDtypeStruct((n, d), F32),
        compiler_params=_cparams(("parallel",)),
        name="ffn",
    )(x, g, wgu_bf, wd_bf)


def _tile(n, pref):
    t = min(n, pref)
    while n % t:
        t //= 2
    return t


def kernel(x_prompt, x_sample, cache_diff_k, cache_diff_v, cache_mem_k, cache_mem_v, state_rwkv, state_shift, page_table, mem_prompt, norm_mix, w_in, rwkv_mu, rwkv_w0, rwkv_w2, rwkv_a0, rwkv_a2, rwkv_g2, rwkv_k_k, rwkv_k_a, rwkv_r_k, rwkv_lnx_w, rwkv_lnx_b, diff_q_norm, diff_k_norm, diff_lam_q1, diff_lam_k1, diff_lam_q2, diff_lam_k2, diff_sub_norm, w_o, norm_mem, norm_memkv, w_mq, w_mkv, mem_q_norm, mem_k_norm, w_mo, norm_ffn, w_gate_up, w_down):
    B, T, D = x_prompt.shape
    Bs, S, _ = x_sample.shape
    depth = w_in.shape[0]
    n_pages = page_table.shape[1]
    past = n_pages * PAGE
    M = mem_prompt.shape[1]
    xp = x_prompt.reshape(B * T, D)
    xs = x_sample.reshape(Bs * S, D)
    mem = mem_prompt.reshape(B * M, D)

    tm_p = _tile(T, 512)
    tm_wide = _tile(T, 512)
    tm_s = _tile(Bs * S, 256)
    cos_p, sin_p = _rope_tables(jnp.arange(T, dtype=jnp.int32))
    cos_s, sin_s = _rope_tables(past + jnp.arange(S, dtype=jnp.int32))
    cos_s, sin_s = jnp.tile(cos_s, (tm_s // S, 1)), jnp.tile(sin_s, (tm_s // S, 1))
    tq = _tile(T, 512)
    pp = _tile(n_pages, 32)
    ns_s = _tile(Bs, 8)
    nb_mem = _tile(Bs, 16)
    tt_p = _tile(T, 256)

    outs = [[] for _ in range(10)]
    for l in range(depth):
        lam_init = 0.8 - 0.6 * math.exp(-0.3 * l)
        row = lambda a: a[l].reshape(1, -1)
        bf = lambda a: a[l].astype(BF16)
        zpad = jnp.zeros((HEAD, RW_WIDTH), F32)
        rw = (row(rwkv_mu), row(rwkv_w0), jnp.concatenate([rwkv_w2[l], zpad], axis=0).astype(BF16), row(rwkv_a0),
              jnp.concatenate([zpad, rwkv_a2[l]], axis=0).astype(BF16), bf(rwkv_g2), row(rwkv_k_k), row(rwkv_k_a),
              row(rwkv_r_k), row(rwkv_lnx_w), row(rwkv_lnx_b))
        lamp = jnp.stack([diff_lam_q1[l], diff_lam_k1[l], diff_lam_q2[l], diff_lam_k2[l]])
        w_in_bf, w_o_bf, w_mq_bf, w_mkv_bf, w_mo_bf = bf(w_in), bf(w_o), bf(w_mq), bf(w_mkv), bf(w_mo)
        w_gu_bf, w_d_bf = bf(w_gate_up), bf(w_down)
        qn, kn, sn = row(diff_q_norm), row(diff_k_norm), row(diff_sub_norm)

        prw, qb, k, v, kb, vt = _in_proj(xp, row(norm_mix), w_in_bf, qn, kn, cos_p, sin_p, tm_p, seq_len=T)
        prw3 = prw.reshape(B, T, RW_COLS)
        r_out, z_p = _rwkv_chunked(prw3, jnp.zeros((B, 1, RW_COLS), F32), jnp.zeros((B, RW_WIDTH, LANES), F32), rw,
                                   tt_p, _tile(tt_p, RWKV_CHUNK), RWKV_TERMS)
        o = _diff_prompt(qb.reshape(B, T, -1), kb.reshape(B, T, -1), vt, lamp, sn, tq, 4, lam_init)
        mk, mv = _mem_kv(mem, row(norm_memkv), w_mkv_bf, row(mem_k_norm), _tile(M, 256))
        x2 = _mem_attn(xp.reshape(B, T, D), r_out, o, w_o_bf,
                       mk.reshape(B, M * N_MEM_HEADS, LANES), mv.reshape(B, M * N_MEM_HEADS, LANES),
                       row(norm_mem), w_mq_bf, row(mem_q_norm), w_mo_bf, 1, tm_wide)
        xp = _ffn(x2.reshape(B * T, D), row(norm_ffn), w_gu_bf, w_d_bf, tm_wide)
        outs[0].append(k.reshape(B, T, N_DIFF_HEADS, LANES))
        outs[1].append(v.reshape(B, T, N_DIFF_HEADS, LANES))
        outs[2].append(mk.reshape(B, M, N_MEM_HEADS, LANES))
        outs[3].append(mv.reshape(B, M, N_MEM_HEADS, LANES))
        outs[4].append(_state_from_blockdiag(z_p, B))
        outs[5].append(prw3[:, -1])

        prw, qb, k, v = _in_proj(xs, row(norm_mix), w_in_bf, qn, kn, cos_s, sin_s, tm_s)
        prw3 = prw.reshape(Bs, S, RW_COLS)
        r_out, st_s = _rwkv(prw3, state_shift[l][:, None, :], state_rwkv[l], rw, ns_s, S)
        o = _diff_sample(page_table, qb.reshape(Bs, S, -1), k.reshape(Bs, S * N_DIFF_HEADS, LANES),
                         v.reshape(Bs, S * N_DIFF_HEADS, LANES),
                         cache_diff_k[l].reshape(-1, PAGE * N_DIFF_HEADS, LANES),
                         cache_diff_v[l].reshape(-1, PAGE * N_DIFF_HEADS, LANES), lamp, sn, pp, lam_init)
        x2 = _mem_attn(xs.reshape(Bs, S, D), r_out, o, w_o_bf, cache_mem_k[l].reshape(Bs, M * N_MEM_HEADS, LANES),
                       cache_mem_v[l].reshape(Bs, M * N_MEM_HEADS, LANES),
                       row(norm_mem), w_mq_bf, row(mem_q_norm), w_mo_bf, nb_mem, S)
        xs = _ffn(x2.reshape(Bs * S, D), row(norm_ffn), w_gu_bf, w_d_bf, tm_s)
        outs[6].append(k.reshape(Bs, S, N_DIFF_HEADS, LANES))
        outs[7].append(v.reshape(Bs, S, N_DIFF_HEADS, LANES))
        outs[8].append(st_s)
        outs[9].append(prw3[:, -1])

    st = [jnp.stack(o) if depth > 1 else o[0][None] for o in outs]
    return (xp.reshape(B, T, D), xs.reshape(Bs, S, D), st[0], st[1], st[2], st[3], st[4], st[5],
            st[6], st[7], st[8], st[9])
```

```python
import functools
import math

import jax
import jax.numpy as jnp
from jax import lax
from jax.experimental import pallas as pl
from jax.experimental.pallas import tpu as pltpu

F32 = jnp.float32
BF16 = jnp.bfloat16

LANES = 128
HEAD = 64
RW_WIDTH = 512
RW_COLS = 3 * RW_WIDTH + 64 + 64 + 128
DIFF_WIDTH = 512
N_DIFF_HEADS = 4
MEM_WIDTH = 512
N_MEM_HEADS = 4
PAGE = 128
ROPE_THETA = 10000.0
RMS_EPS = 1e-6
LNX_EPS = 1e-5 * HEAD
NEG_BIG = -1e30
VMEM_LIMIT = 56 * 1024 * 1024
RWKV_CHUNK = 64
RWKV_TERMS = 1
RWKV_CHUNKS_PER_TRIP = 4
PAGES_PER_MATMUL = 2
QUERY_PANEL = 256
ONES_ROWS = 16
LOG2E = 1.4426950408889634


def _cparams(sem):
    return pltpu.CompilerParams(dimension_semantics=sem, vmem_limit_bytes=VMEM_LIMIT)


def _const_spec(shape):
    nd = len(shape)
    return pl.BlockSpec(shape, lambda *_: (0,) * nd)


def _nt_dot(a, b):
    return lax.dot_general(a, b, (((1,), (1,)), ((), ())), preferred_element_type=F32)


def _rms_rows(x, g):
    ms = jnp.mean(x * x, axis=-1, keepdims=True)
    return x * lax.rsqrt(ms + RMS_EPS) * g


def _lane_iota(shape):
    return lax.broadcasted_iota(jnp.int32, shape, len(shape) - 1)


def _rms_half(xh, gain):
    lo = _lane_iota(xh.shape) < HEAD
    x2 = xh * xh
    s0 = jnp.sum(jnp.where(lo, x2, 0.0), axis=-1, keepdims=True)
    s1 = jnp.sum(jnp.where(lo, 0.0, x2), axis=-1, keepdims=True)
    inv = jnp.where(lo, lax.rsqrt(s0 * (1.0 / HEAD) + RMS_EPS), lax.rsqrt(s1 * (1.0 / HEAD) + RMS_EPS))
    return xh * inv * gain


def _rope_block(xh, cos, sin_signed):
    first = (_lane_iota(xh.shape) & 32) == 0
    partner = jnp.where(first, pltpu.roll(xh, LANES - 32, 1), pltpu.roll(xh, 32, 1))
    return xh * cos + partner * sin_signed


def _rope_tables(pos):
    half = HEAD // 2
    inv = ROPE_THETA ** (-jnp.arange(half, dtype=F32) / half)
    ang = pos.astype(F32)[:, None] * inv[None, :]
    cos, sin = jnp.cos(ang), jnp.sin(ang)
    return jnp.tile(cos, (1, 4)), jnp.concatenate([-sin, sin, -sin, sin], axis=1)


def _split_hi_lo(x):
    hi = x.astype(BF16)
    lo = (x - hi.astype(F32)).astype(BF16)
    return jnp.concatenate([hi, lo], axis=1)


def _group_ones2():
    r = lax.broadcasted_iota(jnp.int32, (2 * LANES, LANES), 0)
    c = lax.broadcasted_iota(jnp.int32, (2 * LANES, LANES), 1)
    return jnp.where(((r & (LANES - 1)) >> 6) == (c >> 6), 1.0, 0.0).astype(BF16)


def _group_sum(x, g2):
    return jnp.dot(_split_hi_lo(x), g2, preferred_element_type=F32)


def _group_sum_wide(x, g2):
    return jnp.concatenate(
        [_group_sum(x[:, c * LANES:(c + 1) * LANES], g2) for c in range(x.shape[1] // LANES)], axis=1)


def _sigmoid(x):
    return 1.0 / (1.0 + jnp.exp(-x))


def _head_rows(n_rows, h, n_heads):
    return pl.ds(h, n_rows, stride=n_heads)


def _in_proj_kernel(x_ref, g_ref, w_ref, qn_ref, kn_ref, cos_ref, sin_ref,
                    prw_ref, qb_ref, k_ref, v_ref, *flash_refs):
    tm = x_ref.shape[0]
    sub = min(tm, LANES)
    qn, kn = qn_ref[...], kn_ref[...]
    for r0 in range(0, tm, sub):
        rows = slice(r0, r0 + sub)
        h = _rms_rows(x_ref[rows, :], g_ref[...]).astype(BF16)
        pd = jnp.dot(h, w_ref[:, RW_COLS:], preferred_element_type=F32)
        prw_ref[rows, :] = jnp.dot(h, w_ref[:, :RW_COLS], preferred_element_type=F32)
        cos, sin = cos_ref[rows, :], sin_ref[rows, :]
        for hh in range(N_DIFF_HEADS):
            sl = slice(hh * LANES, (hh + 1) * LANES)
            q = _rope_block(_rms_half(pd[:, sl], qn), cos, sin) * (HEAD ** -0.5 * LOG2E)
            k = _rope_block(_rms_half(pd[:, DIFF_WIDTH + hh * LANES:DIFF_WIDTH + (hh + 1) * LANES], kn), cos, sin)
            v = pd[:, 2 * DIFF_WIDTH + hh * LANES:2 * DIFF_WIDTH + (hh + 1) * LANES]
            head_rows = pl.ds(r0 * N_DIFF_HEADS + hh, sub, stride=N_DIFF_HEADS)
            qb_ref[rows, sl] = q.astype(BF16)
            k_ref[head_rows, :] = k
            v_ref[head_rows, :] = v
            if flash_refs:
                kb_ref, vt_ref = flash_refs
                kb_ref[rows, sl] = k.astype(BF16)
                vt_ref[0, sl, rows] = jnp.transpose(v).astype(BF16)


def _in_proj(x, g, w_bf, qn, kn, cos, sin, tm, seq_len=None):
    n, d = x.shape
    period = cos.shape[0] // tm
    row = lambda i: (i, 0)
    outs = [jax.ShapeDtypeStruct((n, RW_COLS), F32), jax.ShapeDtypeStruct((n, DIFF_WIDTH), BF16),
            jax.ShapeDtypeStruct((n * N_DIFF_HEADS, LANES), F32), jax.ShapeDtypeStruct((n * N_DIFF_HEADS, LANES), F32)]
    out_specs = [pl.BlockSpec((tm, RW_COLS), row), pl.BlockSpec((tm, DIFF_WIDTH), row),
                 pl.BlockSpec((tm * N_DIFF_HEADS, LANES), row), pl.BlockSpec((tm * N_DIFF_HEADS, LANES), row)]
    if seq_len is not None:
        per_seq = seq_len // tm
        outs += [jax.ShapeDtypeStruct((n, DIFF_WIDTH), BF16), jax.ShapeDtypeStruct((n // seq_len, DIFF_WIDTH, seq_len), BF16)]
        out_specs += [pl.BlockSpec((tm, DIFF_WIDTH), row),
                      pl.BlockSpec((1, DIFF_WIDTH, tm), lambda i: (i // per_seq, 0, i % per_seq))]
    return pl.pallas_call(
        _in_proj_kernel,
        grid=(n // tm,),
        in_specs=[pl.BlockSpec((tm, d), row), _const_spec((1, d)), _const_spec(w_bf.shape),
                  _const_spec((1, LANES)), _const_spec((1, LANES)),
                  pl.BlockSpec((tm, LANES), lambda i: (i % period, 0)),
                  pl.BlockSpec((tm, LANES), lambda i: (i % period, 0))],
        out_specs=tuple(out_specs),
        out_shape=tuple(outs),
        compiler_params=_cparams(("parallel",)),
        name="in_proj",
    )(x, g, w_bf, qn, kn, cos, sin)


def _rwkv_token_vectors(p, shifted, prm, g2):
    mu_ref, w0_ref, w2_ref, a0_ref, a2_ref, g2_ref, kk_ref, ka_ref, rk_ref = prm
    ps = p + (shifted - p) * mu_ref[...]
    r = ps[:, :RW_WIDTH]
    k = ps[:, RW_WIDTH:2 * RW_WIDTH]
    v = ps[:, 2 * RW_WIDTH:3 * RW_WIDTH]
    wa = ps[:, 3 * RW_WIDTH:3 * RW_WIDTH + LANES]
    gd = ps[:, 3 * RW_WIDTH + LANES:]
    z = -(w0_ref[...] + jnp.dot(jnp.tanh(wa).astype(BF16), w2_ref[...], preferred_element_type=F32))
    w_log = -(jnp.maximum(z, 0.0) + jnp.log1p(jnp.exp(-jnp.abs(z)))) - 0.5
    a = _sigmoid(a0_ref[...] + jnp.dot(wa.astype(BF16), a2_ref[...], preferred_element_type=F32))
    gate = jnp.dot(_sigmoid(gd).astype(BF16), g2_ref[...], preferred_element_type=F32)
    kk = k * kk_ref[...]
    kk = kk / jnp.maximum(jnp.sqrt(_group_sum_wide(kk * kk, g2)), 1e-12)
    k2 = k * (1.0 + (a - 1.0) * ka_ref[...])
    bonus = _group_sum_wide(r * k2 * rk_ref[...], g2) * v
    return r, -jnp.exp(w_log), k2, v, -kk, kk * a, bonus, gate


def _rwkv_finish(y, bonus, gate, lw, lb, g2):
    mean = _group_sum_wide(y, g2) * (1.0 / HEAD)
    d = y - mean
    var = _group_sum_wide(d * d, g2) * (1.0 / HEAD)
    return (d * lax.rsqrt(var + LNX_EPS) * lw + lb + bonus) * gate


def _mm(x, w, terms):
    if terms == 1:
        return jnp.dot(x.astype(BF16), w.astype(BF16), preferred_element_type=F32)
    w_hi = w.astype(BF16)
    out = jnp.dot(_split_hi_lo(x), jnp.concatenate([w_hi, w_hi], axis=0), preferred_element_type=F32)
    if terms == 3:
        w_lo = (w - w_hi.astype(F32)).astype(BF16)
        out = out + jnp.dot(x.astype(BF16), w_lo, preferred_element_type=F32)
    return out


def _stack_heads(x):
    lo = _lane_iota(x.shape) < HEAD
    return jnp.concatenate([jnp.where(lo, x, 0.0), jnp.where(lo, 0.0, x)], axis=0)


def _nt_mm(x, w, terms):
    if terms == 1:
        return _nt_dot(x.astype(BF16), w.astype(BF16))
    w_hi = w.astype(BF16)
    out = _nt_dot(_split_hi_lo(x), jnp.concatenate([w_hi, w_hi], axis=1))
    if terms == 3:
        out = out + _nt_dot(x.astype(BF16), (w - w_hi.astype(F32)).astype(BF16))
    return out


def _chunk_units(units, slots, zs, terms):
    n = len(units)
    c2 = 2 * units[0][0].shape[0]
    ti = lax.broadcasted_iota(jnp.int32, (c2, c2), 0)
    tj = lax.broadcasted_iota(jnp.int32, (c2, c2), 1)
    eye = ti == tj
    strict = (ti & (c2 // 2 - 1)) > (tj & (c2 // 2 - 1))
    incl = (ti & (c2 // 2 - 1)) >= (tj & (c2 // 2 - 1))
    s_a = [_stack_heads(u[0]) for u in units]
    s_r = [_stack_heads(u[1]) for u in units]
    s_v = [_stack_heads(u[6]) for u in units]
    sc = [_nt_mm(jnp.concatenate([s_a[i], s_r[i]], axis=0),
                 jnp.concatenate([_stack_heads(units[i][2]), _stack_heads(units[i][3])], axis=0), terms)
          for i in range(n)]
    x = [jnp.where(strict, s[:c2, :c2], 0.0) for s in sc]
    l_ak = [jnp.where(strict, s[:c2, c2:], 0.0) for s in sc]
    m_rbk = [jnp.concatenate([jnp.where(incl, s[c2:, :c2], 0.0), jnp.where(incl, s[c2:, c2:], 0.0)], axis=1) for s in sc]
    t_inv = [jnp.where(eye, 1.0, 0.0) + xi for xi in x]
    lv = [_mm(l_ak[i], s_v[i], terms) for i in range(n)]
    for _ in range(max(1, (c2 // 2 - 1).bit_length()) - 1):
        x = [_mm(xi, xi, terms) for xi in x]
        t_inv = [t_inv[i] + _mm(t_inv[i], x[i], terms) for i in range(n)]
    pq = [_mm(t_inv[i], jnp.concatenate([s_a[i], lv[i]], axis=1), terms) for i in range(n)]
    pqv = [jnp.concatenate([pq[i], jnp.concatenate([jnp.zeros_like(s_v[i]), s_v[i]], axis=1)], axis=0) for i in range(n)]
    ah = [_mm(jnp.transpose(jnp.concatenate([_stack_heads(units[i][4]), _stack_heads(units[i][5])], axis=0)), pqv[i], terms)
          for i in range(n)]
    ry = [_mm(m_rbk[i], pqv[i], terms) for i in range(n)]
    zs, ys = list(zs), []
    for i in range(n):
        z = zs[slots[i]]
        y = _mm(s_r[i] + ry[i][:, :c2], z, 2) + ry[i][:, c2:]
        ys.append(y[:c2 // 2] + y[c2 // 2:])
        zs[slots[i]] = _mm(jnp.where(eye, units[i][7], 0.0) + ah[i][:, :c2], z, 2) + ah[i][:, c2:]
    return ys, zs


def _rwkv_kernel(p_ref, prev_ref, s0_ref, mu_ref, w0_ref, w2_ref, a0_ref, a2_ref, g2_ref,
                 kk_ref, ka_ref, rk_ref, lw_ref, lb_ref,
                 out_ref, st_ref,
                 state, carry, r_s, w_s, k_s, v_s, a_s, b_s, y_s, bonus_s, gate_s, *, ns, tt):
    ti = pl.program_id(1)
    npair = RW_WIDTH // LANES
    g2 = _group_ones2()

    @pl.when(ti == 0)
    def _():
        for s in range(ns):
            for j in range(npair):
                state[(s * npair + j) * HEAD:(s * npair + j + 1) * HEAD, :] = jnp.concatenate(
                    [s0_ref[s, 2 * j], s0_ref[s, 2 * j + 1]], axis=1)
        carry[...] = prev_ref[...]

    row0 = lax.broadcasted_iota(jnp.int32, (tt, RW_COLS), 0) == 0
    prm = (mu_ref, w0_ref, w2_ref, a0_ref, a2_ref, g2_ref, kk_ref, ka_ref, rk_ref)
    for s in range(ns):
        p = p_ref[s]
        shifted = jnp.where(row0, carry[s], pltpu.roll(p, 1, 0))
        carry[s] = p[tt - 1:tt, :]
        r_s[s], log_w, k_s[s], v_s[s], a_s[s], b_s[s], bonus_s[s], gate_s[s] = _rwkv_token_vectors(p, shifted, prm, g2)
        w_s[s] = jnp.exp(log_w)

    tiles = [(s, j) for s in range(ns) for j in range(npair)]
    nt = len(tiles)
    sub = lax.broadcasted_iota(jnp.int32, (nt * HEAD, LANES), 0)
    e1 = jnp.where((_lane_iota((nt * HEAD, LANES)) & (HEAD - 1)) == (sub & (HEAD - 1)), 1.0, 0.0)
    seqs = dict(r=r_s, w=w_s, k=k_s, v=v_s, a=a_s, b=b_s)

    def block(tb, c):
        base = pl.multiple_of(tb * 8, 8)
        rows8 = {q: [ref[s, pl.ds(base, 8), j * LANES:(j + 1) * LANES] for (s, j) in tiles] for q, ref in seqs.items()}
        ys = []
        for i in range(8):
            bc = lambda q: jnp.concatenate(
                [jnp.broadcast_to(rows8[q][n][i:i + 1], (HEAD, LANES)) for n in range(nt)], axis=0)
            st = state[...]
            sa = jnp.dot((st * bc("a")).astype(BF16), g2[:LANES], preferred_element_type=F32)
            vcol = jnp.dot((e1 * bc("v")).astype(BF16), g2[:LANES], preferred_element_type=F32)
            st = st * bc("w") + sa * bc("b") + vcol * bc("k")
            state[...] = st
            yred = jnp.dot((st * bc("r")).astype(BF16), g2[:LANES], preferred_element_type=F32)
            ys.append(jnp.sum((e1 * yred).reshape(nt, HEAD, LANES), axis=1))
        for n, (s, j) in enumerate(tiles):
            y_s[s, pl.ds(base, 8), j * LANES:(j + 1) * LANES] = jnp.concatenate([y[n:n + 1] for y in ys], axis=0)
        return c

    lax.fori_loop(0, tt // 8, block, 0)

    for s in range(ns):
        out_ref[s] = _rwkv_finish(y_s[s], bonus_s[s], gate_s[s], lw_ref[...], lb_ref[...], g2)

    @pl.when(ti == pl.num_programs(1) - 1)
    def _():
        for s in range(ns):
            for j in range(npair):
                tile = state[(s * npair + j) * HEAD:(s * npair + j + 1) * HEAD, :]
                st_ref[s, 2 * j] = tile[:, :HEAD]
                st_ref[s, 2 * j + 1] = tile[:, HEAD:]


def _rwkv_chunk_kernel(p_ref, prev_ref, z0_ref, mu_ref, w0_ref, w2_ref, a0_ref, a2_ref, g2_ref,
                       kk_ref, ka_ref, rk_ref, lw_ref, lb_ref,
                       out_ref, zt_ref,
                       zstate, carry, r_s, g_s, k_s, v_s, a_s, b_s, y_s, bonus_s, gate_s, *, tt, chunk, cpt, terms):
    ti = pl.program_id(1)
    npair = RW_WIDTH // LANES
    g2 = _group_ones2()

    @pl.when(ti == 0)
    def _():
        zstate[...] = z0_ref[0]
        carry[...] = prev_ref[0]

    p = p_ref[0]
    row0 = lax.broadcasted_iota(jnp.int32, (tt, RW_COLS), 0) == 0
    shifted = jnp.where(row0, carry[...], pltpu.roll(p, 1, 0))
    carry[...] = p[tt - 1:tt, :]
    prm = (mu_ref, w0_ref, w2_ref, a0_ref, a2_ref, g2_ref, kk_ref, ka_ref, rk_ref)
    (r_s[...], g_s[...], k_s[...], v_s[...], a_s[...], b_s[...], bonus_s[...], gate_s[...]) = _rwkv_token_vectors(
        p, shifted, prm, g2)

    ci = lax.broadcasted_iota(jnp.int32, (chunk, 3 * chunk), 0)
    cj = lax.broadcasted_iota(jnp.int32, (chunk, 3 * chunk), 1)
    cj = jnp.where(cj >= 2 * chunk, cj - 2 * chunk, jnp.where(cj >= chunk, cj - chunk, cj))
    ltri3 = jnp.where(cj <= ci, 1.0, 0.0).astype(BF16)

    lanes = [slice(j * LANES, (j + 1) * LANES) for j in range(npair)]

    def scaled(c):
        rows = pl.ds(pl.multiple_of(c * chunk, chunk), chunk)
        g = g_s[rows, :]
        g_hi = g.astype(BF16)
        g_r = g - g_hi.astype(F32)
        g_mid = g_r.astype(BF16)
        g_lo = (g_r - g_mid.astype(F32)).astype(BF16)
        cum = jnp.dot(ltri3, jnp.concatenate([g_hi, g_mid, g_lo], axis=0), preferred_element_type=F32)
        total = cum[chunk - 1:chunk, :]
        e_in, e_ex, e_inv, e_rest = jnp.exp(cum), jnp.exp(cum - g), jnp.exp(-cum), jnp.exp(total - cum)
        a_t, r_t = a_s[rows, :] * e_ex, r_s[rows, :] * e_in
        b, k, v = b_s[rows, :], k_s[rows, :], v_s[rows, :]
        ops = (a_t, r_t, b * e_inv, k * e_inv, b * e_rest, k * e_rest, v, jnp.exp(total))
        return rows, [tuple(t[:, sl] for t in ops) for sl in lanes]

    def body(trip, carry_):
        parts = [scaled(trip * cpt + c) for c in range(cpt)]
        ys, z_new = _chunk_units([u for _, units in parts for u in units], list(range(npair)) * cpt,
                                 [zstate[sl, :] for sl in lanes], terms)
        for c, (rows, _) in enumerate(parts):
            for j, sl in enumerate(lanes):
                y_s[rows, sl] = ys[c * npair + j]
        for j, sl in enumerate(lanes):
            zstate[sl, :] = z_new[j]
        return carry_

    lax.fori_loop(0, tt // (cpt * chunk), body, 0)
    out_ref[0] = _rwkv_finish(y_s[...], bonus_s[...], gate_s[...], lw_ref[...], lb_ref[...], g2)

    @pl.when(ti == pl.num_programs(1) - 1)
    def _():
        zt_ref[0] = zstate[...]


def _rwkv(prw, prev, s0, rw, ns, tt):
    S, T, _ = prw.shape
    npair = RW_WIDTH // LANES
    vec = lambda n: _const_spec((1, n))
    blk = lambda w: pl.BlockSpec((ns, tt, w), lambda i, t: (i, t, 0))
    st_spec = pl.BlockSpec((ns, 2 * npair, HEAD, HEAD), lambda i, t: (i, 0, 0, 0))
    seq = lambda: pltpu.VMEM((ns, tt, RW_WIDTH), F32)
    return pl.pallas_call(
        functools.partial(_rwkv_kernel, ns=ns, tt=tt),
        grid=(S // ns, T // tt),
        in_specs=[blk(RW_COLS), pl.BlockSpec((ns, 1, RW_COLS), lambda i, t: (i, 0, 0)), st_spec,
                  vec(RW_COLS), vec(RW_WIDTH), _const_spec((LANES, RW_WIDTH)), vec(RW_WIDTH),
                  _const_spec((LANES, RW_WIDTH)), _const_spec((LANES, RW_WIDTH)),
                  vec(RW_WIDTH), vec(RW_WIDTH), vec(RW_WIDTH), vec(RW_WIDTH), vec(RW_WIDTH)],
        out_specs=(blk(RW_WIDTH), st_spec),
        out_shape=(jax.ShapeDtypeStruct((S, T, RW_WIDTH), F32),
                   jax.ShapeDtypeStruct((S, 2 * npair, HEAD, HEAD), F32)),
        scratch_shapes=[pltpu.VMEM((ns * npair * HEAD, LANES), F32), pltpu.VMEM((ns, 1, RW_COLS), F32)]
                       + [seq() for _ in range(9)],
        compiler_params=_cparams(("parallel", "arbitrary")),
        name="rwkv",
    )(prw, prev, s0, *rw)


def _rwkv_chunked(prw, prev, z0, rw, tt, chunk, terms):
    S, T, _ = prw.shape
    vec = lambda n: _const_spec((1, n))
    blk = lambda w: pl.BlockSpec((1, tt, w), lambda i, t: (i, t, 0))
    z_spec = pl.BlockSpec((1, RW_WIDTH, LANES), lambda i, t: (i, 0, 0))
    seq = lambda: pltpu.VMEM((tt, RW_WIDTH), F32)
    return pl.pallas_call(
        functools.partial(_rwkv_chunk_kernel, tt=tt, chunk=chunk, cpt=_tile(tt // chunk, RWKV_CHUNKS_PER_TRIP),
                          terms=terms),
        grid=(S, T // tt),
        in_specs=[blk(RW_COLS), pl.BlockSpec((1, 1, RW_COLS), lambda i, t: (i, 0, 0)), z_spec,
                  vec(RW_COLS), vec(RW_WIDTH), _const_spec((LANES, RW_WIDTH)), vec(RW_WIDTH),
                  _const_spec((LANES, RW_WIDTH)), _const_spec((LANES, RW_WIDTH)),
                  vec(RW_WIDTH), vec(RW_WIDTH), vec(RW_WIDTH), vec(RW_WIDTH), vec(RW_WIDTH)],
        out_specs=(blk(RW_WIDTH), z_spec),
        out_shape=(jax.ShapeDtypeStruct((S, T, RW_WIDTH), F32), jax.ShapeDtypeStruct((S, RW_WIDTH, LANES), F32)),
        scratch_shapes=[pltpu.VMEM((RW_WIDTH, LANES), F32), pltpu.VMEM((1, RW_COLS), F32)] + [seq() for _ in range(9)],
        compiler_params=_cparams(("parallel", "arbitrary")),
        name="rwkv_chunked",
    )(prw, prev, z0, *rw)


def _state_from_blockdiag(z, S):
    z = z.reshape(S, 4, 2, HEAD, 2, HEAD)
    z = jnp.stack([z[:, :, 0, :, 0, :], z[:, :, 1, :, 1, :]], axis=2)
    return z.reshape(S, 8, HEAD, HEAD).transpose(0, 1, 3, 2)


def _lambda(lamp_ref, lam_init):
    lp = lamp_ref[...]
    s1 = jnp.sum(lp[0:1] * lp[1:2], axis=-1, keepdims=True)
    s2 = jnp.sum(lp[2:3] * lp[3:4], axis=-1, keepdims=True)
    return jnp.exp(s1) - jnp.exp(s2) + lam_init


def _stack_halves(q):
    lo = _lane_iota(q.shape) < HEAD
    zero = jnp.zeros_like(q)
    return jnp.concatenate([jnp.where(lo, q, zero), jnp.where(lo, zero, q)], axis=0)


def _diff_finish(acc, l, lam, sn, lam_init, rows):
    o = acc[:rows] / l[:rows] - lam * (acc[rows:] / l[rows:])
    return _rms_rows(o, sn) * (1.0 - lam_init)


def _diff_prompt_kernel(q_ref, k_ref, vt_ref, lamp_ref, sn_ref, o_ref, m_ref, acc_ref, *, tq, hp, lam_init):
    i = pl.program_id(2)
    heads = [slice(h * LANES, (h + 1) * LANES) for h in range(hp)]
    q2 = [_stack_halves(q_ref[0, :, sl]) for sl in heads]
    ones = jnp.ones((ONES_ROWS, tq), BF16)
    panel = min(tq, QUERY_PANEL)
    units = [(h, slice(c, c + panel)) for h in range(hp) for c in range(0, 2 * tq, panel)]

    def process(off, diagonal):
        keys = [k_ref[0, pl.ds(off, tq), sl] for sl in heads]
        vals = [jnp.concatenate([vt_ref[0, sl, pl.ds(off, tq)], ones], axis=0) for sl in heads]
        first_q = [pan.start % tq for _, pan in units]
        n_keys = [min(tq, fq + panel) if diagonal else tq for fq in first_q]
        s = [_nt_dot(keys[h][:n_keys[u]], q2[h][pan]) for u, (h, pan) in enumerate(units)]
        if diagonal:
            for u in range(len(units)):
                key = lax.broadcasted_iota(jnp.int32, s[u].shape, 0)
                qry = lax.broadcasted_iota(jnp.int32, s[u].shape, 1) + first_q[u]
                s[u] = jnp.where(key <= qry, s[u], NEG_BIG)
        m_old = [m_ref[h, :, pan] for h, pan in units]
        m_new = [jnp.maximum(m_old[u], jnp.max(s[u], axis=0, keepdims=True)) for u in range(len(units))]
        alpha = [jnp.exp2(m_old[u] - m_new[u]) for u in range(len(units))]
        p = [jnp.exp2(s[u] - m_new[u]) for u in range(len(units))]
        pv = [jnp.dot(vals[h][:, :n_keys[u]], p[u].astype(BF16), preferred_element_type=F32)
              for u, (h, _) in enumerate(units)]
        for u, (h, pan) in enumerate(units):
            acc_ref[h, :, pan] = alpha[u] * acc_ref[h, :, pan] + pv[u]
            m_ref[h, :, pan] = m_new[u]

    m_ref[...] = jnp.full(m_ref.shape, NEG_BIG, F32)
    acc_ref[...] = jnp.zeros(acc_ref.shape, F32)
    process(pl.multiple_of(i * tq, tq), True)

    def body(j, c):
        process(pl.multiple_of(j * tq, tq), False)
        return c

    lax.fori_loop(0, i, body, 0)
    lam = _lambda(lamp_ref, lam_init)
    for h, sl in enumerate(heads):
        acc = acc_ref[h]
        acc = acc[:LANES] / acc[LANES:LANES + 1]
        o_t = acc[:, :tq] - lam * acc[:, tq:]
        ms = jnp.mean(o_t * o_t, axis=0, keepdims=True)
        o_t = o_t * lax.rsqrt(ms + RMS_EPS)
        o_ref[0, :, sl] = jnp.transpose(o_t) * sn_ref[...] * (1.0 - lam_init)


def _diff_prompt(qb, kb, vt, lamp, sn, tq, hp, lam_init):
    B, T, _ = qb.shape
    tile = pl.BlockSpec((1, tq, hp * LANES), lambda b, h, i: (b, i, h))
    return pl.pallas_call(
        functools.partial(_diff_prompt_kernel, tq=tq, hp=hp, lam_init=lam_init),
        grid=(B, N_DIFF_HEADS // hp, T // tq),
        in_specs=[tile, pl.BlockSpec((1, T, hp * LANES), lambda b, h, i: (b, 0, h)),
                  pl.BlockSpec((1, hp * LANES, T), lambda b, h, i: (b, h, 0)),
                  _const_spec((4, HEAD)), _const_spec((1, LANES))],
        out_specs=tile,
        out_shape=jax.ShapeDtypeStruct((B, T, DIFF_WIDTH), F32),
        scratch_shapes=[pltpu.VMEM((hp, 1, 2 * tq), F32), pltpu.VMEM((hp, LANES + ONES_ROWS, 2 * tq), F32)],
        compiler_params=_cparams(("parallel", "parallel", "arbitrary")),
        name="diff_prompt",
    )(qb, kb, vt, lamp, sn)


def _diff_sample_kernel(pt_ref, q_ref, kn_ref, vn_ref, lamp_ref, sn_ref, *rest, pp, s_new, lam_init):
    del pt_ref
    kpages, vpages = rest[:pp], rest[pp:2 * pp]
    o_ref, m_ref, l_ref, acc_ref = rest[2 * pp:]
    j = pl.program_id(1)
    nh = N_DIFF_HEADS
    rows = 2 * s_new
    q = q_ref[0]
    q2 = [_stack_halves(q[:, h * LANES:(h + 1) * LANES]) for h in range(nh)]

    def update(scores, values):
        s = jnp.concatenate([jnp.concatenate(sc, axis=1) if len(sc) > 1 else sc[0] for sc in scores], axis=0)
        m_old = m_ref[...]
        m_new = jnp.maximum(m_old, jnp.max(s, axis=-1, keepdims=True))
        alpha = jnp.exp2(m_old - m_new)
        p = jnp.exp2(s - m_new)
        pv = []
        for h in range(nh):
            ph = p[h * rows:(h + 1) * rows].astype(BF16)
            acc, col = None, 0
            for vh in values[h]:
                part = jnp.dot(ph[:, col:col + vh.shape[0]], vh, preferred_element_type=F32)
                acc = part if acc is None else acc + part
                col += vh.shape[0]
            pv.append(acc)
        l_ref[...] = alpha * l_ref[...] + jnp.sum(p, axis=-1, keepdims=True)
        acc_ref[...] = alpha * acc_ref[...] + jnp.concatenate(pv, axis=0)
        m_ref[...] = m_new

    @pl.when(j == 0)
    def _():
        m_ref[...] = jnp.full(m_ref.shape, NEG_BIG, F32)
        l_ref[...] = jnp.zeros(l_ref.shape, F32)
        acc_ref[...] = jnp.zeros(acc_ref.shape, F32)
        pad = jnp.zeros((PAGE - s_new, LANES), F32)
        row = lax.broadcasted_iota(jnp.int32, (rows, PAGE), 0)
        col = lax.broadcasted_iota(jnp.int32, (rows, PAGE), 1)
        mask = col <= jnp.where(row >= s_new, row - s_new, row)
        new_rows = lambda ref, h: jnp.concatenate([ref[0, _head_rows(s_new, h, nh), :], pad], axis=0).astype(BF16)
        update([[jnp.where(mask, _nt_dot(q2[h], new_rows(kn_ref, h)), NEG_BIG)] for h in range(nh)],
               [[new_rows(vn_ref, h)] for h in range(nh)])

    grp = PAGES_PER_MATMUL if pp % PAGES_PER_MATMUL == 0 else 1
    groups = [range(c, c + grp) for c in range(0, pp, grp)]
    head_block = lambda refs, h, g: jnp.concatenate(
        [refs[c][_head_rows(PAGE, h, nh), :].astype(BF16) for c in g], axis=0)
    update([[_nt_dot(q2[h], head_block(kpages, h, g)) for g in groups] for h in range(nh)],
           [[head_block(vpages, h, g) for g in groups] for h in range(nh)])

    @pl.when(j == pl.num_programs(1) - 1)
    def _():
        lam = _lambda(lamp_ref, lam_init)
        acc, l = acc_ref[...], l_ref[...]
        for h in range(nh):
            o_ref[0, :, h * LANES:(h + 1) * LANES] = _diff_finish(
                acc[h * rows:(h + 1) * rows], l[h * rows:(h + 1) * rows], lam, sn_ref[...], lam_init, s_new)


def _diff_sample(page_table, qb, k_new, v_new, pool_k, pool_v, lamp, sn, pp, lam_init):
    Bs, S, _ = qb.shape
    n_pages = page_table.shape[1]
    nh = N_DIFF_HEADS
    qspec = pl.BlockSpec((1, S, DIFF_WIDTH), lambda b, j, pt: (b, 0, 0))
    new = pl.BlockSpec((1, S * nh, LANES), lambda b, j, pt: (b, 0, 0))
    page = lambda c: pl.BlockSpec((None, PAGE * nh, LANES), lambda b, j, pt, c=c: (pt[b, j * pp + c], 0, 0))
    cst = lambda shape: pl.BlockSpec(shape, lambda b, j, pt: (0,) * len(shape))
    return pl.pallas_call(
        functools.partial(_diff_sample_kernel, pp=pp, s_new=S, lam_init=lam_init),
        grid_spec=pltpu.PrefetchScalarGridSpec(
            num_scalar_prefetch=1,
            grid=(Bs, n_pages // pp),
            in_specs=[qspec, new, new, cst((4, HEAD)), cst((1, LANES))] + [page(c) for c in range(pp)] * 2,
            out_specs=qspec,
            scratch_shapes=[pltpu.VMEM((nh * 2 * S, 1), F32), pltpu.VMEM((nh * 2 * S, 1), F32),
                            pltpu.VMEM((nh * 2 * S, LANES), F32)]),
        out_shape=jax.ShapeDtypeStruct((Bs, S, DIFF_WIDTH), F32),
        compiler_params=_cparams(("parallel", "arbitrary")),
        name="diff_sample",
    )(page_table, qb, k_new, v_new, lamp, sn, *([pool_k] * pp), *([pool_v] * pp))


def _mem_kv_kernel(x_ref, g_ref, w_ref, kn_ref, k_ref, v_ref):
    h = _rms_rows(x_ref[...], g_ref[...]).astype(BF16)
    kv = jnp.dot(h, w_ref[...], preferred_element_type=F32)
    tm = x_ref.shape[0]
    for hh in range(N_MEM_HEADS):
        sl = slice(hh * LANES, (hh + 1) * LANES)
        k_ref[_head_rows(tm, hh, N_MEM_HEADS), :] = _rms_rows(kv[:, sl], kn_ref[...])
        v_ref[_head_rows(tm, hh, N_MEM_HEADS), :] = kv[:, MEM_WIDTH + hh * LANES:MEM_WIDTH + (hh + 1) * LANES]


def _mem_kv(mem, g, w_bf, kn, tm):
    n, d = mem.shape
    row = lambda i: (i, 0)
    return pl.pallas_call(
        _mem_kv_kernel,
        grid=(n // tm,),
        in_specs=[pl.BlockSpec((tm, d), row), _const_spec((1, d)), _const_spec(w_bf.shape), _const_spec((1, LANES))],
        out_specs=(pl.BlockSpec((tm * N_MEM_HEADS, LANES), row),) * 2,
        out_shape=(jax.ShapeDtypeStruct((n * N_MEM_HEADS, LANES), F32),) * 2,
        compiler_params=_cparams(("parallel",)),
        name="mem_kv",
    )(mem, g, w_bf, kn)


def _mem_attn_kernel(x_ref, r_ref, a_ref, wmix_ref, mk_ref, mv_ref, g_ref, wq_ref, qn_ref, wo_ref, y_ref, o_scr,
                     *, nb, tr, n_mem):
    d = x_ref.shape[-1]
    rows = nb * tr
    x = x_ref[...].reshape(rows, d)
    x = x + jnp.dot(r_ref[...].reshape(rows, RW_WIDTH).astype(BF16), wmix_ref[:RW_WIDTH, :], preferred_element_type=F32)
    x = x + jnp.dot(a_ref[...].reshape(rows, DIFF_WIDTH).astype(BF16), wmix_ref[RW_WIDTH:, :], preferred_element_type=F32)
    h = _rms_rows(x, g_ref[...]).astype(BF16)
    q = jnp.dot(h, wq_ref[...], preferred_element_type=F32)
    heads = [slice(hh * LANES, (hh + 1) * LANES) for hh in range(N_MEM_HEADS)]
    qh = [(_rms_rows(q[:, sl], qn_ref[...]) * (LANES ** -0.5 * LOG2E)).astype(BF16) for sl in heads]
    units = [(hh, n) for hh in range(N_MEM_HEADS) for n in range(nb)]
    mem_rows = lambda ref, hh, n: ref[n, _head_rows(n_mem, hh, N_MEM_HEADS), :].astype(BF16)
    s = [_nt_dot(qh[hh][n * tr:(n + 1) * tr], mem_rows(mk_ref, hh, n)) for hh, n in units]
    p = [jnp.exp2(su - jnp.max(su, axis=-1, keepdims=True)) for su in s]
    o = [jnp.dot(p[u].astype(BF16), mem_rows(mv_ref, hh, n), preferred_element_type=F32) for u, (hh, n) in enumerate(units)]
    for u, (hh, n) in enumerate(units):
        o_scr[n * tr:(n + 1) * tr, heads[hh]] = o[u] / jnp.sum(p[u], axis=-1, keepdims=True)
    y = x + jnp.dot(o_scr[...].astype(BF16), wo_ref[...], preferred_element_type=F32)
    y_ref[...] = y.reshape(nb, tr, d)


def _mem_attn(x, r_out, attn, wmix_bf, mk, mv, g, wq_bf, qn, wo_bf, nb, tr):
    G, R, d = x.shape
    M4 = mk.shape[1]
    tile = lambda w: pl.BlockSpec((nb, tr, w), lambda i, t: (i, t, 0))
    mem = pl.BlockSpec((nb, M4, LANES), lambda i, t: (i, 0, 0))
    return pl.pallas_call(
        functools.partial(_mem_attn_kernel, nb=nb, tr=tr, n_mem=M4 // N_MEM_HEADS),
        grid=(G // nb, R // tr),
        in_specs=[tile(d), tile(RW_WIDTH), tile(DIFF_WIDTH), _const_spec(wmix_bf.shape), mem, mem,
                  _const_spec((1, d)), _const_spec(wq_bf.shape), _const_spec((1, LANES)), _const_spec(wo_bf.shape)],
        out_specs=tile(d),
        out_shape=jax.ShapeDtypeStruct((G, R, d), F32),
        scratch_shapes=[pltpu.VMEM((nb * tr, MEM_WIDTH), F32)],
        compiler_params=_cparams(("parallel", "arbitrary")),
        name="mem_attn",
    )(x, r_out, attn, wmix_bf, mk, mv, g, wq_bf, qn, wo_bf)


def _ffn_kernel(x_ref, g_ref, wgu_ref, wd_ref, y_ref, *, d_ff, fc):
    x = x_ref[...]
    h = _rms_rows(x, g_ref[...]).astype(BF16)
    y = x
    for c in range(d_ff // fc):
        gt = jnp.dot(h, wgu_ref[:, c * fc:(c + 1) * fc], preferred_element_type=F32)
        up = jnp.dot(h, wgu_ref[:, d_ff + c * fc:d_ff + (c + 1) * fc], preferred_element_type=F32)
        act = (gt * _sigmoid(gt) * up).astype(BF16)
        y = y + jnp.dot(act, wd_ref[c * fc:(c + 1) * fc, :], preferred_element_type=F32)
    y_ref[...] = y


def _ffn(x, g, wgu_bf, wd_bf, tm):
    n, d = x.shape
    d_ff = wd_bf.shape[0]
    fc = d_ff // 2 if (d_ff // 2) % LANES == 0 else d_ff
    row = lambda i: (i, 0)
    return pl.pallas_call(
        functools.partial(_ffn_kernel, d_ff=d_ff, fc=fc),
        grid=(n // tm,),
        in_specs=[pl.BlockSpec((tm, d), row), _const_spec((1, d)), _const_spec(wgu_bf.shape), _const_spec(wd_bf.shape)],
        out_specs=pl.BlockSpec((tm, d), row),
        out_shape=jax.ShapeDtypeStruct((n, d), F32),
        compiler_params=_cparams(("parallel",)),
        name="ffn",
    )(x, g, wgu_bf, wd_bf)


def _tile(n, pref):
    t = min(n, pref)
    while n % t:
        t //= 2
    return t


def kernel(x_prompt, x_sample, cache_diff_k, cache_diff_v, cache_mem_k, cache_mem_v, state_rwkv, state_shift, page_table, mem_prompt, norm_mix, w_in, rwkv_mu, rwkv_w0, rwkv_w2, rwkv_a0, rwkv_a2, rwkv_g2, rwkv_k_k, rwkv_k_a, rwkv_r_k, rwkv_lnx_w, rwkv_lnx_b, diff_q_norm, diff_k_norm, diff_lam_q1, diff_lam_k1, diff_lam_q2, diff_lam_k2, diff_sub_norm, w_o, norm_mem, norm_memkv, w_mq, w_mkv, mem_q_norm, mem_k_norm, w_mo, norm_ffn, w_gate_up, w_down):
    B, T, D = x_prompt.shape
    Bs, S, _ = x_sample.shape
    depth = w_in.shape[0]
    n_pages = page_table.shape[1]
    past = n_pages * PAGE
    M = mem_prompt.shape[1]
    xp = x_prompt.reshape(B * T, D)
    xs = x_sample.reshape(Bs * S, D)
    mem = mem_prompt.reshape(B * M, D)

    tm_p = _tile(T, 512)
    tm_wide = _tile(T, 512)
    tm_s = _tile(Bs * S, 256)
    cos_p, sin_p = _rope_tables(jnp.arange(T, dtype=jnp.int32))
    cos_s, sin_s = _rope_tables(past + jnp.arange(S, dtype=jnp.int32))
    cos_s, sin_s = jnp.tile(cos_s, (tm_s // S, 1)), jnp.tile(sin_s, (tm_s // S, 1))
    tq = _tile(T, 512)
    pp = _tile(n_pages, 32)
    ns_s = _tile(Bs, 8)
    nb_mem = _tile(Bs, 16)
    tt_p = _tile(T, 512)

    outs = [[] for _ in range(10)]
    for l in range(depth):
        lam_init = 0.8 - 0.6 * math.exp(-0.3 * l)
        row = lambda a: a[l].reshape(1, -1)
        bf = lambda a: a[l].astype(BF16)
        zpad = jnp.zeros((HEAD, RW_WIDTH), F32)
        rw = (row(rwkv_mu), row(rwkv_w0), jnp.concatenate([rwkv_w2[l], zpad], axis=0).astype(BF16), row(rwkv_a0),
              jnp.concatenate([zpad, rwkv_a2[l]], axis=0).astype(BF16), bf(rwkv_g2), row(rwkv_k_k), row(rwkv_k_a),
              row(rwkv_r_k), row(rwkv_lnx_w), row(rwkv_lnx_b))
        lamp = jnp.stack([diff_lam_q1[l], diff_lam_k1[l], diff_lam_q2[l], diff_lam_k2[l]])
        w_in_bf, w_o_bf, w_mq_bf, w_mkv_bf, w_mo_bf = bf(w_in), bf(w_o), bf(w_mq), bf(w_mkv), bf(w_mo)
        w_gu_bf, w_d_bf = bf(w_gate_up), bf(w_down)
        qn, kn, sn = row(diff_q_norm), row(diff_k_norm), row(diff_sub_norm)

        prw, qb, k, v, kb, vt = _in_proj(xp, row(norm_mix), w_in_bf, qn, kn, cos_p, sin_p, tm_p, seq_len=T)
        prw3 = prw.reshape(B, T, RW_COLS)
        r_out, z_p = _rwkv_chunked(prw3, jnp.zeros((B, 1, RW_COLS), F32), jnp.zeros((B, RW_WIDTH, LANES), F32), rw,
                                   tt_p, _tile(tt_p, RWKV_CHUNK), RWKV_TERMS)
        o = _diff_prompt(qb.reshape(B, T, -1), kb.reshape(B, T, -1), vt, lamp, sn, tq, 4, lam_init)
        mk, mv = _mem_kv(mem, row(norm_memkv), w_mkv_bf, row(mem_k_norm), _tile(M, 256))
        x2 = _mem_attn(xp.reshape(B, T, D), r_out, o, w_o_bf,
                       mk.reshape(B, M * N_MEM_HEADS, LANES), mv.reshape(B, M * N_MEM_HEADS, LANES),
                       row(norm_mem), w_mq_bf, row(mem_q_norm), w_mo_bf, 1, tm_wide)
        xp = _ffn(x2.reshape(B * T, D), row(norm_ffn), w_gu_bf, w_d_bf, tm_wide)
        outs[0].append(k.reshape(B, T, N_DIFF_HEADS, LANES))
        outs[1].append(v.reshape(B, T, N_DIFF_HEADS, LANES))
        outs[2].append(mk.reshape(B, M, N_MEM_HEADS, LANES))
        outs[3].append(mv.reshape(B, M, N_MEM_HEADS, LANES))
        outs[4].append(_state_from_blockdiag(z_p, B))
        outs[5].append(prw3[:, -1])

        prw, qb, k, v = _in_proj(xs, row(norm_mix), w_in_bf, qn, kn, cos_s, sin_s, tm_s)
        prw3 = prw.reshape(Bs, S, RW_COLS)
        r_out, st_s = _rwkv(prw3, state_shift[l][:, None, :], state_rwkv[l], rw, ns_s, S)
        o = _diff_sample(page_table, qb.reshape(Bs, S, -1), k.reshape(Bs, S * N_DIFF_HEADS, LANES),
                         v.reshape(Bs, S * N_DIFF_HEADS, LANES),
                         cache_diff_k[l].reshape(-1, PAGE * N_DIFF_HEADS, LANES),
                         cache_diff_v[l].reshape(-1, PAGE * N_DIFF_HEADS, LANES), lamp, sn, pp, lam_init)
        x2 = _mem_attn(xs.reshape(Bs, S, D), r_out, o, w_o_bf, cache_mem_k[l].reshape(Bs, M * N_MEM_HEADS, LANES),
                       cache_mem_v[l].reshape(Bs, M * N_MEM_HEADS, LANES),
                       row(norm_mem), w_mq_bf, row(mem_q_norm), w_mo_bf, nb_mem, S)
        xs = _ffn(x2.reshape(Bs * S, D), row(norm_ffn), w_gu_bf, w_d_bf, tm_s)
        outs[6].append(k.reshape(Bs, S, N_DIFF_HEADS, LANES))
        outs[7].append(v.reshape(Bs, S, N_DIFF_HEADS, LANES))
        outs[8].append(st_s)
        outs[9].append(prw3[:, -1])

    st = [jnp.stack(o) if depth > 1 else o[0][None] for o in outs]
    return (xp.reshape(B, T, D), xs.reshape(Bs, S, D), st[0], st[1], st[2], st[3], st[4], st[5],
            st[6], st[7], st[8], st[9])
```
